```python
import math
import jax, jax.numpy as jnp
from jax import lax
import numpy as np

D_MODEL = 1024
BATCH = 2
SEQ = 8192
DEPTH = 2

GRID_W = 64
CTX_LEN = 256
D_FF = 4 * D_MODEL
EPS = 1e-6
N_EVEN = (DEPTH + 1) // 2
N_ODD = DEPTH // 2
MLSTM_HEADS = 6
MLSTM_DH = 128
D_A = MLSTM_HEADS * MLSTM_DH
D_B = D_MODEL - D_A
FNET_GROUPS = 4
FNET_GC = D_B // FNET_GROUPS
MLSTM_CHUNK = 64
QK_CONV_W = 3
EVEN_IN = 4 * D_A + 4 * MLSTM_HEADS + D_B
DIFF_HEADS = 6
DIFF_DH = 64
DIFF_DV = 2 * DIFF_DH
D_C = DIFF_HEADS * DIFF_DV
D_D = D_MODEL - D_C
CONV_W = 31
ODD_IN = 3 * D_C + 2 * D_D
Q_BLOCK = 128
ROPE_BASE = 10000.0

kernel_name = "hybrid_mlstm_fnet_diffattn_conformer_dit"


def rmsnorm(x, g):
    xf = x.astype(jnp.float32)
    y = xf * lax.rsqrt(jnp.mean(xf * xf, axis=-1, keepdims=True) + EPS)
    return (y * g.astype(jnp.float32)).astype(x.dtype)


def layernorm(x, g, b):
    xf = x.astype(jnp.float32)
    mu = jnp.mean(xf, axis=-1, keepdims=True)
    var = jnp.mean(jnp.square(xf - mu), axis=-1, keepdims=True)
    y = (xf - mu) * lax.rsqrt(var + EPS)
    return (y * g.astype(jnp.float32) + b.astype(jnp.float32)).astype(x.dtype)


def modulate(h, shift, scale):
    return h * (1 + scale) + shift


def squared_relu_mlp(h, w1, w2):
    return jnp.square(jax.nn.relu(h @ w1)) @ w2


def depthwise_conv(x, w):
    pad = (w.shape[0] - 1) // 2
    return lax.conv_general_dilated(
        x, w.astype(x.dtype)[:, None, :], window_strides=(1,), padding=[(pad, pad)],
        dimension_numbers=('NWC', 'WIO', 'NWC'), feature_group_count=x.shape[-1])


def heads_first(a, n):
    b_, t_, _ = a.shape
    return a.reshape(b_, t_, n, -1).transpose(0, 2, 1, 3)


def head_rmsnorm(h, g, n):
    b_, t_, dm = h.shape
    hf = h.astype(jnp.float32).reshape(b_, t_, n, dm // n)
    hf = hf * lax.rsqrt(jnp.mean(hf * hf, axis=-1, keepdims=True) + EPS)
    return (hf.reshape(b_, t_, dm) * g.astype(jnp.float32)).astype(h.dtype)


def fourier_mix(f):
    b_, t_, _ = f.shape
    g = f.astype(jnp.float32).reshape(b_, t_, FNET_GROUPS, FNET_GC)
    y = jnp.fft.fft2(g, axes=(1, 3), norm='ortho').real
    return y.reshape(b_, t_, D_B).astype(f.dtype)


def axial_rope(t_len, dh):
    rows = t_len // GRID_W
    row = jnp.repeat(jnp.arange(rows, dtype=jnp.float32), GRID_W)
    col = jnp.tile(jnp.arange(GRID_W, dtype=jnp.float32), rows)
    n_freq = dh // 4
    inv = ROPE_BASE ** (-jnp.arange(n_freq, dtype=jnp.float32) / n_freq)
    ang = jnp.concatenate([row[:, None] * inv, col[:, None] * inv], axis=-1)
    return jnp.cos(ang), jnp.sin(ang)


def apply_rope(x, cos, sin):
    cos = cos[None, :, None, None, :]
    sin = sin[None, :, None, None, :]
    x1, x2 = jnp.split(x.astype(jnp.float32), 2, axis=-1)
    return jnp.concatenate([x1 * cos - x2 * sin, x2 * cos + x1 * sin], axis=-1).astype(x.dtype)


def mlstm_scan(q, k, v, logi, logf, state):
    b_, h_, t_, d_ = q.shape
    nc = t_ // MLSTM_CHUNK

    def to_chunks(a):
        a = a.reshape(a.shape[:2] + (nc, MLSTM_CHUNK) + a.shape[3:])
        return jnp.moveaxis(a, 2, 0)

    causal = jnp.tril(jnp.ones((MLSTM_CHUNK, MLSTM_CHUNK), dtype=bool))

    def step(carry, xs):
        c_mem, n_mem, m_prev = carry
        qc, kc, vc, ic, fc = xs
        bcum = jnp.cumsum(fc, axis=-1)
        dlog = bcum[..., :, None] - bcum[..., None, :] + ic[..., None, :]
        dlog = jnp.where(causal, dlog, -jnp.inf)
        m_t = jnp.maximum(bcum + m_prev[..., None], jnp.max(dlog, axis=-1))
        s = jnp.einsum('bhtd,bhsd->bhts', qc, kc) * jnp.exp(dlog - m_t[..., None])
        inter = jnp.exp(bcum + m_prev[..., None] - m_t)
        num = jnp.einsum('bhts,bhsd->bhtd', s, vc) + inter[..., None] * jnp.einsum('bhvk,bhtk->bhtv', c_mem, qc)
        den = jnp.sum(s, axis=-1) + inter * jnp.einsum('bhk,bhtk->bht', n_mem, qc)
        h = num / jnp.maximum(jnp.abs(den), jnp.exp(-m_t))[..., None]
        g = bcum[..., -1:] - bcum + ic
        m_new = jnp.maximum(bcum[..., -1] + m_prev, jnp.max(g, axis=-1))
        w = jnp.exp(g - m_new[..., None])
        decay = jnp.exp(bcum[..., -1] + m_prev - m_new)
        c_new = decay[..., None, None] * c_mem + jnp.einsum('bhs,bhsv,bhsk->bhvk', w, vc, kc)
        n_new = decay[..., None] * n_mem + jnp.einsum('bhs,bhsk->bhk', w, kc)
        return (c_new, n_new, m_new), h

    state, hs = lax.scan(step, state, (to_chunks(q), to_chunks(k), to_chunks(v), to_chunks(logi), to_chunks(logf)))
    h = jnp.moveaxis(hs, 0, 2).reshape(b_, h_, t_, d_)
    return h, state


def even_mixer(h_lat, h_ctx, w_in, b_gate, w_qk_conv, g_head, w_out, need_ctx_out):
    f32 = jnp.float32

    def project(h):
        u = h @ w_in
        qk, v, o, gates, f = jnp.split(u, [2 * D_A, 3 * D_A, 4 * D_A, 4 * D_A + 4 * MLSTM_HEADS], axis=-1)
        qk = jax.nn.silu(depthwise_conv(qk, w_qk_conv))
        q, k = jnp.split(qk, 2, axis=-1)
        q = heads_first(q, MLSTM_HEADS).astype(f32) * (MLSTM_DH ** -0.5)
        k = heads_first(k, MLSTM_HEADS).astype(f32)
        v = heads_first(v, MLSTM_HEADS).astype(f32)
        gates = (gates + b_gate).astype(f32).transpose(0, 2, 1)
        i_f, f_f, i_b, f_b = jnp.split(gates, 4, axis=1)
        return q, k, v, (i_f, jax.nn.log_sigmoid(f_f)), (i_b, jax.nn.log_sigmoid(f_b)), o, f

    qc, kc, vc, gcf, gcb, oc, fc = project(h_ctx)
    ql, kl, vl, glf, glb, ol, fl = project(h_lat)
    b_ = h_lat.shape[0]
    zero = (jnp.zeros((b_, MLSTM_HEADS, MLSTM_DH, MLSTM_DH), f32),
            jnp.zeros((b_, MLSTM_HEADS, MLSTM_DH), f32),
            jnp.zeros((b_, MLSTM_HEADS), f32))
    flip = lambda a: jnp.flip(a, axis=2)
    hcf, st_f = mlstm_scan(qc, kc, vc, gcf[0], gcf[1], zero)
    hlf, _ = mlstm_scan(ql, kl, vl, glf[0], glf[1], st_f)
    hcb, st_b = mlstm_scan(flip(qc), flip(kc), flip(vc), flip(gcb[0]), flip(gcb[1]), zero)
    hlb, _ = mlstm_scan(flip(ql), flip(kl), flip(vl), flip(glb[0]), flip(glb[1]), st_b)

    def finish(hf, hb_rev, o, f):
        h = hf + flip(hb_rev)
        b2, _, t2, _ = h.shape
        h = h.transpose(0, 2, 1, 3).reshape(b2, t2, D_A).astype(o.dtype)
        h = head_rmsnorm(h, g_head, MLSTM_HEADS) * jax.nn.sigmoid(o)
        return jnp.concatenate([h, fourier_mix(f)], axis=-1) @ w_out

    y_lat = finish(hlf, hlb, ol, fl)
    y_ctx = finish(hcf, hcb, oc, fc) if need_ctx_out else None
    return y_lat, y_ctx


def diff_attention(q, k, v, lam):
    s = jnp.einsum('bqhmd,bkhmd->bhmqk', q, k).astype(jnp.float32) * (DIFF_DH ** -0.5)
    p = jax.nn.softmax(s, axis=-1)
    a = p[:, :, 0] - lam * p[:, :, 1]
    return jnp.einsum('bhqk,bkhd->bqhd', a.astype(v.dtype), v)


def odd_mixer(h_lat, h_ctx, w_in, lam_p, g_sub, w_dw, g_ln, b_ln, w_out, lam_init, need_ctx_out):
    def project(h):
        u = h @ w_in
        q, k, v, glu = jnp.split(u, [D_C, 2 * D_C, 3 * D_C], axis=-1)
        b_, t_, _ = h.shape
        q = q.reshape(b_, t_, DIFF_HEADS, 2, DIFF_DH)
        k = k.reshape(b_, t_, DIFF_HEADS, 2, DIFF_DH)
        v = v.reshape(b_, t_, DIFF_HEADS, DIFF_DV)
        return q, k, v, glu

    qc, kc, vc, gluc = project(h_ctx)
    ql, kl, vl, glul = project(h_lat)
    b_, t_, _ = h_lat.shape
    cos, sin = axial_rope(t_, DIFF_DH)
    ql = apply_rope(ql, cos, sin)
    kl = apply_rope(kl, cos, sin)
    lp = lam_p.astype(jnp.float32)
    lam = jnp.exp(jnp.sum(lp[0] * lp[1])) - jnp.exp(jnp.sum(lp[2] * lp[3])) + lam_init
    k_all = jnp.concatenate([kc, kl], axis=1)
    v_all = jnp.concatenate([vc, vl], axis=1)
    nb = t_ // Q_BLOCK
    qb = jnp.moveaxis(ql.reshape(b_, nb, Q_BLOCK, DIFF_HEADS, 2, DIFF_DH), 1, 0)
    ob = lax.map(lambda q_blk: diff_attention(q_blk, k_all, v_all, lam), qb)
    o_lat = jnp.moveaxis(ob, 0, 1).reshape(b_, t_, DIFF_HEADS, DIFF_DV)

    def finish(o, glu):
        b2, t2 = o.shape[0], o.shape[1]
        o = (rmsnorm(o, g_sub) * (1.0 - lam_init)).reshape(b2, t2, D_C)
        a, gte = jnp.split(glu, 2, axis=-1)
        z = depthwise_conv(a * jax.nn.sigmoid(gte), w_dw)
        z = jax.nn.silu(layernorm(z, g_ln, b_ln))
        return jnp.concatenate([o.astype(z.dtype), z], axis=-1) @ w_out

    y_lat = finish(o_lat, glul)
    y_ctx = finish(diff_attention(qc, kc, vc, lam), gluc) if need_ctx_out else None
    return y_lat, y_ctx


def setup_inputs(seed: int = 0) -> dict:
    key = jax.random.key(seed)
    ks = jax.random.split(key, 24)
    f32 = jnp.float32

    def nrm(k, shape, scale):
        return jax.random.normal(k, shape, f32) * scale

    fbias = jnp.linspace(3.0, 6.0, MLSTM_HEADS, dtype=f32)
    gate_base = jnp.array([0.0, 1.0, 0.0, 1.0], f32)[:, None] * fbias[None, :]
    b_gate = (nrm(ks[8], (N_EVEN, 4, MLSTM_HEADS), 0.1) + gate_base[None]).reshape(N_EVEN, 4 * MLSTM_HEADS)
    return {
        'x': nrm(ks[0], (BATCH, SEQ, D_MODEL), 1.0),
        'c': nrm(ks[1], (BATCH, D_MODEL), 1.0),
        'ctx': nrm(ks[2], (BATCH, CTX_LEN, D_MODEL), 1.0),
        'c_ctx': nrm(ks[3], (D_MODEL,), 1.0),
        'w_mod': nrm(ks[4], (DEPTH, D_MODEL, 6 * D_MODEL), 0.5 * D_MODEL ** -0.5),
        'b_mod': nrm(ks[5], (DEPTH, 6 * D_MODEL), 0.02),
        'g_norm': 1.0 + nrm(ks[6], (DEPTH, 2, D_MODEL), 0.02),
        'w_in_even': nrm(ks[7], (N_EVEN, D_MODEL, EVEN_IN), D_MODEL ** -0.5),
        'b_gate': b_gate,
        'w_qk_conv': nrm(ks[9], (N_EVEN, QK_CONV_W, 2 * D_A), QK_CONV_W ** -0.5),
        'g_mlstm_head': 1.0 + nrm(ks[10], (N_EVEN, D_A), 0.02),
        'w_out_even': nrm(ks[11], (N_EVEN, D_MODEL, D_MODEL), D_MODEL ** -0.5),
        'w_in_odd': nrm(ks[12], (N_ODD, D_MODEL, ODD_IN), D_MODEL ** -0.5),
        'lam_p': nrm(ks[13], (N_ODD, 4, DIFF_DH), 0.1),
        'g_subln': 1.0 + nrm(ks[14], (N_ODD, DIFF_DV), 0.02),
        'w_dw': nrm(ks[15], (N_ODD, CONV_W, D_D), CONV_W ** -0.5),
        'g_conv_ln': 1.0 + nrm(ks[16], (N_ODD, D_D), 0.02),
        'b_conv_ln': nrm(ks[17], (N_ODD, D_D), 0.02),
        'w_out_odd': nrm(ks[18], (N_ODD, D_MODEL, D_MODEL), D_MODEL ** -0.5),
        'w_ff1': nrm(ks[19], (DEPTH, D_MODEL, D_FF), D_MODEL ** -0.5),
        'w_ff2': nrm(ks[20], (DEPTH, D_FF, D_MODEL), D_FF ** -0.5),
        'g_final': 1.0 + nrm(ks[21], (D_MODEL,), 0.02),
    }


def reference(x, c, ctx, c_ctx, w_mod, b_mod, g_norm, w_in_even, b_gate, w_qk_conv, g_mlstm_head,
              w_out_even, w_in_odd, lam_p, g_subln, w_dw, g_conv_ln, b_conv_ln, w_out_odd,
              w_ff1, w_ff2, g_final):
    xc = ctx
    s_lat = jax.nn.silu(c)
    s_ctx = jax.nn.silu(c_ctx)
    for l in range(DEPTH):
        last = l == DEPTH - 1
        j = l // 2
        mod_l = jnp.split((s_lat @ w_mod[l] + b_mod[l])[:, None, :], 6, axis=-1)
        mod_c = jnp.split(s_ctx @ w_mod[l] + b_mod[l], 6, axis=-1)
        h_l = modulate(rmsnorm(x, g_norm[l, 0]), mod_l[0], mod_l[1])
        h_c = modulate(rmsnorm(xc, g_norm[l, 0]), mod_c[0], mod_c[1])
        if l % 2 == 0:
            y_l, y_c = even_mixer(h_l, h_c, w_in_even[j], b_gate[j], w_qk_conv[j], g_mlstm_head[j],
                                  w_out_even[j], not last)
        else:
            lam_init = 0.8 - 0.6 * math.exp(-0.3 * l)
            y_l, y_c = odd_mixer(h_l, h_c, w_in_odd[j], lam_p[j], g_subln[j], w_dw[j], g_conv_ln[j],
                                 b_conv_ln[j], w_out_odd[j], lam_init, not last)
        x = x + mod_l[2] * y_l
        x = x + mod_l[5] * squared_relu_mlp(modulate(rmsnorm(x, g_norm[l, 1]), mod_l[3], mod_l[4]), w_ff1[l], w_ff2[l])
        if not last:
            xc = xc + mod_c[2] * y_c
            xc = xc + mod_c[5] * squared_relu_mlp(modulate(rmsnorm(xc, g_norm[l, 1]), mod_c[3], mod_c[4]), w_ff1[l], w_ff2[l])
    return rmsnorm(x, g_final)
```

```python
import functools
import math

import numpy as np
import jax
import jax.numpy as jnp
from jax import lax
from jax.experimental import pallas as pl
from jax.experimental.pallas import tpu as pltpu

F32 = jnp.float32
BF16 = jnp.bfloat16

EPS = 1e-6
HEADS = 6
HEAD_W = 128
D_HEADS = HEADS * HEAD_W
D_SIDE = 256
FNET_GC = 64
QK_W = 2 * D_HEADS
CONV_W = 31
CONV_HALO = 16
GRID_W = 64
ROPE_BASE = 10000.0
DIFF_DH = 64
FFT_N1 = 128
ROW_TILE = 256
MLSTM_CHUNK = 256
ATTN_TQ = 256
ATTN_TK = 768
NEG = -1e30
VMEM_LIMIT = 56 * 1024 * 1024


def _cparams(*sem):
    return pltpu.CompilerParams(dimension_semantics=sem, vmem_limit_bytes=VMEM_LIMIT)


def _dot(a, b):
    return jnp.dot(a, b, preferred_element_type=F32)


def _dot_nt(a, b):
    return lax.dot_general(a, b, (((1,), (1,)), ((), ())), preferred_element_type=F32)


def _dot_tn(a, b):
    return lax.dot_general(a, b, (((0,), (0,)), ((), ())), preferred_element_type=F32)


def _sigmoid(x):
    return 1.0 / (1.0 + jnp.exp(-x))


def _rms(x, g):
    return x * lax.rsqrt(jnp.mean(x * x, axis=-1, keepdims=True) + EPS) * g


def _norm_mod(x, g, shift, scale):
    return _rms(x, g) * (1.0 + scale) + shift


def _mod_kernel(s_ref, w_ref, b_ref, o_ref):
    s = s_ref[...]
    s = s * _sigmoid(s)
    o_ref[0] = _dot(s.astype(BF16), w_ref[0].astype(BF16)) + b_ref[0]


def _mod_vectors(c, c_ctx, w_mod, b_mod):
    depth, d, n = w_mod.shape
    b = c.shape[0]
    s = jnp.zeros((8, d), F32).at[:b].set(c).at[b].set(c_ctx)
    tn = 1536
    out = pl.pallas_call(
        _mod_kernel, name="adaln_mod",
        grid=(depth, n // tn),
        in_specs=[pl.BlockSpec((8, d), lambda l, j: (0, 0)),
                  pl.BlockSpec((1, d, tn), lambda l, j: (l, 0, j)),
                  pl.BlockSpec((1, 1, tn), lambda l, j: (l, 0, j))],
        out_specs=pl.BlockSpec((1, 8, tn), lambda l, j: (l, 0, j)),
        out_shape=jax.ShapeDtypeStruct((depth, 8, n), F32),
        compiler_params=_cparams("parallel", "parallel"),
    )(s, w_mod, b_mod.reshape(depth, 1, n))
    return out.reshape(depth, 8, 6, d)


def _mod_spec(b, nlat):
    return pl.BlockSpec((1, 6, 1024), lambda bi, i: (jnp.where(i >= nlat, b, bi), 0, 0))


def _const_spec(shape):
    nd = len(shape)
    return pl.BlockSpec(shape, lambda *_: (0,) * nd)


def _proj_even_kernel(x_ref, mod_ref, g_ref, w_ref, bg_ref, dft_ref,
                      qk_ref, v_ref, o_ref, gt_ref, p_ref):
    m = mod_ref[0]
    h = _norm_mod(x_ref[0], g_ref[...], m[0:1], m[1:2]).astype(BF16)
    u = _dot(h, w_ref[...])
    qk_ref[0] = u[:, :QK_W]
    v_ref[0] = u[:, QK_W:QK_W + D_HEADS].astype(BF16)
    o_ref[0] = u[:, QK_W + D_HEADS:QK_W + 2 * D_HEADS]
    c0 = QK_W + 2 * D_HEADS
    gt_ref[0] = u[:, c0:c0 + 256] + bg_ref[...]
    f = u[:, c0 + 256:c0 + 512].astype(BF16)
    p_ref[0] = _dot(f, dft_ref[...].astype(BF16)).astype(BF16)


def _proj_even(xcat, mod, g, w_all, bg, dftc, nlat):
    b, ttot, d = xcat.shape
    tm = ROW_TILE
    nt = ttot // tm
    row = lambda w: pl.BlockSpec((1, tm, w), lambda bi, i: (bi, i, 0))
    return pl.pallas_call(
        _proj_even_kernel, name="proj_even",
        grid=(b, nt),
        in_specs=[row(d), _mod_spec(b, nlat), _const_spec((1, d)), _const_spec(w_all.shape),
                  _const_spec((1, 256)), _const_spec(dftc.shape)],
        out_specs=[row(QK_W), row(D_HEADS), row(D_HEADS), row(256), row(512)],
        out_shape=[jax.ShapeDtypeStruct((b, ttot, QK_W), F32),
                   jax.ShapeDtypeStruct((b, ttot, D_HEADS), BF16),
                   jax.ShapeDtypeStruct((b, ttot, D_HEADS), F32),
                   jax.ShapeDtypeStruct((b, ttot, 256), F32),
                   jax.ShapeDtypeStruct((b, ttot, 512), BF16)],
        compiler_params=_cparams("parallel", "parallel"),
    )(xcat, mod, g, w_all, bg, dftc)


def _qkconv_kernel(x_ref, prev_ref, next_ref, w_ref, q_ref, k_ref, *, tm, nlat, ntot):
    i = pl.program_id(1)
    x = x_ref[0]
    seg_first = jnp.logical_or(i == 0, i == nlat)
    seg_last = jnp.logical_or(i == nlat - 1, i == ntot - 1)
    prow = jnp.where(seg_first, 0.0, prev_ref[0, 7:8, :])
    nrow = jnp.where(seg_last, 0.0, next_ref[0, 0:1, :])
    rid = lax.broadcasted_iota(jnp.int32, x.shape, 0)
    xm = jnp.where(rid == 0, prow, pltpu.roll(x, 1, axis=0))
    xp = jnp.where(rid == tm - 1, nrow, pltpu.roll(x, tm - 1, axis=0))
    w = w_ref[...]
    y = xm * w[0:1] + x * w[1:2] + xp * w[2:3]
    y = y * _sigmoid(y)
    q_ref[0] = (y[:, :D_HEADS] * (HEAD_W ** -0.5)).astype(BF16)
    k_ref[0] = y[:, D_HEADS:].astype(BF16)


def _qkconv(qk_raw, w, nlat):
    b, ttot, _ = qk_raw.shape
    tm = ROW_TILE
    nt = ttot // tm
    r8 = tm // 8
    last8 = ttot // 8 - 1
    return pl.pallas_call(
        functools.partial(_qkconv_kernel, tm=tm, nlat=nlat, ntot=nt), name="qk_conv",
        grid=(b, nt),
        in_specs=[pl.BlockSpec((1, tm, QK_W), lambda bi, i: (bi, i, 0)),
                  pl.BlockSpec((1, 8, QK_W), lambda bi, i: (bi, jnp.maximum(i * r8 - 1, 0), 0)),
                  pl.BlockSpec((1, 8, QK_W), lambda bi, i: (bi, jnp.minimum((i + 1) * r8, last8), 0)),
                  _const_spec((3, QK_W))],
        out_specs=[pl.BlockSpec((1, tm, D_HEADS), lambda bi, i: (bi, i, 0))] * 2,
        out_shape=[jax.ShapeDtypeStruct((b, ttot, D_HEADS), BF16)] * 2,
        compiler_params=_cparams("parallel", "parallel"),
    )(qk_raw, qk_raw, qk_raw, w)


def _mlstm_kernel(q_ref, k_ref, v_ref, g_ref, h_ref, c_scr, m_scr, *, chunk):
    d = pl.program_id(1)
    c = pl.program_id(2)

    @pl.when(c == 0)
    def _():
        c_scr[...] = jnp.zeros_like(c_scr)
        m_scr[...] = jnp.zeros_like(m_scr)

    gates = g_ref[0]
    logf = jnp.minimum(gates, 0.0) - jnp.log(1.0 + jnp.exp(-jnp.abs(gates)))
    row = lax.broadcasted_iota(jnp.int32, (chunk, chunk), 0)
    col = lax.broadcasted_iota(jnp.int32, (chunk, chunk), 1)
    mask = (col - row) * (1 - 2 * d) <= 0
    a_all = jnp.dot(mask.astype(F32), logf, precision=lax.Precision.HIGHEST,
                    preferred_element_type=F32)
    r_all = gates - pltpu.roll(a_all, 128 - HEADS, axis=1)
    r_rows = r_all.T
    one_col = (lax.broadcasted_iota(jnp.int32, (chunk, HEAD_W), 1) == 0)
    one_bf = one_col.astype(BF16)
    one_f = one_col.astype(F32)

    for h in range(HEADS):
        sl = slice(h * HEAD_W, (h + 1) * HEAD_W)
        qh = q_ref[0, :, sl]
        kh = k_ref[0, :, sl]
        vh = v_ref[0, :, sl]
        m_prev = m_scr[h:h + 1, 0:1]
        a_col = a_all[:, HEADS + h:HEADS + h + 1]
        r_col = r_all[:, h:h + 1]
        rm = jnp.where(mask, r_rows[h:h + 1, :], NEG)
        mx = jnp.maximum(jnp.max(rm, axis=1, keepdims=True), m_prev)
        s = _dot_nt(qh, kh) * jnp.exp(rm - mx)
        vext = jnp.concatenate([vh, one_bf], axis=1)
        ct = c_scr[h]
        nd = _dot(s.astype(BF16), vext) + jnp.exp(m_prev - mx) * _dot(qh, ct.astype(BF16))
        den = jnp.maximum(jnp.abs(nd[:, HEAD_W:HEAD_W + 1]), jnp.exp(-(a_col + mx)))
        h_ref[0, 0, :, sl] = nd[:, :HEAD_W] / den
        mx_last = jnp.max(mx, axis=0, keepdims=True)
        a_last = jnp.min(a_col, axis=0, keepdims=True)
        w_col = jnp.exp(r_col - mx_last)
        wv = jnp.concatenate([w_col * vh.astype(F32), w_col * one_f], axis=1).astype(BF16)
        c_scr[h] = jnp.exp(m_prev - mx_last) * ct + _dot_tn(kh, wv)
        m_scr[h:h + 1, :] = jnp.broadcast_to(a_last + mx_last, (1, 128))


def _mlstm(q, k, v, gates, nlat_rows):
    b, ttot, _ = q.shape
    L = MLSTM_CHUNK
    nc = ttot // L
    ncl = nlat_rows // L
    ncc = nc - ncl

    def blk(d, c):
        fwd = jnp.where(c < ncc, ncl + c, c - ncc)
        bwd = jnp.where(c < ncc, nc - 1 - c, ncl - 1 - (c - ncc))
        return jnp.where(d == 0, fwd, bwd)

    head_spec = pl.BlockSpec((1, L, D_HEADS), lambda bi, d, c: (bi, blk(d, c), 0))
    return pl.pallas_call(
        functools.partial(_mlstm_kernel, chunk=L), name="mlstm_scan",
        grid=(b, 2, nc),
        in_specs=[head_spec, head_spec, head_spec,
                  pl.BlockSpec((1, L, 128), lambda bi, d, c: (bi, blk(d, c), d))],
        out_specs=pl.BlockSpec((1, 1, L, D_HEADS), lambda bi, d, c: (d, bi, blk(d, c), 0)),
        out_shape=jax.ShapeDtypeStruct((2, b, ttot, D_HEADS), F32),
        scratch_shapes=[pltpu.VMEM((HEADS, HEAD_W, 2 * HEAD_W), F32),
                        pltpu.VMEM((8, 128), F32)],
        compiler_params=_cparams("parallel", "parallel", "arbitrary"),
    )(q, k, v, gates)


def _fft1_kernel(p_ref, g_ref, o_ref, *, nb):
    for j in range(nb):
        pr = p_ref[0, :, j * 512:j * 512 + 256]
        pi = p_ref[0, :, j * 512 + 256:(j + 1) * 512]
        bb = _dot(g_ref[j].astype(BF16), jnp.concatenate([pr, pi], axis=0))
        o_ref[0, 0, :, j * 256:(j + 1) * 256] = bb[:FFT_N1].astype(BF16)
        o_ref[0, 1, :, j * 256:(j + 1) * 256] = bb[FFT_N1:].astype(BF16)


def _fft2_kernel(b_ref, t_ref, o_ref, *, kb, scale):
    tc = t_ref[0].astype(BF16)
    ts = t_ref[1].astype(BF16)
    for j in range(kb):
        y = _dot(tc, b_ref[0, 0, j]) + _dot(ts, b_ref[0, 1, j])
        o_ref[0, j] = y * scale


def _dft_ctx_kernel(p_ref, t_ref, o_ref, *, scale):
    p = p_ref[0]
    st = jnp.concatenate([p[:, :256], p[:, 256:]], axis=0)
    o_ref[0] = _dot(t_ref[...].astype(BF16), st) * scale


def _fft_tables(t):
    n1, n2 = FFT_N1, t // FFT_N1
    k1 = np.arange(n1, dtype=np.int64)[None, :, None]
    nn = (n2 * np.arange(n1, dtype=np.int64)[None, None, :] + np.arange(n2, dtype=np.int64)[:, None, None])
    ang = 2.0 * np.pi * ((k1 * nn) % t).astype(np.float64) / t
    gr, gi = np.cos(ang), -np.sin(ang)
    g = np.concatenate([np.concatenate([gr, -gi], axis=2), np.concatenate([gi, gr], axis=2)], axis=1)
    a2 = 2.0 * np.pi * ((np.arange(n2)[:, None] * np.arange(n2)[None, :]) % n2) / n2
    t2 = np.stack([np.cos(a2), np.sin(a2)])
    return jnp.asarray(g, F32), jnp.asarray(t2, F32)


def _dft_matrix_cs(n):
    a = 2.0 * np.pi * ((np.arange(n)[:, None] * np.arange(n)[None, :]) % n) / n
    return np.cos(a), np.sin(a)


def _channel_dft():
    c, s = _dft_matrix_cs(FNET_GC)
    eye = np.eye(D_SIDE // FNET_GC)
    return jnp.asarray(np.concatenate([np.kron(eye, c), -np.kron(eye, s)], axis=1), F32)


def _fourier(p, t, ctx_len):
    b = p.shape[0]
    n1, n2 = FFT_N1, t // FFT_N1
    g, t2 = _fft_tables(t)
    nb = min(8, n2)
    kb = 16
    p_lat = p[:, :t].reshape(b, n1, n2 * 512)
    st1 = pl.pallas_call(
        functools.partial(_fft1_kernel, nb=nb), name="fft_stage1",
        grid=(b, n2 // nb),
        in_specs=[pl.BlockSpec((1, n1, nb * 512), lambda bi, j: (bi, 0, j)),
                  pl.BlockSpec((nb, 256, 256), lambda bi, j: (j, 0, 0))],
        out_specs=pl.BlockSpec((1, 2, n1, nb * 256), lambda bi, j: (bi, 0, 0, j)),
        out_shape=jax.ShapeDtypeStruct((b, 2, n1, n2 * 256), BF16),
        compiler_params=_cparams("parallel", "parallel"),
    )(p_lat, g)
    st1 = st1.reshape(b, 2, n1, n2, 256)
    y = pl.pallas_call(
        functools.partial(_fft2_kernel, kb=kb, scale=1.0 / math.sqrt(t * FNET_GC)), name="fft_stage2",
        grid=(b, n1 // kb),
        in_specs=[pl.BlockSpec((1, 2, kb, n2, 256), lambda bi, j: (bi, 0, j, 0, 0)),
                  _const_spec((2, n2, n2))],
        out_specs=pl.BlockSpec((1, kb, n2, 256), lambda bi, j: (bi, j, 0, 0)),
        out_shape=jax.ShapeDtypeStruct((b, n1, n2, 256), F32),
        compiler_params=_cparams("parallel", "parallel"),
    )(st1, t2)
    y_lat = y.transpose(0, 2, 1, 3).reshape(b, t, 256)
    cc, sc = _dft_matrix_cs(ctx_len)
    tc = jnp.asarray(np.concatenate([cc, sc], axis=1), F32)
    y_ctx = pl.pallas_call(
        functools.partial(_dft_ctx_kernel, scale=1.0 / math.sqrt(ctx_len * FNET_GC)), name="dft_ctx",
        grid=(b,),
        in_specs=[pl.BlockSpec((1, ctx_len, 512), lambda bi: (bi, t // ctx_len, 0)),
                  _const_spec((ctx_len, 2 * ctx_len))],
        out_specs=pl.BlockSpec((1, ctx_len, 256), lambda bi: (bi, 0, 0)),
        out_shape=jax.ShapeDtypeStruct((b, ctx_len, 256), F32),
        compiler_params=_cparams("parallel"),
    )(p, tc)
    return y_lat, y_ctx


def _even_finish_kernel(hf_ref, hb_ref, o_ref, yl_ref, yc_ref, x_ref, mod_ref, gh_ref, w_ref,
                        out_ref, *, nlat):
    i = pl.program_id(1)
    hs = hf_ref[0, 0] + hb_ref[0, 0]
    gate = _sigmoid(o_ref[0])
    gh = gh_ref[...]
    parts = []
    for h in range(HEADS):
        sl = slice(h * HEAD_W, (h + 1) * HEAD_W)
        parts.append((_rms(hs[:, sl], gh[:, sl]) * gate[:, sl]).astype(BF16))
    y = jnp.where(i >= nlat, yc_ref[0], yl_ref[0])
    parts.append(y.astype(BF16))
    out = _dot(jnp.concatenate(parts, axis=1), w_ref[...])
    out_ref[0] = x_ref[0] + mod_ref[0][2:3] * out


def _even_finish(h2, o, y_lat, y_ctx, xcat, mod, g_head, w_out, nlat):
    b, ttot, d = xcat.shape
    tm = ROW_TILE
    nt = ttot // tm
    row = lambda w: pl.BlockSpec((1, tm, w), lambda bi, i: (bi, i, 0))
    return pl.pallas_call(
        functools.partial(_even_finish_kernel, nlat=nlat), name="even_finish",
        grid=(b, nt),
        in_specs=[pl.BlockSpec((1, 1, tm, D_HEADS), lambda bi, i: (0, bi, i, 0)),
                  pl.BlockSpec((1, 1, tm, D_HEADS), lambda bi, i: (1, bi, i, 0)),
                  row(D_HEADS),
                  pl.BlockSpec((1, tm, 256), lambda bi, i: (bi, jnp.minimum(i, nlat - 1), 0)),
                  pl.BlockSpec((1, tm, 256), lambda bi, i: (bi, jnp.maximum(i - nlat, 0), 0)),
                  row(d), _mod_spec(b, nlat), _const_spec((1, D_HEADS)), _const_spec((d, d))],
        out_specs=row(d),
        out_shape=jax.ShapeDtypeStruct((b, ttot, d), F32),
        compiler_params=_cparams("parallel", "parallel"),
    )(h2, h2, o, y_lat, y_ctx, xcat, mod, g_head, w_out)


def _mlp_kernel(x_ref, mod_ref, g_ref, w1_ref, w2_ref, gf_ref, out_ref, *, final):
    x = x_ref[0]
    m = mod_ref[0]
    h = _norm_mod(x, g_ref[...], m[3:4], m[4:5]).astype(BF16)
    a = jnp.maximum(_dot(h, w1_ref[...]), 0.0)
    y = _dot((a * a).astype(BF16), w2_ref[...])
    xn = x + m[5:6] * y
    if final:
        xn = _rms(xn, gf_ref[...])
    out_ref[0] = xn


def _mlp(x, mod, g, w1, w2, g_final, nlat, rows_out, final):
    b, _, d = x.shape
    tm = ROW_TILE
    nt = rows_out // tm
    row = pl.BlockSpec((1, tm, d), lambda bi, i: (bi, i, 0))
    single = dict(pipeline_mode=pl.Buffered(1))
    return pl.pallas_call(
        functools.partial(_mlp_kernel, final=final), name="mlp_final" if final else "mlp",
        grid=(b, nt),
        in_specs=[row, _mod_spec(b, nlat), _const_spec((1, d)),
                  pl.BlockSpec(w1.shape, lambda bi, i: (0, 0), **single),
                  pl.BlockSpec(w2.shape, lambda bi, i: (0, 0), **single),
                  _const_spec((1, d))],
        out_specs=row,
        out_shape=jax.ShapeDtypeStruct((b, rows_out, d), F32),
        compiler_params=_cparams("parallel", "parallel"),
    )(x, mod, g, w1, w2, g_final)


def _swap_halves(x):
    lane = lax.broadcasted_iota(jnp.int32, x.shape, 1)
    return jnp.where(lane % 64 < 32, pltpu.roll(x, 96, axis=1), pltpu.roll(x, 32, axis=1))


def _proj_odd_kernel(x_ref, mod_ref, g_ref, w_ref, cos_ref, sin_ref, q_ref, k_ref, v_ref, glu_ref):
    m = mod_ref[0]
    h = _norm_mod(x_ref[0], g_ref[...], m[0:1], m[1:2]).astype(BF16)
    u = _dot(h, w_ref[...])
    cos = cos_ref[...]
    sin = sin_ref[...]
    for hh in range(HEADS):
        sl = slice(hh * HEAD_W, (hh + 1) * HEAD_W)
        qh = u[:, sl]
        kh = u[:, D_HEADS + hh * HEAD_W:D_HEADS + (hh + 1) * HEAD_W]
        q_ref[0, :, sl] = ((qh * cos + _swap_halves(qh) * sin) * (DIFF_DH ** -0.5)).astype(BF16)
        k_ref[0, :, sl] = (kh * cos + _swap_halves(kh) * sin).astype(BF16)
    v_ref[0] = u[:, 2 * D_HEADS:3 * D_HEADS].astype(BF16)
    a = u[:, 3 * D_HEADS:3 * D_HEADS + D_SIDE]
    gte = u[:, 3 * D_HEADS + D_SIDE:]
    glu_ref[0] = a * _sigmoid(gte)


def _rope_tables(t, ctx_len):
    rows = t // GRID_W
    row = jnp.repeat(jnp.arange(rows, dtype=F32), GRID_W)
    col = jnp.tile(jnp.arange(GRID_W, dtype=F32), rows)
    n_freq = DIFF_DH // 4
    inv = ROPE_BASE ** (-jnp.arange(n_freq, dtype=F32) / n_freq)
    ang = jnp.concatenate([row[:, None] * inv, col[:, None] * inv], axis=-1)
    cos, sin = jnp.cos(ang), jnp.sin(ang)
    cos = jnp.concatenate([cos, cos, cos, cos], axis=-1)
    sin = jnp.concatenate([-sin, sin, -sin, sin], axis=-1)
    cos = jnp.concatenate([cos, jnp.ones((ctx_len, 128), F32)], axis=0)
    sin = jnp.concatenate([sin, jnp.zeros((ctx_len, 128), F32)], axis=0)
    return cos, sin


def _proj_odd(x, mod, g, w_all, cos, sin, nlat):
    b, ttot, d = x.shape
    tm = ROW_TILE
    nt = ttot // tm
    row = lambda w: pl.BlockSpec((1, tm, w), lambda bi, i: (bi, i, 0))
    tab = pl.BlockSpec((tm, 128), lambda bi, i: (i, 0))
    return pl.pallas_call(
        _proj_odd_kernel, name="proj_odd",
        grid=(b, nt),
        in_specs=[row(d), _mod_spec(b, nlat), _const_spec((1, d)), _const_spec(w_all.shape), tab, tab],
        out_specs=[row(D_HEADS), row(D_HEADS), row(D_HEADS), row(D_SIDE)],
        out_shape=[jax.ShapeDtypeStruct((b, ttot, D_HEADS), BF16)] * 3
        + [jax.ShapeDtypeStruct((b, ttot, D_SIDE), F32)],
        compiler_params=_cparams("parallel", "parallel"),
    )(x, mod, g, w_all, cos, sin)


def _attn_kernel(q_ref, k_ref, v_ref, lam_ref, gs_ref, o_ref, acc0, acc1, *, tk, nk, lam_init):
    q = q_ref[0]
    q0 = q[:, :DIFF_DH]
    q1 = q[:, DIFF_DH:]
    tq = q.shape[0]
    one_bf = (lax.broadcasted_iota(jnp.int32, (tk, HEAD_W), 1) == 0).astype(BF16)
    acc0[...] = jnp.zeros_like(acc0)
    acc1[...] = jnp.zeros_like(acc1)

    def online(s, m_old, acc, vext):
        m_new = jnp.maximum(m_old, jnp.max(s, axis=1, keepdims=True))
        p = jnp.exp(s - m_new).astype(BF16)
        acc[...] = jnp.exp(m_old - m_new) * acc[...] + _dot(p, vext)
        return m_new

    def body(j, carry):
        m0, m1 = carry
        start = pl.multiple_of(j * tk, tk)
        kk = k_ref[0, pl.ds(start, tk), :]
        vext = jnp.concatenate([v_ref[0, pl.ds(start, tk), :], one_bf], axis=1)
        m0 = online(_dot_nt(q0, kk[:, :DIFF_DH]), m0, acc0, vext)
        m1 = online(_dot_nt(q1, kk[:, DIFF_DH:]), m1, acc1, vext)
        return m0, m1

    init = jnp.full((tq, 1), NEG, F32)
    lax.fori_loop(0, nk, body, (init, init))
    lp = lam_ref[...]
    lam = (jnp.exp(jnp.sum(lp[0:1] * lp[1:2], axis=1, keepdims=True))
           - jnp.exp(jnp.sum(lp[2:3] * lp[3:4], axis=1, keepdims=True)) + lam_init)
    a0 = acc0[...]
    a1 = acc1[...]
    o = a0[:, :HEAD_W] / a0[:, HEAD_W:HEAD_W + 1] - lam * (a1[:, :HEAD_W] / a1[:, HEAD_W:HEAD_W + 1])
    o_ref[0] = _rms(o, gs_ref[...]) * (1.0 - lam_init)


def _diff_attention(q, k, v, lam_p, g_sub, t, lam_init):
    b, ttot, _ = q.shape
    tq, tk = ATTN_TQ, ATTN_TK
    return pl.pallas_call(
        functools.partial(_attn_kernel, tk=tk, nk=ttot // tk, lam_init=lam_init), name="diff_attn",
        grid=(b, HEADS, t // tq),
        in_specs=[pl.BlockSpec((1, tq, HEAD_W), lambda bi, h, i: (bi, i, h)),
                  pl.BlockSpec((1, ttot, HEAD_W), lambda bi, h, i: (bi, 0, h)),
                  pl.BlockSpec((1, ttot, HEAD_W), lambda bi, h, i: (bi, 0, h)),
                  _const_spec(lam_p.shape), _const_spec((1, HEAD_W))],
        out_specs=pl.BlockSpec((1, tq, HEAD_W), lambda bi, h, i: (bi, i, h)),
        out_shape=jax.ShapeDtypeStruct((b, t, D_HEADS), F32),
        scratch_shapes=[pltpu.VMEM((tq, 2 * HEAD_W), F32)] * 2,
        compiler_params=_cparams("parallel", "parallel", "parallel"),
    )(q, k, v, lam_p, g_sub)


def _odd_finish_kernel(o_ref, glu_ref, prev_ref, next_ref, wdw_ref, gln_ref, bln_ref, x_ref, mod_ref,
                       w_ref, out_ref, xs, *, tm, nlat):
    i = pl.program_id(1)
    hal = CONV_HALO
    xs[0:hal, :] = jnp.where(i == 0, 0.0, prev_ref[0])
    xs[hal:hal + tm, :] = glu_ref[0]
    xs[hal + tm:, :] = jnp.where(i == nlat - 1, 0.0, next_ref[0])
    wdw = wdw_ref[...]
    z = jnp.zeros((tm, D_SIDE), F32)
    off = hal - (CONV_W - 1) // 2
    for kk in range(CONV_W):
        z = z + xs[off + kk:off + kk + tm, :] * wdw[kk:kk + 1]
    mu = jnp.mean(z, axis=-1, keepdims=True)
    zc = z - mu
    var = jnp.mean(zc * zc, axis=-1, keepdims=True)
    z = zc * lax.rsqrt(var + EPS) * gln_ref[...] + bln_ref[...]
    z = z * _sigmoid(z)
    cat = jnp.concatenate([o_ref[0].astype(BF16), z.astype(BF16)], axis=1)
    out_ref[0] = x_ref[0] + mod_ref[0][2:3] * _dot(cat, w_ref[...])


def _odd_finish(o, glu, w_dw, g_ln, b_ln, x, mod, w_out, t):
    b, _, d = x.shape
    tm = ROW_TILE
    nlat = t // tm
    r = tm // CONV_HALO
    row = lambda w: pl.BlockSpec((1, tm, w), lambda bi, i: (bi, i, 0))
    return pl.pallas_call(
        functools.partial(_odd_finish_kernel, tm=tm, nlat=nlat), name="odd_finish",
        grid=(b, nlat),
        in_specs=[row(D_HEADS), row(D_SIDE),
                  pl.BlockSpec((1, CONV_HALO, D_SIDE), lambda bi, i: (bi, jnp.maximum(i * r - 1, 0), 0)),
                  pl.BlockSpec((1, CONV_HALO, D_SIDE), lambda bi, i: (bi, (i + 1) * r, 0)),
                  _const_spec((CONV_W, D_SIDE)), _const_spec((1, D_SIDE)), _const_spec((1, D_SIDE)),
                  row(d), _mod_spec(b, nlat), _const_spec((d, d))],
        out_specs=row(d),
        out_shape=jax.ShapeDtypeStruct((b, t, d), F32),
        scratch_shapes=[pltpu.VMEM((tm + 2 * CONV_HALO, D_SIDE), F32)],
        compiler_params=_cparams("parallel", "parallel"),
    )(o, glu, glu, glu, w_dw, g_ln, b_ln, x, mod, w_out)


def kernel(x, c, ctx, c_ctx, w_mod, b_mod, g_norm, w_in_even, b_gate, w_qk_conv, g_mlstm_head,
           w_out_even, w_in_odd, lam_p, g_subln, w_dw, g_conv_ln, b_conv_ln, w_out_odd,
           w_ff1, w_ff2, g_final):
    b, t, d = x.shape
    ctx_len = ctx.shape[1]
    assert w_mod.shape[0] == 2 and d == 1024 and ctx_len == ROW_TILE and t % (FFT_N1 * 8) == 0
    nlat = t // ROW_TILE
    mod = _mod_vectors(c, c_ctx, w_mod, b_mod)
    xcat = jnp.concatenate([x, ctx], axis=1)

    we = w_in_even[0]
    g0 = 4 * D_HEADS
    w_gate = jnp.zeros((d, 256), F32).at[:, 0:12].set(we[:, g0:g0 + 12]).at[:, 128:140].set(we[:, g0 + 12:g0 + 24])
    w_all = jnp.concatenate([we[:, :g0], w_gate, we[:, g0 + 24:]], axis=1).astype(BF16)
    bg = jnp.zeros((1, 256), F32).at[0, 0:12].set(b_gate[0, :12]).at[0, 128:140].set(b_gate[0, 12:])
    qk_raw, v, o, gates, p = _proj_even(xcat, mod[0], g_norm[0, 0][None], w_all, bg, _channel_dft(), nlat)
    q, k = _qkconv(qk_raw, w_qk_conv[0], nlat)
    h2 = _mlstm(q, k, v, gates, t)
    y_lat, y_ctx = _fourier(p, t, ctx_len)
    x1 = _even_finish(h2, o, y_lat, y_ctx, xcat, mod[0], g_mlstm_head[0][None],
                      w_out_even[0].astype(BF16), nlat)
    x2 = _mlp(x1, mod[0], g_norm[0, 1][None], w_ff1[0].astype(BF16), w_ff2[0].astype(BF16),
              g_final[None], nlat, t + ctx_len, False)

    lam_init = 0.8 - 0.6 * math.exp(-0.3 * 1)
    cos, sin = _rope_tables(t, ctx_len)
    qa, ka, va, glu = _proj_odd(x2, mod[1], g_norm[1, 0][None], w_in_odd[0].astype(BF16), cos, sin, nlat)
    oa = _diff_attention(qa, ka, va, lam_p[0], g_subln[0][None], t, lam_init)
    x3 = _odd_finish(oa, glu, w_dw[0], g_conv_ln[0][None], b_conv_ln[0][None], x2, mod[1],
                     w_out_odd[0].astype(BF16), t)
    return _mlp(x3, mod[1], g_norm[1, 1][None], w_ff1[1].astype(BF16), w_ff2[1].astype(BF16),
                g_final[None], nlat, t, True)
```

```python
import functools
import math

import numpy as np
import jax
import jax.numpy as jnp
from jax import lax
from jax.experimental import pallas as pl
from jax.experimental.pallas import tpu as pltpu

F32 = jnp.float32
BF16 = jnp.bfloat16

EPS = 1e-6
HEADS = 6
HEAD_W = 128
D_HEADS = HEADS * HEAD_W
D_SIDE = 256
FNET_GC = 64
QK_W = 2 * D_HEADS
CONV_W = 31
CONV_HALO = 16
GRID_W = 64
ROPE_BASE = 10000.0
DIFF_DH = 64
FFT_N1 = 128
ROW_TILE = 256
MLSTM_CHUNK = 256
ATTN_TQ = 512
ATTN_TK = 1408
NEG = -1e30
Q_SCALE_LOG2 = (DIFF_DH ** -0.5) * math.log2(math.e)
VMEM_LIMIT = 56 * 1024 * 1024


def _cparams(*sem):
    return pltpu.CompilerParams(dimension_semantics=sem, vmem_limit_bytes=VMEM_LIMIT)


def _dot(a, b):
    return jnp.dot(a, b, preferred_element_type=F32)


def _dot_nt(a, b):
    return lax.dot_general(a, b, (((1,), (1,)), ((), ())), preferred_element_type=F32)


def _dot_tn(a, b):
    return lax.dot_general(a, b, (((0,), (0,)), ((), ())), preferred_element_type=F32)


def _sigmoid(x):
    return 1.0 / (1.0 + jnp.exp(-x))


def _rms(x, g):
    return x * lax.rsqrt(jnp.mean(x * x, axis=-1, keepdims=True) + EPS) * g


def _norm_mod(x, g, shift, scale):
    return _rms(x, g) * (1.0 + scale) + shift


def _mod_kernel(s_ref, w_ref, b_ref, o_ref):
    s = s_ref[...]
    s = s * _sigmoid(s)
    o_ref[0] = _dot(s.astype(BF16), w_ref[0].astype(BF16)) + b_ref[0]


def _mod_vectors(c, c_ctx, w_mod, b_mod):
    depth, d, n = w_mod.shape
    b = c.shape[0]
    s = jnp.zeros((8, d), F32).at[:b].set(c).at[b].set(c_ctx)
    tn = 1536
    out = pl.pallas_call(
        _mod_kernel, name="adaln_mod",
        grid=(depth, n // tn),
        in_specs=[pl.BlockSpec((8, d), lambda l, j: (0, 0)),
                  pl.BlockSpec((1, d, tn), lambda l, j: (l, 0, j)),
                  pl.BlockSpec((1, 1, tn), lambda l, j: (l, 0, j))],
        out_specs=pl.BlockSpec((1, 8, tn), lambda l, j: (l, 0, j)),
        out_shape=jax.ShapeDtypeStruct((depth, 8, n), F32),
        compiler_params=_cparams("parallel", "parallel"),
    )(s, w_mod, b_mod.reshape(depth, 1, n))
    return out.reshape(depth, 8, 6, d)


def _mod_spec(b, nlat):
    return pl.BlockSpec((1, 6, 1024), lambda bi, i: (jnp.where(i >= nlat, b, bi), 0, 0))


def _const_spec(shape):
    nd = len(shape)
    return pl.BlockSpec(shape, lambda *_: (0,) * nd)


def _tile_rows(x_ref, ctx_ref, nlat):
    return jnp.where(pl.program_id(1) >= nlat, ctx_ref[0], x_ref[0])


def _lat_ctx_specs(d, tm, nlat):
    return [pl.BlockSpec((1, tm, d), lambda bi, i: (bi, jnp.minimum(i, nlat - 1), 0)),
            pl.BlockSpec((1, tm, d), lambda bi, i: (bi, jnp.maximum(i - nlat, 0), 0))]


def _proj_even_kernel(x_ref, ctx_ref, mod_ref, g_ref, w_ref, bg_ref, dft_ref,
                      qk_ref, v_ref, o_ref, gt_ref, p_ref, *, nlat):
    m = mod_ref[0]
    h = _norm_mod(_tile_rows(x_ref, ctx_ref, nlat), g_ref[...], m[0:1], m[1:2]).astype(BF16)
    u = _dot(h, w_ref[...])
    qk_ref[0] = u[:, :QK_W]
    v_ref[0] = u[:, QK_W:QK_W + D_HEADS].astype(BF16)
    o_ref[0] = u[:, QK_W + D_HEADS:QK_W + 2 * D_HEADS]
    c0 = QK_W + 2 * D_HEADS
    gt_ref[0] = u[:, c0:c0 + 256] + bg_ref[...]
    f = u[:, c0 + 256:c0 + 512].astype(BF16)
    p_ref[0] = _dot(f, dft_ref[...].astype(BF16)).astype(BF16)


def _proj_even(x, ctx, mod, g, w_all, bg, dftc, nlat):
    b, t, d = x.shape
    ttot = t + ctx.shape[1]
    tm = ROW_TILE
    nt = ttot // tm
    row = lambda w: pl.BlockSpec((1, tm, w), lambda bi, i: (bi, i, 0))
    return pl.pallas_call(
        functools.partial(_proj_even_kernel, nlat=nlat), name="proj_even",
        grid=(b, nt),
        in_specs=_lat_ctx_specs(d, tm, nlat) + [_mod_spec(b, nlat), _const_spec((1, d)),
                                                _const_spec(w_all.shape), _const_spec((1, 256)),
                                                _const_spec(dftc.shape)],
        out_specs=[row(QK_W), row(D_HEADS), row(D_HEADS), row(256), row(512)],
        out_shape=[jax.ShapeDtypeStruct((b, ttot, QK_W), F32),
                   jax.ShapeDtypeStruct((b, ttot, D_HEADS), BF16),
                   jax.ShapeDtypeStruct((b, ttot, D_HEADS), F32),
                   jax.ShapeDtypeStruct((b, ttot, 256), F32),
                   jax.ShapeDtypeStruct((b, ttot, 512), BF16)],
        compiler_params=_cparams("parallel", "parallel"),
    )(x, ctx, mod, g, w_all, bg, dftc)


def _qkconv_kernel(x_ref, prev_ref, next_ref, w_ref, q_ref, k_ref, *, tm, nlat, ntot):
    i = pl.program_id(1)
    x = x_ref[0]
    seg_first = jnp.logical_or(i == 0, i == nlat)
    seg_last = jnp.logical_or(i == nlat - 1, i == ntot - 1)
    prow = jnp.where(seg_first, 0.0, prev_ref[0, 7:8, :])
    nrow = jnp.where(seg_last, 0.0, next_ref[0, 0:1, :])
    rid = lax.broadcasted_iota(jnp.int32, x.shape, 0)
    xm = jnp.where(rid == 0, prow, pltpu.roll(x, 1, axis=0))
    xp = jnp.where(rid == tm - 1, nrow, pltpu.roll(x, tm - 1, axis=0))
    w = w_ref[...]
    y = xm * w[0:1] + x * w[1:2] + xp * w[2:3]
    y = y * _sigmoid(y)
    q_ref[0] = (y[:, :D_HEADS] * (HEAD_W ** -0.5)).astype(BF16)
    k_ref[0] = y[:, D_HEADS:].astype(BF16)


def _qkconv(qk_raw, w, nlat):
    b, ttot, _ = qk_raw.shape
    tm = ROW_TILE
    nt = ttot // tm
    r8 = tm // 8
    last8 = ttot // 8 - 1
    return pl.pallas_call(
        functools.partial(_qkconv_kernel, tm=tm, nlat=nlat, ntot=nt), name="qk_conv",
        grid=(b, nt),
        in_specs=[pl.BlockSpec((1, tm, QK_W), lambda bi, i: (bi, i, 0)),
                  pl.BlockSpec((1, 8, QK_W), lambda bi, i: (bi, jnp.maximum(i * r8 - 1, 0), 0)),
                  pl.BlockSpec((1, 8, QK_W), lambda bi, i: (bi, jnp.minimum((i + 1) * r8, last8), 0)),
                  _const_spec((3, QK_W))],
        out_specs=[pl.BlockSpec((1, tm, D_HEADS), lambda bi, i: (bi, i, 0))] * 2,
        out_shape=[jax.ShapeDtypeStruct((b, ttot, D_HEADS), BF16)] * 2,
        compiler_params=_cparams("parallel", "parallel"),
    )(qk_raw, qk_raw, qk_raw, w)


def _mlstm_kernel(q_ref, k_ref, v_ref, g_ref, h_ref, c_scr, m_scr, *, chunk):
    d = pl.program_id(1)
    c = pl.program_id(2)

    @pl.when(c == 0)
    def _():
        c_scr[...] = jnp.zeros_like(c_scr)
        m_scr[...] = jnp.zeros_like(m_scr)

    gates = g_ref[0]
    logf = jnp.minimum(gates, 0.0) - jnp.log(1.0 + jnp.exp(-jnp.abs(gates)))
    row = lax.broadcasted_iota(jnp.int32, (chunk, chunk), 0)
    col = lax.broadcasted_iota(jnp.int32, (chunk, chunk), 1)
    mask = (col - row) * (1 - 2 * d) <= 0
    a_all = jnp.dot(mask.astype(F32), logf, precision=lax.Precision.HIGHEST,
                    preferred_element_type=F32)
    r_all = gates - pltpu.roll(a_all, 128 - HEADS, axis=1)
    r_rows = r_all.T
    one_col = (lax.broadcasted_iota(jnp.int32, (chunk, HEAD_W), 1) == 0)
    one_bf = one_col.astype(BF16)
    one_f = one_col.astype(F32)

    for h in range(HEADS):
        sl = slice(h * HEAD_W, (h + 1) * HEAD_W)
        qh = q_ref[0, :, sl]
        kh = k_ref[0, :, sl]
        vh = v_ref[0, :, sl]
        m_prev = m_scr[h:h + 1, 0:1]
        a_col = a_all[:, HEADS + h:HEADS + h + 1]
        r_col = r_all[:, h:h + 1]
        rm = jnp.where(mask, r_rows[h:h + 1, :], NEG)
        mx = jnp.maximum(jnp.max(rm, axis=1, keepdims=True), m_prev)
        s = _dot_nt(qh, kh) * jnp.exp(rm - mx)
        vext = jnp.concatenate([vh, one_bf], axis=1)
        ct = c_scr[h]
        nd = _dot(s.astype(BF16), vext) + jnp.exp(m_prev - mx) * _dot(qh, ct.astype(BF16))
        den = jnp.maximum(jnp.abs(nd[:, HEAD_W:HEAD_W + 1]), jnp.exp(-(a_col + mx)))
        h_ref[0, 0, :, sl] = nd[:, :HEAD_W] / den
        mx_last = jnp.max(mx, axis=0, keepdims=True)
        a_last = jnp.min(a_col, axis=0, keepdims=True)
        w_col = jnp.exp(r_col - mx_last)
        wv = jnp.concatenate([w_col * vh.astype(F32), w_col * one_f], axis=1).astype(BF16)
        c_scr[h] = jnp.exp(m_prev - mx_last) * ct + _dot_tn(kh, wv)
        m_scr[h:h + 1, :] = jnp.broadcast_to(a_last + mx_last, (1, 128))


def _mlstm(q, k, v, gates, nlat_rows):
    b, ttot, _ = q.shape
    L = MLSTM_CHUNK
    nc = ttot // L
    ncl = nlat_rows // L
    ncc = nc - ncl

    def blk(d, c):
        fwd = jnp.where(c < ncc, ncl + c, c - ncc)
        bwd = jnp.where(c < ncc, nc - 1 - c, ncl - 1 - (c - ncc))
        return jnp.where(d == 0, fwd, bwd)

    head_spec = pl.BlockSpec((1, L, D_HEADS), lambda bi, d, c: (bi, blk(d, c), 0))
    return pl.pallas_call(
        functools.partial(_mlstm_kernel, chunk=L), name="mlstm_scan",
        grid=(b, 2, nc),
        in_specs=[head_spec, head_spec, head_spec,
                  pl.BlockSpec((1, L, 128), lambda bi, d, c: (bi, blk(d, c), d))],
        out_specs=pl.BlockSpec((1, 1, L, D_HEADS), lambda bi, d, c: (d, bi, blk(d, c), 0)),
        out_shape=jax.ShapeDtypeStruct((2, b, ttot, D_HEADS), F32),
        scratch_shapes=[pltpu.VMEM((HEADS, HEAD_W, 2 * HEAD_W), F32),
                        pltpu.VMEM((8, 128), F32)],
        compiler_params=_cparams("parallel", "parallel", "arbitrary"),
    )(q, k, v, gates)


def _fft1_kernel(p_ref, g_ref, o_ref, *, nb):
    for j in range(nb):
        pr = p_ref[0, :, j * 512:j * 512 + 256]
        pi = p_ref[0, :, j * 512 + 256:(j + 1) * 512]
        bb = _dot(g_ref[j].astype(BF16), jnp.concatenate([pr, pi], axis=0))
        o_ref[0, 0, :, j * 256:(j + 1) * 256] = bb[:FFT_N1].astype(BF16)
        o_ref[0, 1, :, j * 256:(j + 1) * 256] = bb[FFT_N1:].astype(BF16)


def _fft2_kernel(b_ref, t_ref, o_ref, *, kb, scale):
    tc = t_ref[0].astype(BF16)
    ts = t_ref[1].astype(BF16)
    for j in range(kb):
        y = _dot(tc, b_ref[0, 0, j]) + _dot(ts, b_ref[0, 1, j])
        o_ref[0, j] = y * scale


def _dft_ctx_kernel(p_ref, t_ref, o_ref, *, scale):
    p = p_ref[0]
    st = jnp.concatenate([p[:, :256], p[:, 256:]], axis=0)
    o_ref[0] = _dot(t_ref[...].astype(BF16), st) * scale


def _fft_tables(t):
    n1, n2 = FFT_N1, t // FFT_N1
    k1 = np.arange(n1, dtype=np.int64)[None, :, None]
    nn = (n2 * np.arange(n1, dtype=np.int64)[None, None, :] + np.arange(n2, dtype=np.int64)[:, None, None])
    ang = 2.0 * np.pi * ((k1 * nn) % t).astype(np.float64) / t
    gr, gi = np.cos(ang), -np.sin(ang)
    g = np.concatenate([np.concatenate([gr, -gi], axis=2), np.concatenate([gi, gr], axis=2)], axis=1)
    a2 = 2.0 * np.pi * ((np.arange(n2)[:, None] * np.arange(n2)[None, :]) % n2) / n2
    t2 = np.stack([np.cos(a2), np.sin(a2)])
    return jnp.asarray(g, F32), jnp.asarray(t2, F32)


def _dft_matrix_cs(n):
    a = 2.0 * np.pi * ((np.arange(n)[:, None] * np.arange(n)[None, :]) % n) / n
    return np.cos(a), np.sin(a)


def _channel_dft():
    c, s = _dft_matrix_cs(FNET_GC)
    eye = np.eye(D_SIDE // FNET_GC)
    return jnp.asarray(np.concatenate([np.kron(eye, c), -np.kron(eye, s)], axis=1), F32)


def _fourier(p, t, ctx_len):
    b = p.shape[0]
    n1, n2 = FFT_N1, t // FFT_N1
    g, t2 = _fft_tables(t)
    nb = min(8, n2)
    kb = 16
    p_lat = p[:, :t].reshape(b, n1, n2 * 512)
    st1 = pl.pallas_call(
        functools.partial(_fft1_kernel, nb=nb), name="fft_stage1",
        grid=(b, n2 // nb),
        in_specs=[pl.BlockSpec((1, n1, nb * 512), lambda bi, j: (bi, 0, j)),
                  pl.BlockSpec((nb, 256, 256), lambda bi, j: (j, 0, 0))],
        out_specs=pl.BlockSpec((1, 2, n1, nb * 256), lambda bi, j: (bi, 0, 0, j)),
        out_shape=jax.ShapeDtypeStruct((b, 2, n1, n2 * 256), BF16),
        compiler_params=_cparams("parallel", "parallel"),
    )(p_lat, g)
    st1 = st1.reshape(b, 2, n1, n2, 256)
    y = pl.pallas_call(
        functools.partial(_fft2_kernel, kb=kb, scale=1.0 / math.sqrt(t * FNET_GC)), name="fft_stage2",
        grid=(b, n1 // kb),
        in_specs=[pl.BlockSpec((1, 2, kb, n2, 256), lambda bi, j: (bi, 0, j, 0, 0)),
                  _const_spec((2, n2, n2))],
        out_specs=pl.BlockSpec((1, kb, n2, 256), lambda bi, j: (bi, j, 0, 0)),
        out_shape=jax.ShapeDtypeStruct((b, n1, n2, 256), F32),
        compiler_params=_cparams("parallel", "parallel"),
    )(st1, t2)
    y_lat = y.transpose(0, 2, 1, 3).reshape(b, t, 256)
    cc, sc = _dft_matrix_cs(ctx_len)
    tc = jnp.asarray(np.concatenate([cc, sc], axis=1), F32)
    y_ctx = pl.pallas_call(
        functools.partial(_dft_ctx_kernel, scale=1.0 / math.sqrt(ctx_len * FNET_GC)), name="dft_ctx",
        grid=(b,),
        in_specs=[pl.BlockSpec((1, ctx_len, 512), lambda bi: (bi, t // ctx_len, 0)),
                  _const_spec((ctx_len, 2 * ctx_len))],
        out_specs=pl.BlockSpec((1, ctx_len, 256), lambda bi: (bi, 0, 0)),
        out_shape=jax.ShapeDtypeStruct((b, ctx_len, 256), F32),
        compiler_params=_cparams("parallel"),
    )(p, tc)
    return y_lat, y_ctx


def _even_finish_kernel(hf_ref, hb_ref, o_ref, yl_ref, yc_ref, x_ref, ctx_ref, mod_ref, gh_ref, w_ref,
                        out_ref, *, nlat):
    i = pl.program_id(1)
    hs = hf_ref[0, 0] + hb_ref[0, 0]
    gate = _sigmoid(o_ref[0])
    gh = gh_ref[...]
    parts = []
    for h in range(HEADS):
        sl = slice(h * HEAD_W, (h + 1) * HEAD_W)
        parts.append((_rms(hs[:, sl], gh[:, sl]) * gate[:, sl]).astype(BF16))
    y = jnp.where(i >= nlat, yc_ref[0], yl_ref[0])
    parts.append(y.astype(BF16))
    out = _dot(jnp.concatenate(parts, axis=1), w_ref[...])
    out_ref[0] = _tile_rows(x_ref, ctx_ref, nlat) + mod_ref[0][2:3] * out


def _even_finish(h2, o, y_lat, y_ctx, x, ctx, mod, g_head, w_out, nlat):
    b, t, d = x.shape
    ttot = t + ctx.shape[1]
    tm = ROW_TILE
    nt = ttot // tm
    row = lambda w: pl.BlockSpec((1, tm, w), lambda bi, i: (bi, i, 0))
    return pl.pallas_call(
        functools.partial(_even_finish_kernel, nlat=nlat), name="even_finish",
        grid=(b, nt),
        in_specs=[pl.BlockSpec((1, 1, tm, D_HEADS), lambda bi, i: (0, bi, i, 0)),
                  pl.BlockSpec((1, 1, tm, D_HEADS), lambda bi, i: (1, bi, i, 0)),
                  row(D_HEADS),
                  pl.BlockSpec((1, tm, 256), lambda bi, i: (bi, jnp.minimum(i, nlat - 1), 0)),
                  pl.BlockSpec((1, tm, 256), lambda bi, i: (bi, jnp.maximum(i - nlat, 0), 0))]
        + _lat_ctx_specs(d, tm, nlat)
        + [_mod_spec(b, nlat), _const_spec((1, D_HEADS)), _const_spec((d, d))],
        out_specs=row(d),
        out_shape=jax.ShapeDtypeStruct((b, ttot, d), F32),
        compiler_params=_cparams("parallel", "parallel"),
    )(h2, h2, o, y_lat, y_ctx, x, ctx, mod, g_head, w_out)


def _mlp_kernel(x_ref, mod_ref, g_ref, w1_ref, w2_ref, gf_ref, out_ref, *, final):
    x = x_ref[0]
    m = mod_ref[0]
    h = _norm_mod(x, g_ref[...], m[3:4], m[4:5]).astype(BF16)
    a = jnp.maximum(_dot(h, w1_ref[...]), 0.0)
    y = _dot((a * a).astype(BF16), w2_ref[...])
    xn = x + m[5:6] * y
    if final:
        xn = _rms(xn, gf_ref[...])
    out_ref[0] = xn


def _mlp(x, mod, g, w1, w2, g_final, nlat, rows_out, final):
    b, _, d = x.shape
    tm = ROW_TILE
    nt = rows_out // tm
    row = pl.BlockSpec((1, tm, d), lambda bi, i: (bi, i, 0))
    single = dict(pipeline_mode=pl.Buffered(1))
    return pl.pallas_call(
        functools.partial(_mlp_kernel, final=final), name="mlp_final" if final else "mlp",
        grid=(b, nt),
        in_specs=[row, _mod_spec(b, nlat), _const_spec((1, d)),
                  pl.BlockSpec(w1.shape, lambda bi, i: (0, 0), **single),
                  pl.BlockSpec(w2.shape, lambda bi, i: (0, 0), **single),
                  _const_spec((1, d))],
        out_specs=row,
        out_shape=jax.ShapeDtypeStruct((b, rows_out, d), F32),
        compiler_params=_cparams("parallel", "parallel"),
    )(x, mod, g, w1, w2, g_final)


def _swap_halves(x):
    lane = lax.broadcasted_iota(jnp.int32, x.shape, 1)
    return jnp.where(lane % 64 < 32, pltpu.roll(x, 96, axis=1), pltpu.roll(x, 32, axis=1))


def _proj_odd_kernel(x_ref, mod_ref, g_ref, w_ref, cos_ref, sin_ref, q_ref, k_ref, v_ref, glu_ref):
    m = mod_ref[0]
    h = _norm_mod(x_ref[0], g_ref[...], m[0:1], m[1:2]).astype(BF16)
    u = _dot(h, w_ref[...])
    cos = cos_ref[...]
    sin = sin_ref[...]
    for hh in range(HEADS):
        sl = slice(hh * HEAD_W, (hh + 1) * HEAD_W)
        qh = u[:, sl]
        kh = u[:, D_HEADS + hh * HEAD_W:D_HEADS + (hh + 1) * HEAD_W]
        q_ref[0, :, sl] = ((qh * cos + _swap_halves(qh) * sin) * Q_SCALE_LOG2).astype(BF16)
        k_ref[0, :, sl] = (kh * cos + _swap_halves(kh) * sin).astype(BF16)
    v_ref[0] = u[:, 2 * D_HEADS:3 * D_HEADS].astype(BF16)
    a = u[:, 3 * D_HEADS:3 * D_HEADS + D_SIDE]
    gte = u[:, 3 * D_HEADS + D_SIDE:]
    glu_ref[0] = a * _sigmoid(gte)


def _rope_tables(t, ctx_len):
    rows = t // GRID_W
    row = jnp.repeat(jnp.arange(rows, dtype=F32), GRID_W)
    col = jnp.tile(jnp.arange(GRID_W, dtype=F32), rows)
    n_freq = DIFF_DH // 4
    inv = ROPE_BASE ** (-jnp.arange(n_freq, dtype=F32) / n_freq)
    ang = jnp.concatenate([row[:, None] * inv, col[:, None] * inv], axis=-1)
    cos, sin = jnp.cos(ang), jnp.sin(ang)
    cos = jnp.concatenate([cos, cos, cos, cos], axis=-1)
    sin = jnp.concatenate([-sin, sin, -sin, sin], axis=-1)
    cos = jnp.concatenate([cos, jnp.ones((ctx_len, 128), F32)], axis=0)
    sin = jnp.concatenate([sin, jnp.zeros((ctx_len, 128), F32)], axis=0)
    return cos, sin


def _proj_odd(x, mod, g, w_all, cos, sin, nlat):
    b, ttot, d = x.shape
    tm = ROW_TILE
    nt = ttot // tm
    row = lambda w: pl.BlockSpec((1, tm, w), lambda bi, i: (bi, i, 0))
    tab = pl.BlockSpec((tm, 128), lambda bi, i: (i, 0))
    return pl.pallas_call(
        _proj_odd_kernel, name="proj_odd",
        grid=(b, nt),
        in_specs=[row(d), _mod_spec(b, nlat), _const_spec((1, d)), _const_spec(w_all.shape), tab, tab],
        out_specs=[row(D_HEADS), row(D_HEADS), row(D_HEADS), row(D_SIDE)],
        out_shape=[jax.ShapeDtypeStruct((b, ttot, D_HEADS), BF16)] * 3
        + [jax.ShapeDtypeStruct((b, ttot, D_SIDE), F32)],
        compiler_params=_cparams("parallel", "parallel"),
    )(x, mod, g, w_all, cos, sin)


def _attn_kernel(q_ref, k_ref, v_ref, lam_ref, gs_ref, o_ref, acc0, acc1, *, tk, nk, lam_init):
    q = q_ref[0]
    q0 = q[:, :DIFF_DH]
    q1 = q[:, DIFF_DH:]
    tq = q.shape[0]
    one_bf = (lax.broadcasted_iota(jnp.int32, (tk, HEAD_W), 1) == 0).astype(BF16)
    acc0[...] = jnp.zeros_like(acc0)
    acc1[...] = jnp.zeros_like(acc1)

    def online(s, m_old, acc, vext):
        m_new = jnp.maximum(m_old, jnp.max(s, axis=1, keepdims=True))
        p = jnp.exp2(s - m_new).astype(BF16)
        acc[...] = jnp.exp2(m_old - m_new) * acc[...] + _dot(p, vext)
        return m_new

    def body(j, carry):
        m0, m1 = carry
        start = pl.multiple_of(j * tk, tk)
        kk = k_ref[0, pl.ds(start, tk), :]
        vext = jnp.concatenate([v_ref[0, pl.ds(start, tk), :], one_bf], axis=1)
        m0 = online(_dot_nt(q0, kk[:, :DIFF_DH]), m0, acc0, vext)
        m1 = online(_dot_nt(q1, kk[:, DIFF_DH:]), m1, acc1, vext)
        return m0, m1

    init = jnp.full((tq, 1), NEG, F32)
    carry = (init, init)
    for j in range(nk):
        carry = body(j, carry)
    lp = lam_ref[...]
    lam = (jnp.exp(jnp.sum(lp[0:1] * lp[1:2], axis=1, keepdims=True))
           - jnp.exp(jnp.sum(lp[2:3] * lp[3:4], axis=1, keepdims=True)) + lam_init)
    a0 = acc0[...]
    a1 = acc1[...]
    o = a0[:, :HEAD_W] / a0[:, HEAD_W:HEAD_W + 1] - lam * (a1[:, :HEAD_W] / a1[:, HEAD_W:HEAD_W + 1])
    o_ref[0] = _rms(o, gs_ref[...]) * (1.0 - lam_init)


def _diff_attention(q, k, v, lam_p, g_sub, t, lam_init):
    b, ttot, _ = q.shape
    tq = ATTN_TQ
    tk = max(n for n in range(128, ATTN_TK + 1, 128) if ttot % n == 0)
    return pl.pallas_call(
        functools.partial(_attn_kernel, tk=tk, nk=ttot // tk, lam_init=lam_init), name="diff_attn",
        grid=(b, HEADS, t // tq),
        in_specs=[pl.BlockSpec((1, tq, HEAD_W), lambda bi, h, i: (bi, i, h)),
                  pl.BlockSpec((1, ttot, HEAD_W), lambda bi, h, i: (bi, 0, h)),
                  pl.BlockSpec((1, ttot, HEAD_W), lambda bi, h, i: (bi, 0, h)),
                  _const_spec(lam_p.shape), _const_spec((1, HEAD_W))],
        out_specs=pl.BlockSpec((1, tq, HEAD_W), lambda bi, h, i: (bi, i, h)),
        out_shape=jax.ShapeDtypeStruct((b, t, D_HEADS), F32),
        scratch_shapes=[pltpu.VMEM((tq, 2 * HEAD_W), F32)] * 2,
        compiler_params=_cparams("parallel", "parallel", "parallel"),
    )(q, k, v, lam_p, g_sub)


def _odd_finish_kernel(o_ref, glu_ref, prev_ref, next_ref, wdw_ref, gln_ref, bln_ref, x_ref, mod_ref,
                       w_ref, out_ref, xs, *, tm, nlat):
    i = pl.program_id(1)
    hal = CONV_HALO
    xs[0:hal, :] = jnp.where(i == 0, 0.0, prev_ref[0])
    xs[hal:hal + tm, :] = glu_ref[0]
    xs[hal + tm:, :] = jnp.where(i == nlat - 1, 0.0, next_ref[0])
    wdw = wdw_ref[...]
    z = jnp.zeros((tm, D_SIDE), F32)
    off = hal - (CONV_W - 1) // 2
    for kk in range(CONV_W):
        z = z + xs[off + kk:off + kk + tm, :] * wdw[kk:kk + 1]
    mu = jnp.mean(z, axis=-1, keepdims=True)
    zc = z - mu
    var = jnp.mean(zc * zc, axis=-1, keepdims=True)
    z = zc * lax.rsqrt(var + EPS) * gln_ref[...] + bln_ref[...]
    z = z * _sigmoid(z)
    cat = jnp.concatenate([o_ref[0].astype(BF16), z.astype(BF16)], axis=1)
    out_ref[0] = x_ref[0] + mod_ref[0][2:3] * _dot(cat, w_ref[...])


def _odd_finish(o, glu, w_dw, g_ln, b_ln, x, mod, w_out, t):
    b, _, d = x.shape
    tm = ROW_TILE
    nlat = t // tm
    r = tm // CONV_HALO
    row = lambda w: pl.BlockSpec((1, tm, w), lambda bi, i: (bi, i, 0))
    return pl.pallas_call(
        functools.partial(_odd_finish_kernel, tm=tm, nlat=nlat), name="odd_finish",
        grid=(b, nlat),
        in_specs=[row(D_HEADS), row(D_SIDE),
                  pl.BlockSpec((1, CONV_HALO, D_SIDE), lambda bi, i: (bi, jnp.maximum(i * r - 1, 0), 0)),
                  pl.BlockSpec((1, CONV_HALO, D_SIDE), lambda bi, i: (bi, (i + 1) * r, 0)),
                  _const_spec((CONV_W, D_SIDE)), _const_spec((1, D_SIDE)), _const_spec((1, D_SIDE)),
                  row(d), _mod_spec(b, nlat), _const_spec((d, d))],
        out_specs=row(d),
        out_shape=jax.ShapeDtypeStruct((b, t, d), F32),
        scratch_shapes=[pltpu.VMEM((tm + 2 * CONV_HALO, D_SIDE), F32)],
        compiler_params=_cparams("parallel", "parallel"),
    )(o, glu, glu, glu, w_dw, g_ln, b_ln, x, mod, w_out)


def kernel(x, c, ctx, c_ctx, w_mod, b_mod, g_norm, w_in_even, b_gate, w_qk_conv, g_mlstm_head,
           w_out_even, w_in_odd, lam_p, g_subln, w_dw, g_conv_ln, b_conv_ln, w_out_odd,
           w_ff1, w_ff2, g_final):
    b, t, d = x.shape
    ctx_len = ctx.shape[1]
    assert w_mod.shape[0] == 2 and d == 1024 and ctx_len == ROW_TILE and t % (FFT_N1 * 8) == 0
    nlat = t // ROW_TILE
    mod = _mod_vectors(c, c_ctx, w_mod, b_mod)

    we = w_in_even[0]
    g0 = 4 * D_HEADS
    w_gate = jnp.zeros((d, 256), F32).at[:, 0:12].set(we[:, g0:g0 + 12]).at[:, 128:140].set(we[:, g0 + 12:g0 + 24])
    w_all = jnp.concatenate([we[:, :g0], w_gate, we[:, g0 + 24:]], axis=1).astype(BF16)
    bg = jnp.zeros((1, 256), F32).at[0, 0:12].set(b_gate[0, :12]).at[0, 128:140].set(b_gate[0, 12:])
    qk_raw, v, o, gates, p = _proj_even(x, ctx, mod[0], g_norm[0, 0][None], w_all, bg, _channel_dft(), nlat)
    q, k = _qkconv(qk_raw, w_qk_conv[0], nlat)
    h2 = _mlstm(q, k, v, gates, t)
    y_lat, y_ctx = _fourier(p, t, ctx_len)
    x1 = _even_finish(h2, o, y_lat, y_ctx, x, ctx, mod[0], g_mlstm_head[0][None],
                      w_out_even[0].astype(BF16), nlat)
    x2 = _mlp(x1, mod[0], g_norm[0, 1][None], w_ff1[0].astype(BF16), w_ff2[0].astype(BF16),
              g_final[None], nlat, t + ctx_len, False)

    lam_init = 0.8 - 0.6 * math.exp(-0.3 * 1)
    cos, sin = _rope_tables(t, ctx_len)
    qa, ka, va, glu = _proj_odd(x2, mod[1], g_norm[1, 0][None], w_in_odd[0].astype(BF16), cos, sin, nlat)
    oa = _diff_attention(qa, ka, va, lam_p[0], g_subln[0][None], t, lam_init)
    x3 = _odd_finish(oa, glu, w_dw[0], g_conv_ln[0][None], b_conv_ln[0][None], x2, mod[1],
                     w_out_odd[0].astype(BF16), t)
    return _mlp(x3, mod[1], g_norm[1, 1][None], w_ff1[1].astype(BF16), w_ff2[1].astype(BF16),
                g_final[None], nlat, t, True)
```

```python
import functools
import math

import numpy as np
import jax
import jax.numpy as jnp
from jax import lax
from jax.experimental import pallas as pl
from jax.experimental.pallas import tpu as pltpu

F32 = jnp.float32
BF16 = jnp.bfloat16

EPS = 1e-6
HEADS = 6
HEAD_W = 128
D_HEADS = HEADS * HEAD_W
D_SIDE = 256
FNET_GC = 64
QK_W = 2 * D_HEADS
CONV_W = 31
CONV_HALO = 16
GRID_W = 64
ROPE_BASE = 10000.0
DIFF_DH = 64
FFT_N1 = 128
ROW_TILE = 256
MLSTM_CHUNK = 256
MLSTM_SUM_ROWS = 16
ATTN_TQ = 512
ATTN_TK = 1408
NEG = -1e30
Q_SCALE_LOG2 = (DIFF_DH ** -0.5) * math.log2(math.e)
VMEM_LIMIT = 56 * 1024 * 1024


def _cparams(*sem, flags=None):
    return pltpu.CompilerParams(dimension_semantics=sem, vmem_limit_bytes=VMEM_LIMIT, flags=flags)


def _dot(a, b):
    return jnp.dot(a, b, preferred_element_type=F32)


def _dot_nt(a, b):
    return lax.dot_general(a, b, (((1,), (1,)), ((), ())), preferred_element_type=F32)


def _dot_tn(a, b):
    return lax.dot_general(a, b, (((0,), (0,)), ((), ())), preferred_element_type=F32)


def _sigmoid(x):
    return 1.0 / (1.0 + jnp.exp(-x))


def _rms(x, g):
    return x * lax.rsqrt(jnp.mean(x * x, axis=-1, keepdims=True) + EPS) * g


def _norm_mod(x, g, shift, scale):
    return _rms(x, g) * (1.0 + scale) + shift


def _mod_kernel(s_ref, w_ref, b_ref, o_ref):
    s = s_ref[...]
    s = s * _sigmoid(s)
    o_ref[0] = _dot(s.astype(BF16), w_ref[0].astype(BF16)) + b_ref[0]


def _mod_vectors(c, c_ctx, w_mod, b_mod):
    depth, d, n = w_mod.shape
    b = c.shape[0]
    s = jnp.zeros((8, d), F32).at[:b].set(c).at[b].set(c_ctx)
    tn = 1536
    out = pl.pallas_call(
        _mod_kernel, name="adaln_mod",
        grid=(depth, n // tn),
        in_specs=[pl.BlockSpec((8, d), lambda l, j: (0, 0)),
                  pl.BlockSpec((1, d, tn), lambda l, j: (l, 0, j)),
                  pl.BlockSpec((1, 1, tn), lambda l, j: (l, 0, j))],
        out_specs=pl.BlockSpec((1, 8, tn), lambda l, j: (l, 0, j)),
        out_shape=jax.ShapeDtypeStruct((depth, 8, n), F32),
        compiler_params=_cparams("parallel", "parallel"),
    )(s, w_mod, b_mod.reshape(depth, 1, n))
    return out.reshape(depth, 8, 6, d)


def _mod_spec(b, nlat):
    return pl.BlockSpec((1, 6, 1024), lambda bi, i: (jnp.where(i >= nlat, b, bi), 0, 0))


def _const_spec(shape):
    nd = len(shape)
    return pl.BlockSpec(shape, lambda *_: (0,) * nd)


def _tile_rows(x_ref, ctx_ref, nlat):
    return jnp.where(pl.program_id(1) >= nlat, ctx_ref[0], x_ref[0])


def _lat_ctx_specs(d, tm, nlat):
    return [pl.BlockSpec((1, tm, d), lambda bi, i: (bi, jnp.minimum(i, nlat - 1), 0)),
            pl.BlockSpec((1, tm, d), lambda bi, i: (bi, jnp.maximum(i - nlat, 0), 0))]


def _proj_even_kernel(x_ref, ctx_ref, mod_ref, g_ref, w_ref, bg_ref, dft_ref,
                      qk_ref, vt_ref, o_ref, gt_ref, p_ref, *, nlat):
    m = mod_ref[0]
    h = _norm_mod(_tile_rows(x_ref, ctx_ref, nlat), g_ref[...], m[0:1], m[1:2]).astype(BF16)
    u = _dot(h, w_ref[...])
    qk_ref[0] = u[:, :QK_W]
    vt_ref[0] = u[:, QK_W:QK_W + D_HEADS].T.astype(BF16)
    o_ref[0] = u[:, QK_W + D_HEADS:QK_W + 2 * D_HEADS]
    c0 = QK_W + 2 * D_HEADS
    gt_ref[0] = u[:, c0:c0 + 256] + bg_ref[...]
    f = u[:, c0 + 256:c0 + 512].astype(BF16)
    p_ref[0] = _dot(f, dft_ref[...].astype(BF16)).astype(BF16)


def _proj_even(x, ctx, mod, g, w_all, bg, dftc, nlat):
    b, t, d = x.shape
    ttot = t + ctx.shape[1]
    tm = ROW_TILE
    nt = ttot // tm
    row = lambda w: pl.BlockSpec((1, tm, w), lambda bi, i: (bi, i, 0))
    return pl.pallas_call(
        functools.partial(_proj_even_kernel, nlat=nlat), name="proj_even",
        grid=(b, nt),
        in_specs=_lat_ctx_specs(d, tm, nlat) + [_mod_spec(b, nlat), _const_spec((1, d)),
                                                _const_spec(w_all.shape), _const_spec((1, 256)),
                                                _const_spec(dftc.shape)],
        out_specs=[row(QK_W), pl.BlockSpec((1, D_HEADS, tm), lambda bi, i: (bi, 0, i)),
                   row(D_HEADS), row(256), row(512)],
        out_shape=[jax.ShapeDtypeStruct((b, ttot, QK_W), F32),
                   jax.ShapeDtypeStruct((b, D_HEADS, ttot), BF16),
                   jax.ShapeDtypeStruct((b, ttot, D_HEADS), F32),
                   jax.ShapeDtypeStruct((b, ttot, 256), F32),
                   jax.ShapeDtypeStruct((b, ttot, 512), BF16)],
        compiler_params=_cparams("parallel", "parallel"),
    )(x, ctx, mod, g, w_all, bg, dftc)


def _qkconv_kernel(x_ref, prev_ref, next_ref, w_ref, q_ref, k_ref, *, tm, nlat, ntot):
    i = pl.program_id(1)
    x = x_ref[0]
    seg_first = jnp.logical_or(i == 0, i == nlat)
    seg_last = jnp.logical_or(i == nlat - 1, i == ntot - 1)
    prow = jnp.where(seg_first, 0.0, prev_ref[0, 7:8, :])
    nrow = jnp.where(seg_last, 0.0, next_ref[0, 0:1, :])
    rid = lax.broadcasted_iota(jnp.int32, x.shape, 0)
    xm = jnp.where(rid == 0, prow, pltpu.roll(x, 1, axis=0))
    xp = jnp.where(rid == tm - 1, nrow, pltpu.roll(x, tm - 1, axis=0))
    w = w_ref[...]
    y = xm * w[0:1] + x * w[1:2] + xp * w[2:3]
    y = y * _sigmoid(y)
    q_ref[0] = (y[:, :D_HEADS] * (HEAD_W ** -0.5)).astype(BF16)
    k_ref[0] = y[:, D_HEADS:].astype(BF16)


def _qkconv(qk_raw, w, nlat):
    b, ttot, _ = qk_raw.shape
    tm = ROW_TILE
    nt = ttot // tm
    r8 = tm // 8
    last8 = ttot // 8 - 1
    return pl.pallas_call(
        functools.partial(_qkconv_kernel, tm=tm, nlat=nlat, ntot=nt), name="qk_conv",
        grid=(b, nt),
        in_specs=[pl.BlockSpec((1, tm, QK_W), lambda bi, i: (bi, i, 0)),
                  pl.BlockSpec((1, 8, QK_W), lambda bi, i: (bi, jnp.maximum(i * r8 - 1, 0), 0)),
                  pl.BlockSpec((1, 8, QK_W), lambda bi, i: (bi, jnp.minimum((i + 1) * r8, last8), 0)),
                  _const_spec((3, QK_W))],
        out_specs=[pl.BlockSpec((1, tm, D_HEADS), lambda bi, i: (bi, i, 0))] * 2,
        out_shape=[jax.ShapeDtypeStruct((b, ttot, D_HEADS), BF16)] * 2,
        compiler_params=_cparams("parallel", "parallel"),
    )(qk_raw, qk_raw, qk_raw, w)


def _mlstm_kernel(q_ref, k_ref, vt_ref, g_ref, h_ref, c_scr, m_scr, *, chunk):
    d = pl.program_id(1)
    c = pl.program_id(2)

    @pl.when(c == 0)
    def _():
        c_scr[...] = jnp.zeros_like(c_scr)
        m_scr[...] = jnp.zeros_like(m_scr)

    gates = g_ref[0]
    logf = jnp.minimum(gates, 0.0) - jnp.log(1.0 + jnp.exp(-jnp.abs(gates)))
    row = lax.broadcasted_iota(jnp.int32, (chunk, chunk), 0)
    col = lax.broadcasted_iota(jnp.int32, (chunk, chunk), 1)
    sgn = 1 - 2 * d
    csum = ((col - row) * sgn <= 0).astype(F32)
    mask_t = (row - col) * sgn <= 0
    a_all = jnp.dot(csum, logf, precision=lax.Precision.HIGHEST,
                    preferred_element_type=F32)
    r_all = gates - pltpu.roll(a_all, 128 - HEADS, axis=1)
    lane = lax.broadcasted_iota(jnp.int32, (chunk, 128), 1)
    rows = jnp.where(lane < HEADS, r_all, a_all).T
    ones = jnp.ones((MLSTM_SUM_ROWS, chunk), BF16)

    for h in range(HEADS):
        sl = slice(h * HEAD_W, (h + 1) * HEAD_W)
        qh = q_ref[0, :, sl]
        kh = k_ref[0, :, sl]
        vext = jnp.concatenate([vt_ref[0, sl, :], ones], axis=0)
        m_prev = m_scr[h:h + 1, 0:1]
        r_row = rows[h:h + 1, :]
        a_row = rows[HEADS + h:HEADS + h + 1, :]
        rm = jnp.where(mask_t, r_all[:, h:h + 1], NEG)
        mx = jnp.maximum(jnp.max(rm, axis=0, keepdims=True), m_prev)
        st = _dot_nt(kh, qh) * jnp.exp(rm - mx)
        ce = c_scr[h]
        nd = _dot(vext, st.astype(BF16)) + jnp.exp(m_prev - mx) * _dot_nt(ce.astype(BF16), qh)
        den = jnp.maximum(jnp.abs(nd[HEAD_W:HEAD_W + 1]), jnp.exp(-(a_row + mx)))
        h_ref[0, 0, sl, :] = nd[:HEAD_W] / den
        mx_last = jnp.max(mx, axis=1, keepdims=True)
        a_last = jnp.min(a_row, axis=1, keepdims=True)
        wv = (vext.astype(F32) * jnp.exp(r_row - mx_last)).astype(BF16)
        c_scr[h] = jnp.exp(m_prev - mx_last) * ce + _dot(wv, kh)
        m_scr[h:h + 1, :] = jnp.broadcast_to(a_last + mx_last, (1, 128))


def _mlstm(q, k, vt, gates, nlat_rows):
    b, ttot, _ = q.shape
    L = MLSTM_CHUNK
    nc = ttot // L
    ncl = nlat_rows // L
    ncc = nc - ncl

    def blk(d, c):
        fwd = jnp.where(c < ncc, ncl + c, c - ncc)
        bwd = jnp.where(c < ncc, nc - 1 - c, ncl - 1 - (c - ncc))
        return jnp.where(d == 0, fwd, bwd)

    head_spec = pl.BlockSpec((1, L, D_HEADS), lambda bi, d, c: (bi, blk(d, c), 0))
    return pl.pallas_call(
        functools.partial(_mlstm_kernel, chunk=L), name="mlstm_scan",
        grid=(b, 2, nc),
        in_specs=[head_spec, head_spec,
                  pl.BlockSpec((1, D_HEADS, L), lambda bi, d, c: (bi, 0, blk(d, c))),
                  pl.BlockSpec((1, L, 128), lambda bi, d, c: (bi, blk(d, c), d))],
        out_specs=pl.BlockSpec((1, 1, D_HEADS, L), lambda bi, d, c: (d, bi, 0, blk(d, c))),
        out_shape=jax.ShapeDtypeStruct((2, b, D_HEADS, ttot), F32),
        scratch_shapes=[pltpu.VMEM((HEADS, HEAD_W + MLSTM_SUM_ROWS, HEAD_W), F32),
                        pltpu.VMEM((8, 128), F32)],
        compiler_params=_cparams("parallel", "parallel", "arbitrary"),
    )(q, k, vt, gates)


def _fft1_kernel(p_ref, g_ref, o_ref, *, nb):
    for j in range(nb):
        pr = p_ref[0, :, j * 512:j * 512 + 256]
        pi = p_ref[0, :, j * 512 + 256:(j + 1) * 512]
        bb = _dot(g_ref[j].astype(BF16), jnp.concatenate([pr, pi], axis=0))
        o_ref[0, 0, :, j * 256:(j + 1) * 256] = bb[:FFT_N1].astype(BF16)
        o_ref[0, 1, :, j * 256:(j + 1) * 256] = bb[FFT_N1:].astype(BF16)


def _fft2_kernel(b_ref, t_ref, o_ref, *, kb, scale):
    tc = t_ref[0].astype(BF16)
    ts = t_ref[1].astype(BF16)
    for j in range(kb):
        y = _dot(tc, b_ref[0, 0, j]) + _dot(ts, b_ref[0, 1, j])
        o_ref[0, j] = y * scale


def _dft_ctx_kernel(p_ref, t_ref, o_ref, *, scale):
    p = p_ref[0]
    st = jnp.concatenate([p[:, :256], p[:, 256:]], axis=0)
    o_ref[0] = _dot(t_ref[...].astype(BF16), st) * scale


def _fft_tables(t):
    n1, n2 = FFT_N1, t // FFT_N1
    k1 = np.arange(n1, dtype=np.int64)[None, :, None]
    nn = (n2 * np.arange(n1, dtype=np.int64)[None, None, :] + np.arange(n2, dtype=np.int64)[:, None, None])
    ang = 2.0 * np.pi * ((k1 * nn) % t).astype(np.float64) / t
    gr, gi = np.cos(ang), -np.sin(ang)
    g = np.concatenate([np.concatenate([gr, -gi], axis=2), np.concatenate([gi, gr], axis=2)], axis=1)
    a2 = 2.0 * np.pi * ((np.arange(n2)[:, None] * np.arange(n2)[None, :]) % n2) / n2
    t2 = np.stack([np.cos(a2), np.sin(a2)])
    return jnp.asarray(g, F32), jnp.asarray(t2, F32)


def _dft_matrix_cs(n):
    a = 2.0 * np.pi * ((np.arange(n)[:, None] * np.arange(n)[None, :]) % n) / n
    return np.cos(a), np.sin(a)


def _channel_dft():
    c, s = _dft_matrix_cs(FNET_GC)
    eye = np.eye(D_SIDE // FNET_GC)
    return jnp.asarray(np.concatenate([np.kron(eye, c), -np.kron(eye, s)], axis=1), F32)


def _fourier(p, t, ctx_len):
    b = p.shape[0]
    n1, n2 = FFT_N1, t // FFT_N1
    g, t2 = _fft_tables(t)
    nb = min(8, n2)
    kb = 16
    p_lat = p[:, :t].reshape(b, n1, n2 * 512)
    st1 = pl.pallas_call(
        functools.partial(_fft1_kernel, nb=nb), name="fft_stage1",
        grid=(b, n2 // nb),
        in_specs=[pl.BlockSpec((1, n1, nb * 512), lambda bi, j: (bi, 0, j)),
                  pl.BlockSpec((nb, 256, 256), lambda bi, j: (j, 0, 0))],
        out_specs=pl.BlockSpec((1, 2, n1, nb * 256), lambda bi, j: (bi, 0, 0, j)),
        out_shape=jax.ShapeDtypeStruct((b, 2, n1, n2 * 256), BF16),
        compiler_params=_cparams("parallel", "parallel"),
    )(p_lat, g)
    st1 = st1.reshape(b, 2, n1, n2, 256)
    y = pl.pallas_call(
        functools.partial(_fft2_kernel, kb=kb, scale=1.0 / math.sqrt(t * FNET_GC)), name="fft_stage2",
        grid=(b, n1 // kb),
        in_specs=[pl.BlockSpec((1, 2, kb, n2, 256), lambda bi, j: (bi, 0, j, 0, 0)),
                  _const_spec((2, n2, n2))],
        out_specs=pl.BlockSpec((1, kb, n2, 256), lambda bi, j: (bi, j, 0, 0)),
        out_shape=jax.ShapeDtypeStruct((b, n1, n2, 256), F32),
        compiler_params=_cparams("parallel", "parallel"),
    )(st1, t2)
    y_lat = y.transpose(0, 2, 1, 3).reshape(b, t, 256)
    cc, sc = _dft_matrix_cs(ctx_len)
    tc = jnp.asarray(np.concatenate([cc, sc], axis=1), F32)
    y_ctx = pl.pallas_call(
        functools.partial(_dft_ctx_kernel, scale=1.0 / math.sqrt(ctx_len * FNET_GC)), name="dft_ctx",
        grid=(b,),
        in_specs=[pl.BlockSpec((1, ctx_len, 512), lambda bi: (bi, t // ctx_len, 0)),
                  _const_spec((ctx_len, 2 * ctx_len))],
        out_specs=pl.BlockSpec((1, ctx_len, 256), lambda bi: (bi, 0, 0)),
        out_shape=jax.ShapeDtypeStruct((b, ctx_len, 256), F32),
        compiler_params=_cparams("parallel"),
    )(p, tc)
    return y_lat, y_ctx


def _even_finish_kernel(hf_ref, hb_ref, o_ref, yl_ref, yc_ref, x_ref, ctx_ref, mod_ref, gh_ref, w_ref,
                        out_ref, *, nlat):
    i = pl.program_id(1)
    hs = (hf_ref[0, 0] + hb_ref[0, 0]).T
    gate = _sigmoid(o_ref[0])
    gh = gh_ref[...]
    parts = []
    for h in range(HEADS):
        sl = slice(h * HEAD_W, (h + 1) * HEAD_W)
        parts.append((_rms(hs[:, sl], gh[:, sl]) * gate[:, sl]).astype(BF16))
    y = jnp.where(i >= nlat, yc_ref[0], yl_ref[0])
    parts.append(y.astype(BF16))
    out = _dot(jnp.concatenate(parts, axis=1), w_ref[...])
    out_ref[0] = _tile_rows(x_ref, ctx_ref, nlat) + mod_ref[0][2:3] * out


def _even_finish(h2, o, y_lat, y_ctx, x, ctx, mod, g_head, w_out, nlat):
    b, t, d = x.shape
    ttot = t + ctx.shape[1]
    tm = ROW_TILE
    nt = ttot // tm
    row = lambda w: pl.BlockSpec((1, tm, w), lambda bi, i: (bi, i, 0))
    return pl.pallas_call(
        functools.partial(_even_finish_kernel, nlat=nlat), name="even_finish",
        grid=(b, nt),
        in_specs=[pl.BlockSpec((1, 1, D_HEADS, tm), lambda bi, i: (0, bi, 0, i)),
                  pl.BlockSpec((1, 1, D_HEADS, tm), lambda bi, i: (1, bi, 0, i)),
                  row(D_HEADS),
                  pl.BlockSpec((1, tm, 256), lambda bi, i: (bi, jnp.minimum(i, nlat - 1), 0)),
                  pl.BlockSpec((1, tm, 256), lambda bi, i: (bi, jnp.maximum(i - nlat, 0), 0))]
        + _lat_ctx_specs(d, tm, nlat)
        + [_mod_spec(b, nlat), _const_spec((1, D_HEADS)), _const_spec((d, d))],
        out_specs=row(d),
        out_shape=jax.ShapeDtypeStruct((b, ttot, d), F32),
        compiler_params=_cparams("parallel", "parallel"),
    )(h2, h2, o, y_lat, y_ctx, x, ctx, mod, g_head, w_out)


def _mlp_kernel(x_ref, mod_ref, g_ref, w1_ref, w2_ref, gf_ref, out_ref, *, final):
    x = x_ref[0]
    m = mod_ref[0]
    h = _norm_mod(x, g_ref[...], m[3:4], m[4:5]).astype(BF16)
    a = jnp.maximum(_dot(h, w1_ref[...]), 0.0)
    y = _dot((a * a).astype(BF16), w2_ref[...])
    xn = x + m[5:6] * y
    if final:
        xn = _rms(xn, gf_ref[...])
    out_ref[0] = xn


def _mlp(x, mod, g, w1, w2, g_final, nlat, rows_out, final):
    b, _, d = x.shape
    tm = ROW_TILE
    nt = rows_out // tm
    row = pl.BlockSpec((1, tm, d), lambda bi, i: (bi, i, 0))
    single = dict(pipeline_mode=pl.Buffered(1))
    return pl.pallas_call(
        functools.partial(_mlp_kernel, final=final), name="mlp_final" if final else "mlp",
        grid=(b, nt),
        in_specs=[row, _mod_spec(b, nlat), _const_spec((1, d)),
                  pl.BlockSpec(w1.shape, lambda bi, i: (0, 0), **single),
                  pl.BlockSpec(w2.shape, lambda bi, i: (0, 0), **single),
                  _const_spec((1, d))],
        out_specs=row,
        out_shape=jax.ShapeDtypeStruct((b, rows_out, d), F32),
        compiler_params=_cparams("parallel", "parallel"),
    )(x, mod, g, w1, w2, g_final)


def _swap_halves(x):
    lane = lax.broadcasted_iota(jnp.int32, x.shape, 1)
    return jnp.where(lane % 64 < 32, pltpu.roll(x, 96, axis=1), pltpu.roll(x, 32, axis=1))


def _proj_odd_kernel(x_ref, mod_ref, g_ref, w_ref, cos_ref, sin_ref, q_ref, k_ref, v_ref, glu_ref):
    m = mod_ref[0]
    h = _norm_mod(x_ref[0], g_ref[...], m[0:1], m[1:2]).astype(BF16)
    u = _dot(h, w_ref[...])
    cos = cos_ref[...]
    sin = sin_ref[...]
    for hh in range(HEADS):
        sl = slice(hh * HEAD_W, (hh + 1) * HEAD_W)
        qh = u[:, sl]
        kh = u[:, D_HEADS + hh * HEAD_W:D_HEADS + (hh + 1) * HEAD_W]
        q_ref[0, :, sl] = ((qh * cos + _swap_halves(qh) * sin) * Q_SCALE_LOG2).astype(BF16)
        k_ref[0, :, sl] = (kh * cos + _swap_halves(kh) * sin).astype(BF16)
    v_ref[0] = u[:, 2 * D_HEADS:3 * D_HEADS].astype(BF16)
    a = u[:, 3 * D_HEADS:3 * D_HEADS + D_SIDE]
    gte = u[:, 3 * D_HEADS + D_SIDE:]
    glu_ref[0] = a * _sigmoid(gte)


def _rope_tables(t, ctx_len):
    rows = t // GRID_W
    row = jnp.repeat(jnp.arange(rows, dtype=F32), GRID_W)
    col = jnp.tile(jnp.arange(GRID_W, dtype=F32), rows)
    n_freq = DIFF_DH // 4
    inv = ROPE_BASE ** (-jnp.arange(n_freq, dtype=F32) / n_freq)
    ang = jnp.concatenate([row[:, None] * inv, col[:, None] * inv], axis=-1)
    cos, sin = jnp.cos(ang), jnp.sin(ang)
    cos = jnp.concatenate([cos, cos, cos, cos], axis=-1)
    sin = jnp.concatenate([-sin, sin, -sin, sin], axis=-1)
    cos = jnp.concatenate([cos, jnp.ones((ctx_len, 128), F32)], axis=0)
    sin = jnp.concatenate([sin, jnp.zeros((ctx_len, 128), F32)], axis=0)
    return cos, sin


def _proj_odd(x, mod, g, w_all, cos, sin, nlat):
    b, ttot, d = x.shape
    tm = ROW_TILE
    nt = ttot // tm
    row = lambda w: pl.BlockSpec((1, tm, w), lambda bi, i: (bi, i, 0))
    tab = pl.BlockSpec((tm, 128), lambda bi, i: (i, 0))
    return pl.pallas_call(
        _proj_odd_kernel, name="proj_odd",
        grid=(b, nt),
        in_specs=[row(d), _mod_spec(b, nlat), _const_spec((1, d)), _const_spec(w_all.shape), tab, tab],
        out_specs=[row(D_HEADS), row(D_HEADS), row(D_HEADS), row(D_SIDE)],
        out_shape=[jax.ShapeDtypeStruct((b, ttot, D_HEADS), BF16)] * 3
        + [jax.ShapeDtypeStruct((b, ttot, D_SIDE), F32)],
        compiler_params=_cparams("parallel", "parallel"),
    )(x, mod, g, w_all, cos, sin)


def _attn_kernel(q_ref, k_ref, v_ref, lam_ref, gs_ref, o_ref, acc0, acc1, *, tk, nk, lam_init):
    q = q_ref[0]
    q0 = q[:, :DIFF_DH]
    q1 = q[:, DIFF_DH:]
    tq = q.shape[0]
    one_bf = (lax.broadcasted_iota(jnp.int32, (tk, HEAD_W), 1) == 0).astype(BF16)
    acc0[...] = jnp.zeros_like(acc0)
    acc1[...] = jnp.zeros_like(acc1)

    def online(s, m_old, acc, vext):
        m_new = jnp.maximum(m_old, jnp.max(s, axis=1, keepdims=True))
        p = jnp.exp2(s - m_new).astype(BF16)
        acc[...] = jnp.exp2(m_old - m_new) * acc[...] + _dot(p, vext)
        return m_new

    m0 = m1 = jnp.full((tq, 1), NEG, F32)
    for j in range(nk):
        kk = k_ref[0, j * tk:(j + 1) * tk, :]
        vext = jnp.concatenate([v_ref[0, j * tk:(j + 1) * tk, :], one_bf], axis=1)
        m0 = online(_dot_nt(q0, kk[:, :DIFF_DH]), m0, acc0, vext)
        m1 = online(_dot_nt(q1, kk[:, DIFF_DH:]), m1, acc1, vext)
    lp = lam_ref[...]
    lam = (jnp.exp(jnp.sum(lp[0:1] * lp[1:2], axis=1, keepdims=True))
           - jnp.exp(jnp.sum(lp[2:3] * lp[3:4], axis=1, keepdims=True)) + lam_init)
    a0 = acc0[...]
    a1 = acc1[...]
    o = a0[:, :HEAD_W] / a0[:, HEAD_W:HEAD_W + 1] - lam * (a1[:, :HEAD_W] / a1[:, HEAD_W:HEAD_W + 1])
    o_ref[0] = _rms(o, gs_ref[...]) * (1.0 - lam_init)


def _diff_attention(q, k, v, lam_p, g_sub, t, lam_init):
    b, ttot, _ = q.shape
    tq = ATTN_TQ
    tk = max(n for n in range(128, ATTN_TK + 1, 128) if ttot % n == 0)
    return pl.pallas_call(
        functools.partial(_attn_kernel, tk=tk, nk=ttot // tk, lam_init=lam_init), name="diff_attn",
        grid=(b, HEADS, t // tq),
        in_specs=[pl.BlockSpec((1, tq, HEAD_W), lambda bi, h, i: (bi, i, h)),
                  pl.BlockSpec((1, ttot, HEAD_W), lambda bi, h, i: (bi, 0, h)),
                  pl.BlockSpec((1, ttot, HEAD_W), lambda bi, h, i: (bi, 0, h)),
                  _const_spec(lam_p.shape), _const_spec((1, HEAD_W))],
        out_specs=pl.BlockSpec((1, tq, HEAD_W), lambda bi, h, i: (bi, i, h)),
        out_shape=jax.ShapeDtypeStruct((b, t, D_HEADS), F32),
        scratch_shapes=[pltpu.VMEM((tq, 2 * HEAD_W), F32)] * 2,
        compiler_params=_cparams("parallel", "parallel", "parallel"),
    )(q, k, v, lam_p, g_sub)


def _odd_finish_kernel(o_ref, glu_ref, prev_ref, next_ref, wdw_ref, gln_ref, bln_ref, x_ref, mod_ref,
                       w_ref, out_ref, xs, *, tm, nlat):
    i = pl.program_id(1)
    hal = CONV_HALO
    xs[0:hal, :] = jnp.where(i == 0, 0.0, prev_ref[0])
    xs[hal:hal + tm, :] = glu_ref[0]
    xs[hal + tm:, :] = jnp.where(i == nlat - 1, 0.0, next_ref[0])
    wdw = wdw_ref[...]
    z = jnp.zeros((tm, D_SIDE), F32)
    off = hal - (CONV_W - 1) // 2
    for kk in range(CONV_W):
        z = z + xs[off + kk:off + kk + tm, :] * wdw[kk:kk + 1]
    mu = jnp.mean(z, axis=-1, keepdims=True)
    zc = z - mu
    var = jnp.mean(zc * zc, axis=-1, keepdims=True)
    z = zc * lax.rsqrt(var + EPS) * gln_ref[...] + bln_ref[...]
    z = z * _sigmoid(z)
    cat = jnp.concatenate([o_ref[0].astype(BF16), z.astype(BF16)], axis=1)
    out_ref[0] = x_ref[0] + mod_ref[0][2:3] * _dot(cat, w_ref[...])


def _odd_finish(o, glu, w_dw, g_ln, b_ln, x, mod, w_out, t):
    b, _, d = x.shape
    tm = ROW_TILE
    nlat = t // tm
    r = tm // CONV_HALO
    row = lambda w: pl.BlockSpec((1, tm, w), lambda bi, i: (bi, i, 0))
    return pl.pallas_call(
        functools.partial(_odd_finish_kernel, tm=tm, nlat=nlat), name="odd_finish",
        grid=(b, nlat),
        in_specs=[row(D_HEADS), row(D_SIDE),
                  pl.BlockSpec((1, CONV_HALO, D_SIDE), lambda bi, i: (bi, jnp.maximum(i * r - 1, 0), 0)),
                  pl.BlockSpec((1, CONV_HALO, D_SIDE), lambda bi, i: (bi, (i + 1) * r, 0)),
                  _const_spec((CONV_W, D_SIDE)), _const_spec((1, D_SIDE)), _const_spec((1, D_SIDE)),
                  row(d), _mod_spec(b, nlat), _const_spec((d, d))],
        out_specs=row(d),
        out_shape=jax.ShapeDtypeStruct((b, t, d), F32),
        scratch_shapes=[pltpu.VMEM((tm + 2 * CONV_HALO, D_SIDE), F32)],
        compiler_params=_cparams("parallel", "parallel"),
    )(o, glu, glu, glu, w_dw, g_ln, b_ln, x, mod, w_out)


def kernel(x, c, ctx, c_ctx, w_mod, b_mod, g_norm, w_in_even, b_gate, w_qk_conv, g_mlstm_head,
           w_out_even, w_in_odd, lam_p, g_subln, w_dw, g_conv_ln, b_conv_ln, w_out_odd,
           w_ff1, w_ff2, g_final):
    b, t, d = x.shape
    ctx_len = ctx.shape[1]
    assert w_mod.shape[0] == 2 and d == 1024 and ctx_len == ROW_TILE and t % (FFT_N1 * 8) == 0
    nlat = t // ROW_TILE
    mod = _mod_vectors(c, c_ctx, w_mod, b_mod)

    we = w_in_even[0]
    g0 = 4 * D_HEADS
    w_gate = jnp.zeros((d, 256), F32).at[:, 0:12].set(we[:, g0:g0 + 12]).at[:, 128:140].set(we[:, g0 + 12:g0 + 24])
    w_all = jnp.concatenate([we[:, :g0], w_gate, we[:, g0 + 24:]], axis=1).astype(BF16)
    bg = jnp.zeros((1, 256), F32).at[0, 0:12].set(b_gate[0, :12]).at[0, 128:140].set(b_gate[0, 12:])
    qk_raw, v, o, gates, p = _proj_even(x, ctx, mod[0], g_norm[0, 0][None], w_all, bg, _channel_dft(), nlat)
    q, k = _qkconv(qk_raw, w_qk_conv[0], nlat)
    h2 = _mlstm(q, k, v, gates, t)
    y_lat, y_ctx = _fourier(p, t, ctx_len)
    x1 = _even_finish(h2, o, y_lat, y_ctx, x, ctx, mod[0], g_mlstm_head[0][None],
                      w_out_even[0].astype(BF16), nlat)
    x2 = _mlp(x1, mod[0], g_norm[0, 1][None], w_ff1[0].astype(BF16), w_ff2[0].astype(BF16),
              g_final[None], nlat, t + ctx_len, False)

    lam_init = 0.8 - 0.6 * math.exp(-0.3 * 1)
    cos, sin = _rope_tables(t, ctx_len)
    qa, ka, va, glu = _proj_odd(x2, mod[1], g_norm[1, 0][None], w_in_odd[0].astype(BF16), cos, sin, nlat)
    oa = _diff_attention(qa, ka, va, lam_p[0], g_subln[0][None], t, lam_init)
    x3 = _odd_finish(oa, glu, w_dw[0], g_conv_ln[0][None], b_conv_ln[0][None], x2, mod[1],
                     w_out_odd[0].astype(BF16), t)
    return _mlp(x3, mod[1], g_norm[1, 1][None], w_ff1[1].astype(BF16), w_ff2[1].astype(BF16),
                g_final[None], nlat, t, True)
```

```python
import functools
import math

import numpy as np
import jax
import jax.numpy as jnp
from jax import lax
from jax.experimental import pallas as pl
from jax.experimental.pallas import tpu as pltpu

F32 = jnp.float32
BF16 = jnp.bfloat16

EPS = 1e-6
SUBLANES = 8
HEADS = 6
HEAD_W = 128
D_HEADS = HEADS * HEAD_W
D_SIDE = 256
FNET_GC = 64
QK_W = 2 * D_HEADS
CONV_W = 31
CONV_HALO = 16
GRID_W = 64
ROPE_BASE = 10000.0
DIFF_DH = 64
FFT_N1 = 128
ROW_TILE = 256
MLSTM_CHUNK = 256
MLSTM_SUM_ROWS = 16
ATTN_TQ = 512
ATTN_TK = 1408
NEG = -1e30
Q_SCALE_LOG2 = (DIFF_DH ** -0.5) * math.log2(math.e)
VMEM_LIMIT = 56 * 1024 * 1024


def _cparams(*sem, flags=None):
    return pltpu.CompilerParams(dimension_semantics=sem, vmem_limit_bytes=VMEM_LIMIT, flags=flags)


def _dot(a, b):
    return jnp.dot(a, b, preferred_element_type=F32)


def _dot_nt(a, b):
    return lax.dot_general(a, b, (((1,), (1,)), ((), ())), preferred_element_type=F32)


def _dot_tn(a, b):
    return lax.dot_general(a, b, (((0,), (0,)), ((), ())), preferred_element_type=F32)


def _sigmoid(x):
    return 1.0 / (1.0 + jnp.exp(-x))


def _rms(x, g):
    return x * lax.rsqrt(jnp.mean(x * x, axis=-1, keepdims=True) + EPS) * g


def _norm_mod(x, g, shift, scale):
    return _rms(x, g) * (1.0 + scale) + shift


def _mod_kernel(s_ref, w_ref, b_ref, o_ref):
    s = s_ref[...]
    s = s * _sigmoid(s)
    o_ref[0] = _dot(s.astype(BF16), w_ref[0].astype(BF16)) + b_ref[0]


def _mod_vectors(c, c_ctx, w_mod, b_mod):
    depth, d, n = w_mod.shape
    b = c.shape[0]
    s = jnp.zeros((8, d), F32).at[:b].set(c).at[b].set(c_ctx)
    tn = 1536
    out = pl.pallas_call(
        _mod_kernel, name="adaln_mod",
        grid=(depth, n // tn),
        in_specs=[pl.BlockSpec((8, d), lambda l, j: (0, 0)),
                  pl.BlockSpec((1, d, tn), lambda l, j: (l, 0, j)),
                  pl.BlockSpec((1, 1, tn), lambda l, j: (l, 0, j))],
        out_specs=pl.BlockSpec((1, 8, tn), lambda l, j: (l, 0, j)),
        out_shape=jax.ShapeDtypeStruct((depth, 8, n), F32),
        compiler_params=_cparams("parallel", "parallel"),
    )(s, w_mod, b_mod.reshape(depth, 1, n))
    return out.reshape(depth, 8, 6, d)


def _mod_spec(b, nlat):
    return pl.BlockSpec((1, 6, 1024), lambda bi, i: (jnp.where(i >= nlat, b, bi), 0, 0))


def _const_spec(shape):
    nd = len(shape)
    return pl.BlockSpec(shape, lambda *_: (0,) * nd)


def _tile_rows(x_ref, ctx_ref, nlat):
    return jnp.where(pl.program_id(1) >= nlat, ctx_ref[0], x_ref[0])


def _lat_ctx_specs(d, tm, nlat):
    return [pl.BlockSpec((1, tm, d), lambda bi, i: (bi, jnp.minimum(i, nlat - 1), 0)),
            pl.BlockSpec((1, tm, d), lambda bi, i: (bi, jnp.maximum(i - nlat, 0), 0))]


def _proj_even_kernel(x_ref, ctx_ref, xp_ref, xn_ref, mod_ref, g_ref, w_ref, bg_ref, dft_ref, wc_ref,
                      q_ref, k_ref, vt_ref, o_ref, gt_ref, p_ref, *, tm, nlat):
    i = pl.program_id(1)
    m = mod_ref[0]
    g = g_ref[...]
    h = _norm_mod(_tile_rows(x_ref, ctx_ref, nlat), g, m[0:1], m[1:2]).astype(BF16)
    xq = _dot(h, w_ref[:, :QK_W])
    u = _dot(h, w_ref[:, QK_W:])
    halo = jnp.concatenate([xp_ref[0], xn_ref[0]], axis=0)
    uh = _dot(_norm_mod(halo, g, m[0:1], m[1:2]).astype(BF16), w_ref[:, :QK_W])
    prow = jnp.where(jnp.logical_or(i == 0, i >= nlat), 0.0, uh[SUBLANES - 1:SUBLANES])
    nrow = jnp.where(i >= nlat - 1, 0.0, uh[SUBLANES:SUBLANES + 1])
    rid = lax.broadcasted_iota(jnp.int32, xq.shape, 0)
    xm = jnp.where(rid == 0, prow, pltpu.roll(xq, 1, axis=0))
    xp = jnp.where(rid == tm - 1, nrow, pltpu.roll(xq, tm - 1, axis=0))
    wc = wc_ref[...]
    y = xm * wc[0:1] + xq * wc[1:2] + xp * wc[2:3]
    y = y * _sigmoid(y)
    q_ref[0] = (y[:, :D_HEADS] * (HEAD_W ** -0.5)).astype(BF16)
    k_ref[0] = y[:, D_HEADS:].astype(BF16)
    vt_ref[0] = u[:, :D_HEADS].T.astype(BF16)
    o_ref[0] = u[:, D_HEADS:2 * D_HEADS]
    c0 = 2 * D_HEADS
    gt_ref[0] = u[:, c0:c0 + 256] + bg_ref[...]
    f = u[:, c0 + 256:c0 + 512].astype(BF16)
    p_ref[0] = _dot(f, dft_ref[...].astype(BF16)).astype(BF16)


def _proj_even(x, ctx, mod, g, w_all, bg, dftc, w_conv, nlat):
    b, t, d = x.shape
    ttot = t + ctx.shape[1]
    tm = ROW_TILE
    nt = ttot // tm
    r8 = tm // SUBLANES
    last8 = t // SUBLANES - 1
    row = lambda w: pl.BlockSpec((1, tm, w), lambda bi, i: (bi, i, 0))
    halo = lambda f: pl.BlockSpec((1, SUBLANES, d), lambda bi, i: (bi, jnp.clip(f(i), 0, last8), 0))
    return pl.pallas_call(
        functools.partial(_proj_even_kernel, tm=tm, nlat=nlat), name="proj_even",
        grid=(b, nt),
        in_specs=_lat_ctx_specs(d, tm, nlat)
        + [halo(lambda i: i * r8 - 1), halo(lambda i: (i + 1) * r8),
           _mod_spec(b, nlat), _const_spec((1, d)), _const_spec(w_all.shape), _const_spec((1, 256)),
           _const_spec(dftc.shape), _const_spec((3, QK_W))],
        out_specs=[row(D_HEADS), row(D_HEADS), pl.BlockSpec((1, D_HEADS, tm), lambda bi, i: (bi, 0, i)),
                   row(D_HEADS), row(256), row(512)],
        out_shape=[jax.ShapeDtypeStruct((b, ttot, D_HEADS), BF16),
                   jax.ShapeDtypeStruct((b, ttot, D_HEADS), BF16),
                   jax.ShapeDtypeStruct((b, D_HEADS, ttot), BF16),
                   jax.ShapeDtypeStruct((b, ttot, D_HEADS), F32),
                   jax.ShapeDtypeStruct((b, ttot, 256), F32),
                   jax.ShapeDtypeStruct((b, ttot, 512), BF16)],
        compiler_params=_cparams("parallel", "parallel"),
    )(x, ctx, x, x, mod, g, w_all, bg, dftc, w_conv)


def _mlstm_kernel(q_ref, k_ref, vt_ref, g_ref, h_ref, c_scr, m_scr, *, chunk):
    d = pl.program_id(1)
    c = pl.program_id(2)

    @pl.when(c == 0)
    def _():
        c_scr[...] = jnp.zeros_like(c_scr)
        m_scr[...] = jnp.zeros_like(m_scr)

    gates = g_ref[0]
    logf = jnp.minimum(gates, 0.0) - jnp.log(1.0 + jnp.exp(-jnp.abs(gates)))
    row = lax.broadcasted_iota(jnp.int32, (chunk, chunk), 0)
    col = lax.broadcasted_iota(jnp.int32, (chunk, chunk), 1)
    sgn = 1 - 2 * d
    csum = ((col - row) * sgn <= 0).astype(F32)
    mask_t = (row - col) * sgn <= 0
    a_all = jnp.dot(csum, logf, precision=lax.Precision.HIGHEST,
                    preferred_element_type=F32)
    r_all = gates - pltpu.roll(a_all, 128 - HEADS, axis=1)
    lane = lax.broadcasted_iota(jnp.int32, (chunk, 128), 1)
    rows = jnp.where(lane < HEADS, r_all, a_all).T
    ones = jnp.ones((MLSTM_SUM_ROWS, chunk), BF16)

    for h in range(HEADS):
        sl = slice(h * HEAD_W, (h + 1) * HEAD_W)
        qh = q_ref[0, :, sl]
        kh = k_ref[0, :, sl]
        vext = jnp.concatenate([vt_ref[0, sl, :], ones], axis=0)
        m_prev = m_scr[h:h + 1, 0:1]
        r_row = rows[h:h + 1, :]
        a_row = rows[HEADS + h:HEADS + h + 1, :]
        rm = jnp.where(mask_t, r_all[:, h:h + 1], NEG)
        mx = jnp.maximum(jnp.max(rm, axis=0, keepdims=True), m_prev)
        st = _dot_nt(kh, qh) * jnp.exp(rm - mx)
        ce = c_scr[h]
        nd = _dot(vext, st.astype(BF16)) + jnp.exp(m_prev - mx) * _dot_nt(ce.astype(BF16), qh)
        den = jnp.maximum(jnp.abs(nd[HEAD_W:HEAD_W + 1]), jnp.exp(-(a_row + mx)))
        h_ref[0, 0, sl, :] = (nd[:HEAD_W] / den).astype(BF16)
        mx_last = jnp.max(mx, axis=1, keepdims=True)
        a_last = jnp.min(a_row, axis=1, keepdims=True)
        wv = (vext.astype(F32) * jnp.exp(r_row - mx_last)).astype(BF16)
        c_scr[h] = jnp.exp(m_prev - mx_last) * ce + _dot(wv, kh)
        m_scr[h:h + 1, :] = jnp.broadcast_to(a_last + mx_last, (1, 128))


def _mlstm(q, k, vt, gates, nlat_rows):
    b, ttot, _ = q.shape
    L = MLSTM_CHUNK
    nc = ttot // L
    ncl = nlat_rows // L
    ncc = nc - ncl

    def blk(d, c):
        fwd = jnp.where(c < ncc, ncl + c, c - ncc)
        bwd = jnp.where(c < ncc, nc - 1 - c, ncl - 1 - (c - ncc))
        return jnp.where(d == 0, fwd, bwd)

    head_spec = pl.BlockSpec((1, L, D_HEADS), lambda bi, d, c: (bi, blk(d, c), 0))
    return pl.pallas_call(
        functools.partial(_mlstm_kernel, chunk=L), name="mlstm_scan",
        grid=(b, 2, nc),
        in_specs=[head_spec, head_spec,
                  pl.BlockSpec((1, D_HEADS, L), lambda bi, d, c: (bi, 0, blk(d, c))),
                  pl.BlockSpec((1, L, 128), lambda bi, d, c: (bi, blk(d, c), d))],
        out_specs=pl.BlockSpec((1, 1, D_HEADS, L), lambda bi, d, c: (d, bi, 0, blk(d, c))),
        out_shape=jax.ShapeDtypeStruct((2, b, D_HEADS, ttot), BF16),
        scratch_shapes=[pltpu.VMEM((HEADS, HEAD_W + MLSTM_SUM_ROWS, HEAD_W), F32),
                        pltpu.VMEM((8, 128), F32)],
        compiler_params=_cparams("parallel", "parallel", "arbitrary"),
    )(q, k, vt, gates)


def _fft1_kernel(p_ref, g_ref, o_ref, *, nb):
    for j in range(nb):
        pr = p_ref[0, :, j * 512:j * 512 + 256]
        pi = p_ref[0, :, j * 512 + 256:(j + 1) * 512]
        bb = _dot(g_ref[j].astype(BF16), jnp.concatenate([pr, pi], axis=0))
        o_ref[0, 0, :, j * 256:(j + 1) * 256] = bb[:FFT_N1].astype(BF16)
        o_ref[0, 1, :, j * 256:(j + 1) * 256] = bb[FFT_N1:].astype(BF16)


def _fft2_kernel(b_ref, t_ref, o_ref, *, kb, scale):
    tc = t_ref[0].astype(BF16)
    ts = t_ref[1].astype(BF16)
    for j in range(kb):
        y = _dot(tc, b_ref[0, 0, j]) + _dot(ts, b_ref[0, 1, j])
        o_ref[0, j] = y * scale


def _dft_ctx_kernel(p_ref, t_ref, o_ref, *, scale):
    p = p_ref[0]
    st = jnp.concatenate([p[:, :256], p[:, 256:]], axis=0)
    o_ref[0] = _dot(t_ref[...].astype(BF16), st) * scale


def _fft_tables(t):
    n1, n2 = FFT_N1, t // FFT_N1
    k1 = np.arange(n1, dtype=np.int64)[None, :, None]
    nn = (n2 * np.arange(n1, dtype=np.int64)[None, None, :] + np.arange(n2, dtype=np.int64)[:, None, None])
    ang = 2.0 * np.pi * ((k1 * nn) % t).astype(np.float64) / t
    gr, gi = np.cos(ang), -np.sin(ang)
    g = np.concatenate([np.concatenate([gr, -gi], axis=2), np.concatenate([gi, gr], axis=2)], axis=1)
    a2 = 2.0 * np.pi * ((np.arange(n2)[:, None] * np.arange(n2)[None, :]) % n2) / n2
    t2 = np.stack([np.cos(a2), np.sin(a2)])
    return jnp.asarray(g, F32), jnp.asarray(t2, F32)


def _dft_matrix_cs(n):
    a = 2.0 * np.pi * ((np.arange(n)[:, None] * np.arange(n)[None, :]) % n) / n
    return np.cos(a), np.sin(a)


def _channel_dft():
    c, s = _dft_matrix_cs(FNET_GC)
    eye = np.eye(D_SIDE // FNET_GC)
    return jnp.asarray(np.concatenate([np.kron(eye, c), -np.kron(eye, s)], axis=1), F32)


def _fourier(p, t, ctx_len):
    b = p.shape[0]
    n1, n2 = FFT_N1, t // FFT_N1
    g, t2 = _fft_tables(t)
    nb = min(8, n2)
    kb = 16
    p_lat = p[:, :t].reshape(b, n1, n2 * 512)
    st1 = pl.pallas_call(
        functools.partial(_fft1_kernel, nb=nb), name="fft_stage1",
        grid=(b, n2 // nb),
        in_specs=[pl.BlockSpec((1, n1, nb * 512), lambda bi, j: (bi, 0, j)),
                  pl.BlockSpec((nb, 256, 256), lambda bi, j: (j, 0, 0))],
        out_specs=pl.BlockSpec((1, 2, n1, nb * 256), lambda bi, j: (bi, 0, 0, j)),
        out_shape=jax.ShapeDtypeStruct((b, 2, n1, n2 * 256), BF16),
        compiler_params=_cparams("parallel", "parallel"),
    )(p_lat, g)
    st1 = st1.reshape(b, 2, n1, n2, 256)
    y = pl.pallas_call(
        functools.partial(_fft2_kernel, kb=kb, scale=1.0 / math.sqrt(t * FNET_GC)), name="fft_stage2",
        grid=(b, n1 // kb),
        in_specs=[pl.BlockSpec((1, 2, kb, n2, 256), lambda bi, j: (bi, 0, j, 0, 0)),
                  _const_spec((2, n2, n2))],
        out_specs=pl.BlockSpec((1, kb, n2, 256), lambda bi, j: (bi, j, 0, 0)),
        out_shape=jax.ShapeDtypeStruct((b, n1, n2, 256), F32),
        compiler_params=_cparams("parallel", "parallel"),
    )(st1, t2)
    y_lat = y.transpose(0, 2, 1, 3).reshape(b, t, 256)
    cc, sc = _dft_matrix_cs(ctx_len)
    tc = jnp.asarray(np.concatenate([cc, sc], axis=1), F32)
    y_ctx = pl.pallas_call(
        functools.partial(_dft_ctx_kernel, scale=1.0 / math.sqrt(ctx_len * FNET_GC)), name="dft_ctx",
        grid=(b,),
        in_specs=[pl.BlockSpec((1, ctx_len, 512), lambda bi: (bi, t // ctx_len, 0)),
                  _const_spec((ctx_len, 2 * ctx_len))],
        out_specs=pl.BlockSpec((1, ctx_len, 256), lambda bi: (bi, 0, 0)),
        out_shape=jax.ShapeDtypeStruct((b, ctx_len, 256), F32),
        compiler_params=_cparams("parallel"),
    )(p, tc)
    return y_lat, y_ctx


def _even_finish_kernel(hf_ref, hb_ref, o_ref, yl_ref, yc_ref, x_ref, ctx_ref, mod_ref, gh_ref, w_ref,
                        out_ref, *, nlat):
    i = pl.program_id(1)
    hs = (hf_ref[0, 0].astype(F32) + hb_ref[0, 0].astype(F32)).T
    gate = _sigmoid(o_ref[0])
    gh = gh_ref[...]
    parts = []
    for h in range(HEADS):
        sl = slice(h * HEAD_W, (h + 1) * HEAD_W)
        parts.append((_rms(hs[:, sl], gh[:, sl]) * gate[:, sl]).astype(BF16))
    y = jnp.where(i >= nlat, yc_ref[0], yl_ref[0])
    parts.append(y.astype(BF16))
    out = _dot(jnp.concatenate(parts, axis=1), w_ref[...])
    out_ref[0] = _tile_rows(x_ref, ctx_ref, nlat) + mod_ref[0][2:3] * out


def _even_finish(h2, o, y_lat, y_ctx, x, ctx, mod, g_head, w_out, nlat):
    b, t, d = x.shape
    ttot = t + ctx.shape[1]
    tm = ROW_TILE
    nt = ttot // tm
    row = lambda w: pl.BlockSpec((1, tm, w), lambda bi, i: (bi, i, 0))
    return pl.pallas_call(
        functools.partial(_even_finish_kernel, nlat=nlat), name="even_finish",
        grid=(b, nt),
        in_specs=[pl.BlockSpec((1, 1, D_HEADS, tm), lambda bi, i: (0, bi, 0, i)),
                  pl.BlockSpec((1, 1, D_HEADS, tm), lambda bi, i: (1, bi, 0, i)),
                  row(D_HEADS),
                  pl.BlockSpec((1, tm, 256), lambda bi, i: (bi, jnp.minimum(i, nlat - 1), 0)),
                  pl.BlockSpec((1, tm, 256), lambda bi, i: (bi, jnp.maximum(i - nlat, 0), 0))]
        + _lat_ctx_specs(d, tm, nlat)
        + [_mod_spec(b, nlat), _const_spec((1, D_HEADS)), _const_spec((d, d))],
        out_specs=row(d),
        out_shape=jax.ShapeDtypeStruct((b, ttot, d), F32),
        compiler_params=_cparams("parallel", "parallel"),
    )(h2, h2, o, y_lat, y_ctx, x, ctx, mod, g_head, w_out)


def _mlp_kernel(x_ref, mod_ref, g_ref, w1_ref, w2_ref, gf_ref, out_ref, *, final):
    x = x_ref[0]
    m = mod_ref[0]
    h = _norm_mod(x, g_ref[...], m[3:4], m[4:5]).astype(BF16)
    a = jnp.maximum(_dot(h, w1_ref[...]), 0.0)
    y = _dot((a * a).astype(BF16), w2_ref[...])
    xn = x + m[5:6] * y
    if final:
        xn = _rms(xn, gf_ref[...])
    out_ref[0] = xn


def _mlp(x, mod, g, w1, w2, g_final, nlat, rows_out, final):
    b, _, d = x.shape
    tm = ROW_TILE
    nt = rows_out // tm
    row = pl.BlockSpec((1, tm, d), lambda bi, i: (bi, i, 0))
    single = dict(pipeline_mode=pl.Buffered(1))
    return pl.pallas_call(
        functools.partial(_mlp_kernel, final=final), name="mlp_final" if final else "mlp",
        grid=(b, nt),
        in_specs=[row, _mod_spec(b, nlat), _const_spec((1, d)),
                  pl.BlockSpec(w1.shape, lambda bi, i: (0, 0), **single),
                  pl.BlockSpec(w2.shape, lambda bi, i: (0, 0), **single),
                  _const_spec((1, d))],
        out_specs=row,
        out_shape=jax.ShapeDtypeStruct((b, rows_out, d), F32),
        compiler_params=_cparams("parallel", "parallel"),
    )(x, mod, g, w1, w2, g_final)


def _swap_halves(x):
    lane = lax.broadcasted_iota(jnp.int32, x.shape, 1)
    return jnp.where(lane % 64 < 32, pltpu.roll(x, 96, axis=1), pltpu.roll(x, 32, axis=1))


def _proj_odd_kernel(x_ref, mod_ref, g_ref, w_ref, cos_ref, sin_ref, q_ref, k_ref, v_ref, glu_ref):
    m = mod_ref[0]
    h = _norm_mod(x_ref[0], g_ref[...], m[0:1], m[1:2]).astype(BF16)
    u = _dot(h, w_ref[...])
    cos = cos_ref[...]
    sin = sin_ref[...]
    for hh in range(HEADS):
        sl = slice(hh * HEAD_W, (hh + 1) * HEAD_W)
        qh = u[:, sl]
        kh = u[:, D_HEADS + hh * HEAD_W:D_HEADS + (hh + 1) * HEAD_W]
        q_ref[0, :, sl] = ((qh * cos + _swap_halves(qh) * sin) * Q_SCALE_LOG2).astype(BF16)
        k_ref[0, :, sl] = (kh * cos + _swap_halves(kh) * sin).astype(BF16)
    v_ref[0] = u[:, 2 * D_HEADS:3 * D_HEADS].astype(BF16)
    a = u[:, 3 * D_HEADS:3 * D_HEADS + D_SIDE]
    gte = u[:, 3 * D_HEADS + D_SIDE:]
    glu_ref[0] = a * _sigmoid(gte)


def _rope_tables(t, ctx_len):
    rows = t // GRID_W
    row = jnp.repeat(jnp.arange(rows, dtype=F32), GRID_W)
    col = jnp.tile(jnp.arange(GRID_W, dtype=F32), rows)
    n_freq = DIFF_DH // 4
    inv = ROPE_BASE ** (-jnp.arange(n_freq, dtype=F32) / n_freq)
    ang = jnp.concatenate([row[:, None] * inv, col[:, None] * inv], axis=-1)
    cos, sin = jnp.cos(ang), jnp.sin(ang)
    cos = jnp.concatenate([cos, cos, cos, cos], axis=-1)
    sin = jnp.concatenate([-sin, sin, -sin, sin], axis=-1)
    cos = jnp.concatenate([cos, jnp.ones((ctx_len, 128), F32)], axis=0)
    sin = jnp.concatenate([sin, jnp.zeros((ctx_len, 128), F32)], axis=0)
    return cos, sin


def _proj_odd(x, mod, g, w_all, cos, sin, nlat):
    b, ttot, d = x.shape
    tm = ROW_TILE
    nt = ttot // tm
    row = lambda w: pl.BlockSpec((1, tm, w), lambda bi, i: (bi, i, 0))
    tab = pl.BlockSpec((tm, 128), lambda bi, i: (i, 0))
    return pl.pallas_call(
        _proj_odd_kernel, name="proj_odd",
        grid=(b, nt),
        in_specs=[row(d), _mod_spec(b, nlat), _const_spec((1, d)), _const_spec(w_all.shape), tab, tab],
        out_specs=[row(D_HEADS), row(D_HEADS), row(D_HEADS), row(D_SIDE)],
        out_shape=[jax.ShapeDtypeStruct((b, ttot, D_HEADS), BF16)] * 3
        + [jax.ShapeDtypeStruct((b, ttot, D_SIDE), F32)],
        compiler_params=_cparams("parallel", "parallel"),
    )(x, mod, g, w_all, cos, sin)


def _attn_kernel(q_ref, k_ref, v_ref, lam_ref, gs_ref, o_ref, acc0, acc1, *, tk, nk, lam_init):
    q = q_ref[0]
    q0 = q[:, :DIFF_DH]
    q1 = q[:, DIFF_DH:]
    tq = q.shape[0]
    one_bf = (lax.broadcasted_iota(jnp.int32, (tk, HEAD_W), 1) == 0).astype(BF16)
    acc0[...] = jnp.zeros_like(acc0)
    acc1[...] = jnp.zeros_like(acc1)

    def online(s, m_old, acc, vext):
        m_new = jnp.maximum(m_old, jnp.max(s, axis=1, keepdims=True))
        p = jnp.exp2(s - m_new).astype(BF16)
        acc[...] = jnp.exp2(m_old - m_new) * acc[...] + _dot(p, vext)
        return m_new

    m0 = m1 = jnp.full((tq, 1), NEG, F32)
    for j in range(nk):
        kk = k_ref[0, j * tk:(j + 1) * tk, :]
        vext = jnp.concatenate([v_ref[0, j * tk:(j + 1) * tk, :], one_bf], axis=1)
        m0 = online(_dot_nt(q0, kk[:, :DIFF_DH]), m0, acc0, vext)
        m1 = online(_dot_nt(q1, kk[:, DIFF_DH:]), m1, acc1, vext)
    lp = lam_ref[...]
    lam = (jnp.exp(jnp.sum(lp[0:1] * lp[1:2], axis=1, keepdims=True))
           - jnp.exp(jnp.sum(lp[2:3] * lp[3:4], axis=1, keepdims=True)) + lam_init)
    a0 = acc0[...]
    a1 = acc1[...]
    o = a0[:, :HEAD_W] / a0[:, HEAD_W:HEAD_W + 1] - lam * (a1[:, :HEAD_W] / a1[:, HEAD_W:HEAD_W + 1])
    o_ref[0] = (_rms(o, gs_ref[...]) * (1.0 - lam_init)).astype(BF16)


def _diff_attention(q, k, v, lam_p, g_sub, t, lam_init):
    b, ttot, _ = q.shape
    tq = ATTN_TQ
    tk = max(n for n in range(128, ATTN_TK + 1, 128) if ttot % n == 0)
    return pl.pallas_call(
        functools.partial(_attn_kernel, tk=tk, nk=ttot // tk, lam_init=lam_init), name="diff_attn",
        grid=(b, HEADS, t // tq),
        in_specs=[pl.BlockSpec((1, tq, HEAD_W), lambda bi, h, i: (bi, i, h)),
                  pl.BlockSpec((1, ttot, HEAD_W), lambda bi, h, i: (bi, 0, h)),
                  pl.BlockSpec((1, ttot, HEAD_W), lambda bi, h, i: (bi, 0, h)),
                  _const_spec(lam_p.shape), _const_spec((1, HEAD_W))],
        out_specs=pl.BlockSpec((1, tq, HEAD_W), lambda bi, h, i: (bi, i, h)),
        out_shape=jax.ShapeDtypeStruct((b, t, D_HEADS), BF16),
        scratch_shapes=[pltpu.VMEM((tq, 2 * HEAD_W), F32)] * 2,
        compiler_params=_cparams("parallel", "parallel", "parallel"),
    )(q, k, v, lam_p, g_sub)


def _odd_finish_kernel(o_ref, glu_ref, prev_ref, next_ref, wdw_ref, gln_ref, bln_ref, x_ref, mod_ref,
                       w_ref, out_ref, xs, *, tm, nlat):
    i = pl.program_id(1)
    hal = CONV_HALO
    xs[0:hal, :] = jnp.where(i == 0, 0.0, prev_ref[0])
    xs[hal:hal + tm, :] = glu_ref[0]
    xs[hal + tm:, :] = jnp.where(i == nlat - 1, 0.0, next_ref[0])
    wdw = wdw_ref[...]
    z = jnp.zeros((tm, D_SIDE), F32)
    off = hal - (CONV_W - 1) // 2
    for r in range(SUBLANES):
        u = None
        for a in range((off + CONV_W - 1) // SUBLANES + 1):
            kk = SUBLANES * a + r - off
            if 0 <= kk < CONV_W:
                term = xs[SUBLANES * a:SUBLANES * a + tm + SUBLANES, :] * wdw[kk:kk + 1]
                u = term if u is None else u + term
        z = z + u[r:r + tm]
    mu = jnp.mean(z, axis=-1, keepdims=True)
    zc = z - mu
    var = jnp.mean(zc * zc, axis=-1, keepdims=True)
    z = zc * lax.rsqrt(var + EPS) * gln_ref[...] + bln_ref[...]
    z = z * _sigmoid(z)
    cat = jnp.concatenate([o_ref[0], z.astype(BF16)], axis=1)
    out_ref[0] = x_ref[0] + mod_ref[0][2:3] * _dot(cat, w_ref[...])


def _odd_finish(o, glu, w_dw, g_ln, b_ln, x, mod, w_out, t):
    b, _, d = x.shape
    tm = ROW_TILE
    nlat = t // tm
    r = tm // CONV_HALO
    row = lambda w: pl.BlockSpec((1, tm, w), lambda bi, i: (bi, i, 0))
    return pl.pallas_call(
        functools.partial(_odd_finish_kernel, tm=tm, nlat=nlat), name="odd_finish",
        grid=(b, nlat),
        in_specs=[row(D_HEADS), row(D_SIDE),
                  pl.BlockSpec((1, CONV_HALO, D_SIDE), lambda bi, i: (bi, jnp.maximum(i * r - 1, 0), 0)),
                  pl.BlockSpec((1, CONV_HALO, D_SIDE), lambda bi, i: (bi, (i + 1) * r, 0)),
                  _const_spec((CONV_W, D_SIDE)), _const_spec((1, D_SIDE)), _const_spec((1, D_SIDE)),
                  row(d), _mod_spec(b, nlat), _const_spec((d, d))],
        out_specs=row(d),
        out_shape=jax.ShapeDtypeStruct((b, t, d), F32),
        scratch_shapes=[pltpu.VMEM((tm + 2 * CONV_HALO, D_SIDE), F32)],
        compiler_params=_cparams("parallel", "parallel"),
    )(o, glu, glu, glu, w_dw, g_ln, b_ln, x, mod, w_out)


def kernel(x, c, ctx, c_ctx, w_mod, b_mod, g_norm, w_in_even, b_gate, w_qk_conv, g_mlstm_head,
           w_out_even, w_in_odd, lam_p, g_subln, w_dw, g_conv_ln, b_conv_ln, w_out_odd,
           w_ff1, w_ff2, g_final):
    b, t, d = x.shape
    ctx_len = ctx.shape[1]
    assert w_mod.shape[0] == 2 and d == 1024 and ctx_len == ROW_TILE and t % (FFT_N1 * 8) == 0
    nlat = t // ROW_TILE
    mod = _mod_vectors(c, c_ctx, w_mod, b_mod)

    we = w_in_even[0]
    g0 = 4 * D_HEADS
    w_gate = jnp.zeros((d, 256), F32).at[:, 0:12].set(we[:, g0:g0 + 12]).at[:, 128:140].set(we[:, g0 + 12:g0 + 24])
    w_all = jnp.concatenate([we[:, :g0], w_gate, we[:, g0 + 24:]], axis=1).astype(BF16)
    bg = jnp.zeros((1, 256), F32).at[0, 0:12].set(b_gate[0, :12]).at[0, 128:140].set(b_gate[0, 12:])
    q, k, v, o, gates, p = _proj_even(x, ctx, mod[0], g_norm[0, 0][None], w_all, bg, _channel_dft(),
                                      w_qk_conv[0], nlat)
    h2 = _mlstm(q, k, v, gates, t)
    y_lat, y_ctx = _fourier(p, t, ctx_len)
    x1 = _even_finish(h2, o, y_lat, y_ctx, x, ctx, mod[0], g_mlstm_head[0][None],
                      w_out_even[0].astype(BF16), nlat)
    x2 = _mlp(x1, mod[0], g_norm[0, 1][None], w_ff1[0].astype(BF16), w_ff2[0].astype(BF16),
              g_final[None], nlat, t + ctx_len, False)

    lam_init = 0.8 - 0.6 * math.exp(-0.3 * 1)
    cos, sin = _rope_tables(t, ctx_len)
    qa, ka, va, glu = _proj_odd(x2, mod[1], g_norm[1, 0][None], w_in_odd[0].astype(BF16), cos, sin, nlat)
    oa = _diff_attention(qa, ka, va, lam_p[0], g_subln[0][None], t, lam_init)
    x3 = _odd_finish(oa, glu, w_dw[0], g_conv_ln[0][None], b_conv_ln[0][None], x2, mod[1],
                     w_out_odd[0].astype(BF16), t)
    return _mlp(x3, mod[1], g_norm[1, 1][None], w_ff1[1].astype(BF16), w_ff2[1].astype(BF16),
                g_final[None], nlat, t, True)
```

```python
import functools
import math

import numpy as np
import jax
import jax.numpy as jnp
from jax import lax
from jax.experimental import pallas as pl
from jax.experimental.pallas import tpu as pltpu

F32 = jnp.float32
BF16 = jnp.bfloat16

EPS = 1e-6
SUBLANES = 8
HEADS = 6
HEAD_W = 128
D_HEADS = HEADS * HEAD_W
D_SIDE = 256
FNET_GC = 64
QK_W = 2 * D_HEADS
CONV_W = 31
CONV_HALO = 16
GRID_W = 64
ROPE_BASE = 10000.0
DIFF_DH = 64
FFT_N1 = 128
ROW_TILE = 256
LATENT_ROW_TILE = 512
MLSTM_CHUNK = 256
MLSTM_SUM_ROWS = 16
ATTN_TQ = 512
ATTN_TK = 1408
NEG = -1e30
Q_SCALE_LOG2 = (DIFF_DH ** -0.5) * math.log2(math.e)
VMEM_LIMIT = 56 * 1024 * 1024


def _cparams(*sem, flags=None):
    return pltpu.CompilerParams(dimension_semantics=sem, vmem_limit_bytes=VMEM_LIMIT, flags=flags)


def _dot(a, b):
    return jnp.dot(a, b, preferred_element_type=F32)


def _dot_nt(a, b):
    return lax.dot_general(a, b, (((1,), (1,)), ((), ())), preferred_element_type=F32)


def _dot_tn(a, b):
    return lax.dot_general(a, b, (((0,), (0,)), ((), ())), preferred_element_type=F32)


def _sigmoid(x):
    return 1.0 / (1.0 + jnp.exp(-x))


def _rms(x, g):
    return x * lax.rsqrt(jnp.mean(x * x, axis=-1, keepdims=True) + EPS) * g


def _norm_mod(x, g, shift, scale):
    return _rms(x, g) * (1.0 + scale) + shift


def _mod_kernel(s_ref, w_ref, b_ref, o_ref):
    s = s_ref[...]
    s = s * _sigmoid(s)
    o_ref[0] = _dot(s.astype(BF16), w_ref[0].astype(BF16)) + b_ref[0]


def _mod_vectors(c, c_ctx, w_mod, b_mod):
    depth, d, n = w_mod.shape
    b = c.shape[0]
    s = jnp.zeros((8, d), F32).at[:b].set(c).at[b].set(c_ctx)
    tn = 1536
    out = pl.pallas_call(
        _mod_kernel, name="adaln_mod",
        grid=(depth, n // tn),
        in_specs=[pl.BlockSpec((8, d), lambda l, j: (0, 0)),
                  pl.BlockSpec((1, d, tn), lambda l, j: (l, 0, j)),
                  pl.BlockSpec((1, 1, tn), lambda l, j: (l, 0, j))],
        out_specs=pl.BlockSpec((1, 8, tn), lambda l, j: (l, 0, j)),
        out_shape=jax.ShapeDtypeStruct((depth, 8, n), F32),
        compiler_params=_cparams("parallel", "parallel"),
    )(s, w_mod, b_mod.reshape(depth, 1, n))
    return out.reshape(depth, 8, 6, d)


def _mod_spec(b, nlat):
    return pl.BlockSpec((1, 6, 1024), lambda bi, i: (jnp.where(i >= nlat, b, bi), 0, 0))


def _const_spec(shape):
    nd = len(shape)
    return pl.BlockSpec(shape, lambda *_: (0,) * nd)


def _tile_rows(x_ref, ctx_ref, nlat):
    return jnp.where(pl.program_id(1) >= nlat, ctx_ref[0], x_ref[0])


def _lat_ctx_specs(d, tm, nlat):
    return [pl.BlockSpec((1, tm, d), lambda bi, i: (bi, jnp.minimum(i, nlat - 1), 0)),
            pl.BlockSpec((1, tm, d), lambda bi, i: (bi, jnp.maximum(i - nlat, 0), 0))]


def _proj_even_kernel(x_ref, ctx_ref, xp_ref, xn_ref, mod_ref, g_ref, w_ref, bg_ref, dft_ref, wc_ref,
                      q_ref, k_ref, vt_ref, o_ref, gt_ref, p_ref, *, tm, nlat):
    i = pl.program_id(1)
    m = mod_ref[0]
    g = g_ref[...]
    h = _norm_mod(_tile_rows(x_ref, ctx_ref, nlat), g, m[0:1], m[1:2]).astype(BF16)
    xq = _dot(h, w_ref[:, :QK_W])
    u = _dot(h, w_ref[:, QK_W:])
    halo = jnp.concatenate([xp_ref[0], xn_ref[0]], axis=0)
    uh = _dot(_norm_mod(halo, g, m[0:1], m[1:2]).astype(BF16), w_ref[:, :QK_W])
    prow = jnp.where(jnp.logical_or(i == 0, i >= nlat), 0.0, uh[SUBLANES - 1:SUBLANES])
    nrow = jnp.where(i >= nlat - 1, 0.0, uh[SUBLANES:SUBLANES + 1])
    rid = lax.broadcasted_iota(jnp.int32, xq.shape, 0)
    xm = jnp.where(rid == 0, prow, pltpu.roll(xq, 1, axis=0))
    xp = jnp.where(rid == tm - 1, nrow, pltpu.roll(xq, tm - 1, axis=0))
    wc = wc_ref[...]
    y = xm * wc[0:1] + xq * wc[1:2] + xp * wc[2:3]
    y = y * _sigmoid(y)
    q_ref[0] = (y[:, :D_HEADS] * (HEAD_W ** -0.5)).astype(BF16)
    k_ref[0] = y[:, D_HEADS:].astype(BF16)
    vt_ref[0] = u[:, :D_HEADS].T.astype(BF16)
    o_ref[0] = u[:, D_HEADS:2 * D_HEADS]
    c0 = 2 * D_HEADS
    gt_ref[0] = u[:, c0:c0 + 256] + bg_ref[...]
    f = u[:, c0 + 256:c0 + 512].astype(BF16)
    p_ref[0] = _dot(f, dft_ref[...].astype(BF16))


def _proj_even(x, ctx, mod, g, w_all, bg, dftc, w_conv, nlat):
    b, t, d = x.shape
    ttot = t + ctx.shape[1]
    tm = ROW_TILE
    nt = ttot // tm
    r8 = tm // SUBLANES
    last8 = t // SUBLANES - 1
    row = lambda w: pl.BlockSpec((1, tm, w), lambda bi, i: (bi, i, 0))
    halo = lambda f: pl.BlockSpec((1, SUBLANES, d), lambda bi, i: (bi, jnp.clip(f(i), 0, last8), 0))
    return pl.pallas_call(
        functools.partial(_proj_even_kernel, tm=tm, nlat=nlat), name="proj_even",
        grid=(b, nt),
        in_specs=_lat_ctx_specs(d, tm, nlat)
        + [halo(lambda i: i * r8 - 1), halo(lambda i: (i + 1) * r8),
           _mod_spec(b, nlat), _const_spec((1, d)), _const_spec(w_all.shape), _const_spec((1, 256)),
           _const_spec(dftc.shape), _const_spec((3, QK_W))],
        out_specs=[row(D_HEADS), row(D_HEADS), pl.BlockSpec((1, D_HEADS, tm), lambda bi, i: (bi, 0, i)),
                   row(D_HEADS), row(256), row(512)],
        out_shape=[jax.ShapeDtypeStruct((b, ttot, D_HEADS), BF16),
                   jax.ShapeDtypeStruct((b, ttot, D_HEADS), BF16),
                   jax.ShapeDtypeStruct((b, D_HEADS, ttot), BF16),
                   jax.ShapeDtypeStruct((b, ttot, D_HEADS), F32),
                   jax.ShapeDtypeStruct((b, ttot, 256), F32),
                   jax.ShapeDtypeStruct((b, ttot, 512), F32)],
        compiler_params=_cparams("parallel", "parallel"),
    )(x, ctx, x, x, mod, g, w_all, bg, dftc, w_conv)


def _mlstm_kernel(q_ref, k_ref, vt_ref, g_ref, h_ref, c_scr, m_scr, *, chunk):
    d = pl.program_id(1)
    c = pl.program_id(2)

    @pl.when(c == 0)
    def _():
        c_scr[...] = jnp.zeros_like(c_scr)
        m_scr[...] = jnp.zeros_like(m_scr)

    gates = g_ref[0]
    logf = jnp.minimum(gates, 0.0) - jnp.log(1.0 + jnp.exp(-jnp.abs(gates)))
    row = lax.broadcasted_iota(jnp.int32, (chunk, chunk), 0)
    col = lax.broadcasted_iota(jnp.int32, (chunk, chunk), 1)
    sgn = 1 - 2 * d
    csum = ((col - row) * sgn <= 0).astype(BF16)
    mask_t = (row - col) * sgn <= 0
    f_hi = logf.astype(BF16)
    f_mid = (logf - f_hi.astype(F32)).astype(BF16)
    f_lo = (logf - f_hi.astype(F32) - f_mid.astype(F32)).astype(BF16)
    a_all = _dot(csum, f_hi) + _dot(csum, f_mid) + _dot(csum, f_lo)
    r_all = gates - pltpu.roll(a_all, 128 - HEADS, axis=1)
    lane = lax.broadcasted_iota(jnp.int32, (chunk, 128), 1)
    rows = jnp.where(lane < HEADS, r_all, a_all).T
    ones = jnp.ones((MLSTM_SUM_ROWS, chunk), BF16)

    for h in range(HEADS):
        sl = slice(h * HEAD_W, (h + 1) * HEAD_W)
        qh = q_ref[0, :, sl]
        kh = k_ref[0, :, sl]
        vext = jnp.concatenate([vt_ref[0, sl, :], ones], axis=0)
        m_prev = m_scr[h:h + 1, 0:1]
        r_row = rows[h:h + 1, :]
        a_row = rows[HEADS + h:HEADS + h + 1, :]
        rm = jnp.where(mask_t, r_all[:, h:h + 1], NEG)
        mx = jnp.maximum(jnp.max(rm, axis=0, keepdims=True), m_prev)
        st = _dot_nt(kh, qh) * jnp.exp(rm - mx)
        ce = c_scr[h]
        nd = _dot(vext, st.astype(BF16)) + jnp.exp(m_prev - mx) * _dot_nt(ce.astype(BF16), qh)
        den = jnp.maximum(jnp.abs(nd[HEAD_W:HEAD_W + 1]), jnp.exp(-(a_row + mx)))
        h_ref[0, 0, sl, :] = (nd[:HEAD_W] / den).astype(BF16)
        mx_last = jnp.max(mx, axis=1, keepdims=True)
        a_last = jnp.min(a_row, axis=1, keepdims=True)
        wv = (vext.astype(F32) * jnp.exp(r_row - mx_last)).astype(BF16)
        c_scr[h] = jnp.exp(m_prev - mx_last) * ce + _dot(wv, kh)
        m_scr[h:h + 1, :] = jnp.broadcast_to(a_last + mx_last, (1, 128))


def _mlstm(q, k, vt, gates, nlat_rows):
    b, ttot, _ = q.shape
    L = MLSTM_CHUNK
    nc = ttot // L
    ncl = nlat_rows // L
    ncc = nc - ncl

    def blk(d, c):
        fwd = jnp.where(c < ncc, ncl + c, c - ncc)
        bwd = jnp.where(c < ncc, nc - 1 - c, ncl - 1 - (c - ncc))
        return jnp.where(d == 0, fwd, bwd)

    head_spec = pl.BlockSpec((1, L, D_HEADS), lambda bi, d, c: (bi, blk(d, c), 0))
    return pl.pallas_call(
        functools.partial(_mlstm_kernel, chunk=L), name="mlstm_scan",
        grid=(b, 2, nc),
        in_specs=[head_spec, head_spec,
                  pl.BlockSpec((1, D_HEADS, L), lambda bi, d, c: (bi, 0, blk(d, c))),
                  pl.BlockSpec((1, L, 128), lambda bi, d, c: (bi, blk(d, c), d))],
        out_specs=pl.BlockSpec((1, 1, D_HEADS, L), lambda bi, d, c: (d, bi, 0, blk(d, c))),
        out_shape=jax.ShapeDtypeStruct((2, b, D_HEADS, ttot), BF16),
        scratch_shapes=[pltpu.VMEM((HEADS, HEAD_W + MLSTM_SUM_ROWS, HEAD_W), F32),
                        pltpu.VMEM((8, 128), F32)],
        compiler_params=_cparams("parallel", "parallel", "arbitrary"),
    )(q, k, vt, gates)


def _fft1_kernel(p_ref, g_ref, o_ref):
    for r in range(SUBLANES):
        x = p_ref[0, :, 0, r, :].astype(BF16)
        bb = _dot(g_ref[r].astype(BF16), jnp.concatenate([x[:, :256], x[:, 256:]], axis=0))
        o_ref[0, 0, :, r, :] = bb[:FFT_N1]
        o_ref[0, 1, :, r, :] = bb[FFT_N1:]


def _fft2_kernel(b_ref, t_ref, o_ref, *, kb, scale):
    tc = t_ref[0].astype(BF16)
    ts = t_ref[1].astype(BF16)
    for j in range(kb):
        y = _dot(tc, b_ref[0, 0, j].astype(BF16)) + _dot(ts, b_ref[0, 1, j].astype(BF16))
        o_ref[0, :, j, :] = y * scale


def _dft_ctx_kernel(p_ref, t_ref, o_ref, *, scale):
    p = p_ref[0].astype(BF16)
    st = jnp.concatenate([p[:, :256], p[:, 256:]], axis=0)
    o_ref[0] = _dot(t_ref[...].astype(BF16), st) * scale


def _fft_tables(t):
    n1, n2 = FFT_N1, t // FFT_N1
    k1 = np.arange(n1, dtype=np.int64)[None, :, None]
    nn = (n2 * np.arange(n1, dtype=np.int64)[None, None, :] + np.arange(n2, dtype=np.int64)[:, None, None])
    ang = 2.0 * np.pi * ((k1 * nn) % t).astype(np.float64) / t
    gr, gi = np.cos(ang), -np.sin(ang)
    g = np.concatenate([np.concatenate([gr, -gi], axis=2), np.concatenate([gi, gr], axis=2)], axis=1)
    a2 = 2.0 * np.pi * ((np.arange(n2)[:, None] * np.arange(n2)[None, :]) % n2) / n2
    t2 = np.stack([np.cos(a2), np.sin(a2)])
    return jnp.asarray(g, F32), jnp.asarray(t2, F32)


def _dft_matrix_cs(n):
    a = 2.0 * np.pi * ((np.arange(n)[:, None] * np.arange(n)[None, :]) % n) / n
    return np.cos(a), np.sin(a)


def _channel_dft():
    c, s = _dft_matrix_cs(FNET_GC)
    eye = np.eye(D_SIDE // FNET_GC)
    return jnp.asarray(np.concatenate([np.kron(eye, c), -np.kron(eye, s)], axis=1), F32)


def _fourier(p, t, ctx_len):
    b = p.shape[0]
    n1, n2 = FFT_N1, t // FFT_N1
    g, t2 = _fft_tables(t)
    kb = 16
    ttot = p.shape[1]
    p5 = p.reshape(b, ttot // n2, n2 // SUBLANES, SUBLANES, 512)
    st1 = pl.pallas_call(
        _fft1_kernel, name="fft_stage1",
        grid=(b, n2 // SUBLANES),
        in_specs=[pl.BlockSpec((1, n1, 1, SUBLANES, 512), lambda bi, j: (bi, 0, j, 0, 0)),
                  pl.BlockSpec((SUBLANES, 256, 256), lambda bi, j: (j, 0, 0))],
        out_specs=pl.BlockSpec((1, 2, n1, SUBLANES, 256), lambda bi, j: (bi, 0, 0, j, 0)),
        out_shape=jax.ShapeDtypeStruct((b, 2, n1, n2, 256), F32),
        compiler_params=_cparams("parallel", "parallel"),
    )(p5, g)
    y = pl.pallas_call(
        functools.partial(_fft2_kernel, kb=kb, scale=1.0 / math.sqrt(t * FNET_GC)), name="fft_stage2",
        grid=(b, n1 // kb),
        in_specs=[pl.BlockSpec((1, 2, kb, n2, 256), lambda bi, j: (bi, 0, j, 0, 0)),
                  _const_spec((2, n2, n2))],
        out_specs=pl.BlockSpec((1, n2, kb, 256), lambda bi, j: (bi, 0, j, 0)),
        out_shape=jax.ShapeDtypeStruct((b, n2, n1, 256), F32),
        compiler_params=_cparams("parallel", "parallel"),
    )(st1, t2)
    y_lat = y.reshape(b, t, 256)
    cc, sc = _dft_matrix_cs(ctx_len)
    tc = jnp.asarray(np.concatenate([cc, sc], axis=1), F32)
    y_ctx = pl.pallas_call(
        functools.partial(_dft_ctx_kernel, scale=1.0 / math.sqrt(ctx_len * FNET_GC)), name="dft_ctx",
        grid=(b,),
        in_specs=[pl.BlockSpec((1, ctx_len, 512), lambda bi: (bi, t // ctx_len, 0)),
                  _const_spec((ctx_len, 2 * ctx_len))],
        out_specs=pl.BlockSpec((1, ctx_len, 256), lambda bi: (bi, 0, 0)),
        out_shape=jax.ShapeDtypeStruct((b, ctx_len, 256), F32),
        compiler_params=_cparams("parallel"),
    )(p, tc)
    return y_lat, y_ctx


def _even_finish_kernel(hf_ref, hb_ref, o_ref, yl_ref, yc_ref, x_ref, ctx_ref, mod_ref, gh_ref, w_ref,
                        out_ref, *, nlat):
    i = pl.program_id(1)
    hs = (hf_ref[0, 0].astype(F32) + hb_ref[0, 0].astype(F32)).T
    gate = _sigmoid(o_ref[0])
    gh = gh_ref[...]
    parts = []
    for h in range(HEADS):
        sl = slice(h * HEAD_W, (h + 1) * HEAD_W)
        parts.append((_rms(hs[:, sl], gh[:, sl]) * gate[:, sl]).astype(BF16))
    y = jnp.where(i >= nlat, yc_ref[0], yl_ref[0])
    parts.append(y.astype(BF16))
    out = _dot(jnp.concatenate(parts, axis=1), w_ref[...])
    out_ref[0] = _tile_rows(x_ref, ctx_ref, nlat) + mod_ref[0][2:3] * out


def _even_finish(h2, o, y_lat, y_ctx, x, ctx, mod, g_head, w_out, nlat):
    b, t, d = x.shape
    ttot = t + ctx.shape[1]
    tm = ROW_TILE
    nt = ttot // tm
    row = lambda w: pl.BlockSpec((1, tm, w), lambda bi, i: (bi, i, 0))
    return pl.pallas_call(
        functools.partial(_even_finish_kernel, nlat=nlat), name="even_finish",
        grid=(b, nt),
        in_specs=[pl.BlockSpec((1, 1, D_HEADS, tm), lambda bi, i: (0, bi, 0, i)),
                  pl.BlockSpec((1, 1, D_HEADS, tm), lambda bi, i: (1, bi, 0, i)),
                  row(D_HEADS),
                  pl.BlockSpec((1, tm, 256), lambda bi, i: (bi, jnp.minimum(i, nlat - 1), 0)),
                  pl.BlockSpec((1, tm, 256), lambda bi, i: (bi, jnp.maximum(i - nlat, 0), 0))]
        + _lat_ctx_specs(d, tm, nlat)
        + [_mod_spec(b, nlat), _const_spec((1, D_HEADS)), _const_spec((d, d))],
        out_specs=row(d),
        out_shape=jax.ShapeDtypeStruct((b, ttot, d), F32),
        compiler_params=_cparams("parallel", "parallel"),
    )(h2, h2, o, y_lat, y_ctx, x, ctx, mod, g_head, w_out)


def _mlp_kernel(x_ref, mod_ref, g_ref, w1_ref, w2_ref, gf_ref, out_ref, *, final):
    x = x_ref[0]
    m = mod_ref[0]
    h = _norm_mod(x, g_ref[...], m[3:4], m[4:5]).astype(BF16)
    a = jnp.maximum(_dot(h, w1_ref[...]), 0.0)
    y = _dot((a * a).astype(BF16), w2_ref[...])
    xn = x + m[5:6] * y
    if final:
        xn = _rms(xn, gf_ref[...])
    out_ref[0] = xn


def _mlp(x, mod, g, w1, w2, g_final, nlat, rows_out, final):
    b, _, d = x.shape
    tm = LATENT_ROW_TILE if final else ROW_TILE
    nt = rows_out // tm
    row = pl.BlockSpec((1, tm, d), lambda bi, i: (bi, i, 0))
    single = dict(pipeline_mode=pl.Buffered(1))
    return pl.pallas_call(
        functools.partial(_mlp_kernel, final=final), name="mlp_final" if final else "mlp",
        grid=(b, nt),
        in_specs=[row, _mod_spec(b, nlat), _const_spec((1, d)),
                  pl.BlockSpec(w1.shape, lambda bi, i: (0, 0), **single),
                  pl.BlockSpec(w2.shape, lambda bi, i: (0, 0), **single),
                  _const_spec((1, d))],
        out_specs=row,
        out_shape=jax.ShapeDtypeStruct((b, rows_out, d), F32),
        compiler_params=_cparams("parallel", "parallel"),
    )(x, mod, g, w1, w2, g_final)


def _swap_halves(x):
    lane = lax.broadcasted_iota(jnp.int32, x.shape, 1)
    return jnp.where(lane % 64 < 32, pltpu.roll(x, 96, axis=1), pltpu.roll(x, 32, axis=1))


def _proj_odd_kernel(x_ref, mod_ref, g_ref, w_ref, cos_ref, sin_ref, q_ref, k_ref, v_ref, glu_ref):
    m = mod_ref[0]
    h = _norm_mod(x_ref[0], g_ref[...], m[0:1], m[1:2]).astype(BF16)
    u = _dot(h, w_ref[...])
    cos = cos_ref[...]
    sin = sin_ref[...]
    for hh in range(HEADS):
        sl = slice(hh * HEAD_W, (hh + 1) * HEAD_W)
        qh = u[:, sl]
        kh = u[:, D_HEADS + hh * HEAD_W:D_HEADS + (hh + 1) * HEAD_W]
        q_ref[0, :, sl] = ((qh * cos + _swap_halves(qh) * sin) * Q_SCALE_LOG2).astype(BF16)
        k_ref[0, :, sl] = (kh * cos + _swap_halves(kh) * sin).astype(BF16)
    v_ref[0] = u[:, 2 * D_HEADS:3 * D_HEADS].astype(BF16)
    a = u[:, 3 * D_HEADS:3 * D_HEADS + D_SIDE]
    gte = u[:, 3 * D_HEADS + D_SIDE:]
    glu_ref[0] = a * _sigmoid(gte)


def _rope_tables(t, ctx_len):
    rows = t // GRID_W
    row = np.repeat(np.arange(rows, dtype=np.float64), GRID_W)
    col = np.tile(np.arange(GRID_W, dtype=np.float64), rows)
    n_freq = DIFF_DH // 4
    inv = ROPE_BASE ** (-np.arange(n_freq, dtype=np.float64) / n_freq)
    ang = np.concatenate([row[:, None] * inv, col[:, None] * inv], axis=-1)
    cos, sin = np.cos(ang), np.sin(ang)
    cos = np.concatenate([cos, cos, cos, cos], axis=-1)
    sin = np.concatenate([-sin, sin, -sin, sin], axis=-1)
    cos = np.concatenate([cos, np.ones((ctx_len, 128))], axis=0)
    sin = np.concatenate([sin, np.zeros((ctx_len, 128))], axis=0)
    return jnp.asarray(cos, F32), jnp.asarray(sin, F32)


def _proj_odd(x, mod, g, w_all, cos, sin, nlat):
    b, ttot, d = x.shape
    tm = ROW_TILE
    nt = ttot // tm
    row = lambda w: pl.BlockSpec((1, tm, w), lambda bi, i: (bi, i, 0))
    tab = pl.BlockSpec((tm, 128), lambda bi, i: (i, 0))
    return pl.pallas_call(
        _proj_odd_kernel, name="proj_odd",
        grid=(b, nt),
        in_specs=[row(d), _mod_spec(b, nlat), _const_spec((1, d)), _const_spec(w_all.shape), tab, tab],
        out_specs=[row(D_HEADS), row(D_HEADS), row(D_HEADS), row(D_SIDE)],
        out_shape=[jax.ShapeDtypeStruct((b, ttot, D_HEADS), BF16)] * 3
        + [jax.ShapeDtypeStruct((b, ttot, D_SIDE), F32)],
        compiler_params=_cparams("parallel", "parallel"),
    )(x, mod, g, w_all, cos, sin)


def _attn_kernel(q_ref, k_ref, v_ref, lam_ref, gs_ref, o_ref, acc0, acc1, *, tk, nk, lam_init):
    q = q_ref[0]
    q0 = q[:, :DIFF_DH]
    q1 = q[:, DIFF_DH:]
    tq = q.shape[0]
    one_bf = (lax.broadcasted_iota(jnp.int32, (tk, HEAD_W), 1) == 0).astype(BF16)
    acc0[...] = jnp.zeros_like(acc0)
    acc1[...] = jnp.zeros_like(acc1)

    def online(s, m_old, acc, vext):
        m_new = jnp.maximum(m_old, jnp.max(s, axis=1, keepdims=True))
        p = jnp.exp2(s - m_new).astype(BF16)
        acc[...] = jnp.exp2(m_old - m_new) * acc[...] + _dot(p, vext)
        return m_new

    m0 = m1 = jnp.full((tq, 1), NEG, F32)
    for j in range(nk):
        kk = k_ref[0, j * tk:(j + 1) * tk, :]
        vext = jnp.concatenate([v_ref[0, j * tk:(j + 1) * tk, :], one_bf], axis=1)
        m0 = online(_dot_nt(q0, kk[:, :DIFF_DH]), m0, acc0, vext)
        m1 = online(_dot_nt(q1, kk[:, DIFF_DH:]), m1, acc1, vext)
    lp = lam_ref[...]
    lam = (jnp.exp(jnp.sum(lp[0:1] * lp[1:2], axis=1, keepdims=True))
           - jnp.exp(jnp.sum(lp[2:3] * lp[3:4], axis=1, keepdims=True)) + lam_init)
    a0 = acc0[...]
    a1 = acc1[...]
    o = a0[:, :HEAD_W] / a0[:, HEAD_W:HEAD_W + 1] - lam * (a1[:, :HEAD_W] / a1[:, HEAD_W:HEAD_W + 1])
    o_ref[0] = (_rms(o, gs_ref[...]) * (1.0 - lam_init)).astype(BF16)


def _diff_attention(q, k, v, lam_p, g_sub, t, lam_init):
    b, ttot, _ = q.shape
    tq = ATTN_TQ
    tk = max(n for n in range(128, ATTN_TK + 1, 128) if ttot % n == 0)
    return pl.pallas_call(
        functools.partial(_attn_kernel, tk=tk, nk=ttot // tk, lam_init=lam_init), name="diff_attn",
        grid=(b, HEADS, t // tq),
        in_specs=[pl.BlockSpec((1, tq, HEAD_W), lambda bi, h, i: (bi, i, h)),
                  pl.BlockSpec((1, ttot, HEAD_W), lambda bi, h, i: (bi, 0, h)),
                  pl.BlockSpec((1, ttot, HEAD_W), lambda bi, h, i: (bi, 0, h)),
                  _const_spec(lam_p.shape), _const_spec((1, HEAD_W))],
        out_specs=pl.BlockSpec((1, tq, HEAD_W), lambda bi, h, i: (bi, i, h)),
        out_shape=jax.ShapeDtypeStruct((b, t, D_HEADS), BF16),
        scratch_shapes=[pltpu.VMEM((tq, 2 * HEAD_W), F32)] * 2,
        compiler_params=_cparams("parallel", "parallel", "parallel"),
    )(q, k, v, lam_p, g_sub)


def _odd_finish_kernel(o_ref, glu_ref, prev_ref, next_ref, wdw_ref, gln_ref, bln_ref, x_ref, mod_ref,
                       w_ref, out_ref, xs, *, tm, nlat):
    i = pl.program_id(1)
    hal = CONV_HALO
    xs[0:hal, :] = jnp.where(i == 0, 0.0, prev_ref[0])
    xs[hal:hal + tm, :] = glu_ref[0]
    xs[hal + tm:, :] = jnp.where(i == nlat - 1, 0.0, next_ref[0])
    wdw = wdw_ref[...]
    z = jnp.zeros((tm, D_SIDE), F32)
    off = hal - (CONV_W - 1) // 2
    for r in range(SUBLANES):
        u = None
        for a in range((off + CONV_W - 1) // SUBLANES + 1):
            kk = SUBLANES * a + r - off
            if 0 <= kk < CONV_W:
                term = xs[SUBLANES * a:SUBLANES * a + tm + SUBLANES, :] * wdw[kk:kk + 1]
                u = term if u is None else u + term
        z = z + u[r:r + tm]
    mu = jnp.mean(z, axis=-1, keepdims=True)
    zc = z - mu
    var = jnp.mean(zc * zc, axis=-1, keepdims=True)
    z = zc * lax.rsqrt(var + EPS) * gln_ref[...] + bln_ref[...]
    z = z * _sigmoid(z)
    cat = jnp.concatenate([o_ref[0], z.astype(BF16)], axis=1)
    out_ref[0] = x_ref[0] + mod_ref[0][2:3] * _dot(cat, w_ref[...])


def _odd_finish(o, glu, w_dw, g_ln, b_ln, x, mod, w_out, t):
    b, _, d = x.shape
    tm = ROW_TILE
    nlat = t // tm
    r = tm // CONV_HALO
    row = lambda w: pl.BlockSpec((1, tm, w), lambda bi, i: (bi, i, 0))
    return pl.pallas_call(
        functools.partial(_odd_finish_kernel, tm=tm, nlat=nlat), name="odd_finish",
        grid=(b, nlat),
        in_specs=[row(D_HEADS), row(D_SIDE),
                  pl.BlockSpec((1, CONV_HALO, D_SIDE), lambda bi, i: (bi, jnp.maximum(i * r - 1, 0), 0)),
                  pl.BlockSpec((1, CONV_HALO, D_SIDE), lambda bi, i: (bi, (i + 1) * r, 0)),
                  _const_spec((CONV_W, D_SIDE)), _const_spec((1, D_SIDE)), _const_spec((1, D_SIDE)),
                  row(d), _mod_spec(b, nlat), _const_spec((d, d))],
        out_specs=row(d),
        out_shape=jax.ShapeDtypeStruct((b, t, d), F32),
        scratch_shapes=[pltpu.VMEM((tm + 2 * CONV_HALO, D_SIDE), F32)],
        compiler_params=_cparams("parallel", "parallel"),
    )(o, glu, glu, glu, w_dw, g_ln, b_ln, x, mod, w_out)


def kernel(x, c, ctx, c_ctx, w_mod, b_mod, g_norm, w_in_even, b_gate, w_qk_conv, g_mlstm_head,
           w_out_even, w_in_odd, lam_p, g_subln, w_dw, g_conv_ln, b_conv_ln, w_out_odd,
           w_ff1, w_ff2, g_final):
    b, t, d = x.shape
    ctx_len = ctx.shape[1]
    assert w_mod.shape[0] == 2 and d == 1024 and ctx_len == ROW_TILE and t % (FFT_N1 * 8) == 0
    nlat = t // ROW_TILE
    mod = _mod_vectors(c, c_ctx, w_mod, b_mod)

    we = w_in_even[0]
    g0 = 4 * D_HEADS
    w_gate = jnp.zeros((d, 256), F32).at[:, 0:12].set(we[:, g0:g0 + 12]).at[:, 128:140].set(we[:, g0 + 12:g0 + 24])
    w_all = jnp.concatenate([we[:, :g0], w_gate, we[:, g0 + 24:]], axis=1).astype(BF16)
    bg = jnp.zeros((1, 256), F32).at[0, 0:12].set(b_gate[0, :12]).at[0, 128:140].set(b_gate[0, 12:])
    q, k, v, o, gates, p = _proj_even(x, ctx, mod[0], g_norm[0, 0][None], w_all, bg, _channel_dft(),
                                      w_qk_conv[0], nlat)
    h2 = _mlstm(q, k, v, gates, t)
    y_lat, y_ctx = _fourier(p, t, ctx_len)
    x1 = _even_finish(h2, o, y_lat, y_ctx, x, ctx, mod[0], g_mlstm_head[0][None],
                      w_out_even[0].astype(BF16), nlat)
    x2 = _mlp(x1, mod[0], g_norm[0, 1][None], w_ff1[0].astype(BF16), w_ff2[0].astype(BF16),
              g_final[None], nlat, t + ctx_len, False)

    lam_init = 0.8 - 0.6 * math.exp(-0.3 * 1)
    cos, sin = _rope_tables(t, ctx_len)
    qa, ka, va, glu = _proj_odd(x2, mod[1], g_norm[1, 0][None], w_in_odd[0].astype(BF16), cos, sin, nlat)
    oa = _diff_attention(qa, ka, va, lam_p[0], g_subln[0][None], t, lam_init)
    x3 = _odd_finish(oa, glu, w_dw[0], g_conv_ln[0][None], b_conv_ln[0][None], x2, mod[1],
                     w_out_odd[0].astype(BF16), t)
    return _mlp(x3, mod[1], g_norm[1, 1][None], w_ff1[1].astype(BF16), w_ff2[1].astype(BF16),
                g_final[None], nlat, t, True)
```

```python
import functools
import math

import numpy as np
import jax
import jax.numpy as jnp
from jax import lax
from jax.experimental import pallas as pl
from jax.experimental.pallas import tpu as pltpu

F32 = jnp.float32
BF16 = jnp.bfloat16

EPS = 1e-6
SUBLANES = 8
HEADS = 6
HEAD_W = 128
D_HEADS = HEADS * HEAD_W
D_SIDE = 256
FNET_GC = 64
QK_W = 2 * D_HEADS
CONV_W = 31
CONV_HALO = 16
GRID_W = 64
ROPE_BASE = 10000.0
DIFF_DH = 64
FFT_N1 = 128
ROW_TILE = 256
LATENT_ROW_TILE = 512
MLSTM_CHUNK = 256
MLSTM_SUM_ROWS = 16
ATTN_TQ = 512
ATTN_TK = 1408
NEG = -1e30
Q_SCALE_LOG2 = (DIFF_DH ** -0.5) * math.log2(math.e)
VMEM_LIMIT = 56 * 1024 * 1024


def _cparams(*sem, flags=None):
    return pltpu.CompilerParams(dimension_semantics=sem, vmem_limit_bytes=VMEM_LIMIT, flags=flags)


def _dot(a, b):
    return jnp.dot(a, b, preferred_element_type=F32)


def _dot_nt(a, b):
    return lax.dot_general(a, b, (((1,), (1,)), ((), ())), preferred_element_type=F32)


def _dot_tn(a, b):
    return lax.dot_general(a, b, (((0,), (0,)), ((), ())), preferred_element_type=F32)


def _sigmoid(x):
    return 1.0 / (1.0 + jnp.exp(-x))


def _rms(x, g):
    return x * lax.rsqrt(jnp.mean(x * x, axis=-1, keepdims=True) + EPS) * g


def _norm_mod(x, g, shift, scale):
    return _rms(x, g) * (1.0 + scale) + shift


def _mod_kernel(s_ref, w_ref, b_ref, o_ref):
    s = s_ref[...]
    s = s * _sigmoid(s)
    o_ref[0] = _dot(s.astype(BF16), w_ref[0].astype(BF16)) + b_ref[0]


def _mod_vectors(c, c_ctx, w_mod, b_mod):
    depth, d, n = w_mod.shape
    b = c.shape[0]
    s = jnp.zeros((8, d), F32).at[:b].set(c).at[b].set(c_ctx)
    tn = 1536
    out = pl.pallas_call(
        _mod_kernel, name="adaln_mod",
        grid=(depth, n // tn),
        in_specs=[pl.BlockSpec((8, d), lambda l, j: (0, 0)),
                  pl.BlockSpec((1, d, tn), lambda l, j: (l, 0, j)),
                  pl.BlockSpec((1, 1, tn), lambda l, j: (l, 0, j))],
        out_specs=pl.BlockSpec((1, 8, tn), lambda l, j: (l, 0, j)),
        out_shape=jax.ShapeDtypeStruct((depth, 8, n), F32),
        compiler_params=_cparams("parallel", "parallel"),
    )(s, w_mod, b_mod.reshape(depth, 1, n))
    return out.reshape(depth, 8, 6, d)


def _mod_spec(b, nlat):
    return pl.BlockSpec((1, 6, 1024), lambda bi, i: (jnp.where(i >= nlat, b, bi), 0, 0))


def _const_spec(shape):
    nd = len(shape)
    return pl.BlockSpec(shape, lambda *_: (0,) * nd)


def _tile_rows(x_ref, ctx_ref, nlat):
    return jnp.where(pl.program_id(1) >= nlat, ctx_ref[0], x_ref[0])


def _lat_ctx_specs(d, tm, nlat):
    return [pl.BlockSpec((1, tm, d), lambda bi, i: (bi, jnp.minimum(i, nlat - 1), 0)),
            pl.BlockSpec((1, tm, d), lambda bi, i: (bi, jnp.maximum(i - nlat, 0), 0))]


def _proj_even_kernel(x_ref, ctx_ref, xp_ref, xn_ref, mod_ref, g_ref, w_ref, bg_ref, dft_ref, wc_ref,
                      q_ref, k_ref, vt_ref, o_ref, gt_ref, p_ref, *, tm, nlat):
    i = pl.program_id(1)
    m = mod_ref[0]
    g = g_ref[...]
    h = _norm_mod(_tile_rows(x_ref, ctx_ref, nlat), g, m[0:1], m[1:2]).astype(BF16)
    xq = _dot(h, w_ref[:, :QK_W])
    u = _dot(h, w_ref[:, QK_W:])
    halo = jnp.concatenate([xp_ref[0], xn_ref[0]], axis=0)
    uh = _dot(_norm_mod(halo, g, m[0:1], m[1:2]).astype(BF16), w_ref[:, :QK_W])
    prow = jnp.where(jnp.logical_or(i == 0, i >= nlat), 0.0, uh[SUBLANES - 1:SUBLANES])
    nrow = jnp.where(i >= nlat - 1, 0.0, uh[SUBLANES:SUBLANES + 1])
    rid = lax.broadcasted_iota(jnp.int32, xq.shape, 0)
    xm = jnp.where(rid == 0, prow, pltpu.roll(xq, 1, axis=0))
    xp = jnp.where(rid == tm - 1, nrow, pltpu.roll(xq, tm - 1, axis=0))
    wc = wc_ref[...]
    y = xm * wc[0:1] + xq * wc[1:2] + xp * wc[2:3]
    y = y * _sigmoid(y)
    q_ref[0] = (y[:, :D_HEADS] * (HEAD_W ** -0.5)).astype(BF16)
    k_ref[0] = y[:, D_HEADS:].astype(BF16)
    vt_ref[0] = u[:, :D_HEADS].T.astype(BF16)
    o_ref[0] = u[:, D_HEADS:2 * D_HEADS]
    c0 = 2 * D_HEADS
    gt_ref[0] = u[:, c0:c0 + 256] + bg_ref[...]
    f = u[:, c0 + 256:c0 + 512].astype(BF16)
    p_ref[0] = _dot(f, dft_ref[...].astype(BF16))


def _proj_even(x, ctx, mod, g, w_all, bg, dftc, w_conv, nlat):
    b, t, d = x.shape
    ttot = t + ctx.shape[1]
    tm = ROW_TILE
    nt = ttot // tm
    r8 = tm // SUBLANES
    last8 = t // SUBLANES - 1
    row = lambda w: pl.BlockSpec((1, tm, w), lambda bi, i: (bi, i, 0))
    halo = lambda f: pl.BlockSpec((1, SUBLANES, d), lambda bi, i: (bi, jnp.clip(f(i), 0, last8), 0))
    return pl.pallas_call(
        functools.partial(_proj_even_kernel, tm=tm, nlat=nlat), name="proj_even",
        grid=(b, nt),
        in_specs=_lat_ctx_specs(d, tm, nlat)
        + [halo(lambda i: i * r8 - 1), halo(lambda i: (i + 1) * r8),
           _mod_spec(b, nlat), _const_spec((1, d)), _const_spec(w_all.shape), _const_spec((1, 256)),
           _const_spec(dftc.shape), _const_spec((3, QK_W))],
        out_specs=[row(D_HEADS), row(D_HEADS), pl.BlockSpec((1, D_HEADS, tm), lambda bi, i: (bi, 0, i)),
                   row(D_HEADS), row(256), row(512)],
        out_shape=[jax.ShapeDtypeStruct((b, ttot, D_HEADS), BF16),
                   jax.ShapeDtypeStruct((b, ttot, D_HEADS), BF16),
                   jax.ShapeDtypeStruct((b, D_HEADS, ttot), BF16),
                   jax.ShapeDtypeStruct((b, ttot, D_HEADS), F32),
                   jax.ShapeDtypeStruct((b, ttot, 256), F32),
                   jax.ShapeDtypeStruct((b, ttot, 512), F32)],
        compiler_params=_cparams("parallel", "parallel"),
    )(x, ctx, x, x, mod, g, w_all, bg, dftc, w_conv)


def _mlstm_kernel(q_ref, k_ref, vt_ref, g_ref, h_ref, c_scr, m_scr, *, chunk):
    d = pl.program_id(1)
    c = pl.program_id(2)

    @pl.when(c == 0)
    def _():
        c_scr[...] = jnp.zeros_like(c_scr)
        m_scr[...] = jnp.zeros_like(m_scr)

    gates = g_ref[0]
    logf = jnp.minimum(gates, 0.0) - jnp.log(1.0 + jnp.exp(-jnp.abs(gates)))
    row = lax.broadcasted_iota(jnp.int32, (chunk, chunk), 0)
    col = lax.broadcasted_iota(jnp.int32, (chunk, chunk), 1)
    sgn = 1 - 2 * d
    csum = ((col - row) * sgn <= 0).astype(BF16)
    mask_t = (row - col) * sgn <= 0
    f_hi = logf.astype(BF16)
    f_mid = (logf - f_hi.astype(F32)).astype(BF16)
    f_lo = (logf - f_hi.astype(F32) - f_mid.astype(F32)).astype(BF16)
    a_all = _dot(csum, f_hi) + _dot(csum, f_mid) + _dot(csum, f_lo)
    r_all = gates - pltpu.roll(a_all, 128 - HEADS, axis=1)
    lane = lax.broadcasted_iota(jnp.int32, (chunk, 128), 1)
    rows = jnp.where(lane < HEADS, r_all, a_all).T
    ones = jnp.ones((MLSTM_SUM_ROWS, chunk), BF16)

    for h in range(HEADS):
        sl = slice(h * HEAD_W, (h + 1) * HEAD_W)
        qh = q_ref[0, :, sl]
        kh = k_ref[0, :, sl]
        vext = jnp.concatenate([vt_ref[0, sl, :], ones], axis=0)
        m_prev = m_scr[h:h + 1, 0:1]
        r_row = rows[h:h + 1, :]
        a_row = rows[HEADS + h:HEADS + h + 1, :]
        rm = jnp.where(mask_t, r_all[:, h:h + 1], NEG)
        mx = jnp.maximum(jnp.max(rm, axis=0, keepdims=True), m_prev)
        st = _dot_nt(kh, qh) * jnp.exp(rm - mx)
        ce = c_scr[h]
        nd = _dot(vext, st.astype(BF16)) + jnp.exp(m_prev - mx) * _dot_nt(ce.astype(BF16), qh)
        den = jnp.maximum(jnp.abs(nd[HEAD_W:HEAD_W + 1]), jnp.exp(-(a_row + mx)))
        h_ref[0, 0, sl, :] = (nd[:HEAD_W] / den).astype(BF16)
        mx_last = jnp.max(mx, axis=1, keepdims=True)
        a_last = jnp.min(a_row, axis=1, keepdims=True)
        wv = (vext.astype(F32) * jnp.exp(r_row - mx_last)).astype(BF16)
        c_scr[h] = jnp.exp(m_prev - mx_last) * ce + _dot(wv, kh)
        m_scr[h:h + 1, :] = jnp.broadcast_to(a_last + mx_last, (1, 128))


def _mlstm(q, k, vt, gates, nlat_rows):
    b, ttot, _ = q.shape
    L = MLSTM_CHUNK
    nc = ttot // L
    ncl = nlat_rows // L
    ncc = nc - ncl

    def blk(d, c):
        fwd = jnp.where(c < ncc, ncl + c, c - ncc)
        bwd = jnp.where(c < ncc, nc - 1 - c, ncl - 1 - (c - ncc))
        return jnp.where(d == 0, fwd, bwd)

    head_spec = pl.BlockSpec((1, L, D_HEADS), lambda bi, d, c: (bi, blk(d, c), 0))
    return pl.pallas_call(
        functools.partial(_mlstm_kernel, chunk=L), name="mlstm_scan",
        grid=(b, 2, nc),
        in_specs=[head_spec, head_spec,
                  pl.BlockSpec((1, D_HEADS, L), lambda bi, d, c: (bi, 0, blk(d, c))),
                  pl.BlockSpec((1, L, 128), lambda bi, d, c: (bi, blk(d, c), d))],
        out_specs=pl.BlockSpec((1, 1, D_HEADS, L), lambda bi, d, c: (d, bi, 0, blk(d, c))),
        out_shape=jax.ShapeDtypeStruct((2, b, D_HEADS, ttot), BF16),
        scratch_shapes=[pltpu.VMEM((HEADS, HEAD_W + MLSTM_SUM_ROWS, HEAD_W), F32),
                        pltpu.VMEM((8, 128), F32)],
        compiler_params=_cparams("parallel", "parallel", "arbitrary"),
    )(q, k, vt, gates)


def _fft1_kernel(p_ref, g_ref, o_ref):
    for r in range(SUBLANES):
        x = p_ref[0, :, 0, r, :].astype(BF16)
        bb = _dot(g_ref[r].astype(BF16), jnp.concatenate([x[:, :256], x[:, 256:]], axis=0))
        o_ref[0, 0, :, r, :] = bb[:FFT_N1]
        o_ref[0, 1, :, r, :] = bb[FFT_N1:]


def _fft2_kernel(b_ref, t_ref, o_ref, *, kb, scale):
    tc = t_ref[0].astype(BF16)
    ts = t_ref[1].astype(BF16)
    for j in range(kb):
        y = _dot(tc, b_ref[0, 0, j].astype(BF16)) + _dot(ts, b_ref[0, 1, j].astype(BF16))
        o_ref[0, :, j, :] = y * scale


def _dft_ctx_kernel(p_ref, t_ref, o_ref, *, scale):
    p = p_ref[0].astype(BF16)
    st = jnp.concatenate([p[:, :256], p[:, 256:]], axis=0)
    o_ref[0] = _dot(t_ref[...].astype(BF16), st) * scale


def _fft_tables(t):
    n1, n2 = FFT_N1, t // FFT_N1
    k1 = np.arange(n1, dtype=np.int64)[None, :, None]
    nn = (n2 * np.arange(n1, dtype=np.int64)[None, None, :] + np.arange(n2, dtype=np.int64)[:, None, None])
    ang = 2.0 * np.pi * ((k1 * nn) % t).astype(np.float64) / t
    gr, gi = np.cos(ang), -np.sin(ang)
    g = np.concatenate([np.concatenate([gr, -gi], axis=2), np.concatenate([gi, gr], axis=2)], axis=1)
    a2 = 2.0 * np.pi * ((np.arange(n2)[:, None] * np.arange(n2)[None, :]) % n2) / n2
    t2 = np.stack([np.cos(a2), np.sin(a2)])
    return jnp.asarray(g, F32), jnp.asarray(t2, F32)


def _dft_matrix_cs(n):
    a = 2.0 * np.pi * ((np.arange(n)[:, None] * np.arange(n)[None, :]) % n) / n
    return np.cos(a), np.sin(a)


def _channel_dft():
    c, s = _dft_matrix_cs(FNET_GC)
    eye = np.eye(D_SIDE // FNET_GC)
    return jnp.asarray(np.concatenate([np.kron(eye, c), -np.kron(eye, s)], axis=1), F32)


def _fourier(p, t, ctx_len):
    b = p.shape[0]
    n1, n2 = FFT_N1, t // FFT_N1
    g, t2 = _fft_tables(t)
    kb = 16
    ttot = p.shape[1]
    p5 = p.reshape(b, ttot // n2, n2 // SUBLANES, SUBLANES, 512)
    st1 = pl.pallas_call(
        _fft1_kernel, name="fft_stage1",
        grid=(b, n2 // SUBLANES),
        in_specs=[pl.BlockSpec((1, n1, 1, SUBLANES, 512), lambda bi, j: (bi, 0, j, 0, 0)),
                  pl.BlockSpec((SUBLANES, 256, 256), lambda bi, j: (j, 0, 0))],
        out_specs=pl.BlockSpec((1, 2, n1, SUBLANES, 256), lambda bi, j: (bi, 0, 0, j, 0)),
        out_shape=jax.ShapeDtypeStruct((b, 2, n1, n2, 256), F32),
        compiler_params=_cparams("parallel", "parallel"),
    )(p5, g)
    y = pl.pallas_call(
        functools.partial(_fft2_kernel, kb=kb, scale=1.0 / math.sqrt(t * FNET_GC)), name="fft_stage2",
        grid=(b, n1 // kb),
        in_specs=[pl.BlockSpec((1, 2, kb, n2, 256), lambda bi, j: (bi, 0, j, 0, 0)),
                  _const_spec((2, n2, n2))],
        out_specs=pl.BlockSpec((1, n2, kb, 256), lambda bi, j: (bi, 0, j, 0)),
        out_shape=jax.ShapeDtypeStruct((b, n2, n1, 256), F32),
        compiler_params=_cparams("parallel", "parallel"),
    )(st1, t2)
    y_lat = y.reshape(b, t, 256)
    cc, sc = _dft_matrix_cs(ctx_len)
    tc = jnp.asarray(np.concatenate([cc, sc], axis=1), F32)
    y_ctx = pl.pallas_call(
        functools.partial(_dft_ctx_kernel, scale=1.0 / math.sqrt(ctx_len * FNET_GC)), name="dft_ctx",
        grid=(b,),
        in_specs=[pl.BlockSpec((1, ctx_len, 512), lambda bi: (bi, t // ctx_len, 0)),
                  _const_spec((ctx_len, 2 * ctx_len))],
        out_specs=pl.BlockSpec((1, ctx_len, 256), lambda bi: (bi, 0, 0)),
        out_shape=jax.ShapeDtypeStruct((b, ctx_len, 256), F32),
        compiler_params=_cparams("parallel"),
    )(p, tc)
    return y_lat, y_ctx


def _even_finish_kernel(hf_ref, hb_ref, o_ref, yl_ref, yc_ref, x_ref, ctx_ref, mod_ref, gh_ref, w_ref,
                        g2_ref, w1_ref, w2_ref, out_ref, *, nlat):
    i = pl.program_id(1)
    hs = (hf_ref[0, 0].astype(F32) + hb_ref[0, 0].astype(F32)).T
    gate = _sigmoid(o_ref[0])
    gh = gh_ref[...]
    parts = []
    for h in range(HEADS):
        sl = slice(h * HEAD_W, (h + 1) * HEAD_W)
        parts.append((_rms(hs[:, sl], gh[:, sl]) * gate[:, sl]).astype(BF16))
    y = jnp.where(i >= nlat, yc_ref[0], yl_ref[0])
    parts.append(y.astype(BF16))
    out = _dot(jnp.concatenate(parts, axis=1), w_ref[...])
    m = mod_ref[0]
    x1 = _tile_rows(x_ref, ctx_ref, nlat) + m[2:3] * out
    out_ref[0] = _mlp_rows(x1, m, g2_ref[...], w1_ref, w2_ref)


def _mlp_rows(x, m, g, w1_ref, w2_ref):
    h = _norm_mod(x, g, m[3:4], m[4:5]).astype(BF16)
    a = jnp.maximum(_dot(h, w1_ref[...]), 0.0)
    return x + m[5:6] * _dot((a * a).astype(BF16), w2_ref[...])


def _resident_spec(shape):
    nd = len(shape)
    return pl.BlockSpec(shape, lambda *_: (0,) * nd, pipeline_mode=pl.Buffered(1))


def _even_finish(h2, o, y_lat, y_ctx, x, ctx, mod, g_head, w_out, g2, w1, w2, nlat):
    b, t, d = x.shape
    ttot = t + ctx.shape[1]
    tm = ROW_TILE
    nt = ttot // tm
    row = lambda w: pl.BlockSpec((1, tm, w), lambda bi, i: (bi, i, 0))
    return pl.pallas_call(
        functools.partial(_even_finish_kernel, nlat=nlat), name="even_finish_mlp",
        grid=(b, nt),
        in_specs=[pl.BlockSpec((1, 1, D_HEADS, tm), lambda bi, i: (0, bi, 0, i)),
                  pl.BlockSpec((1, 1, D_HEADS, tm), lambda bi, i: (1, bi, 0, i)),
                  row(D_HEADS),
                  pl.BlockSpec((1, tm, 256), lambda bi, i: (bi, jnp.minimum(i, nlat - 1), 0)),
                  pl.BlockSpec((1, tm, 256), lambda bi, i: (bi, jnp.maximum(i - nlat, 0), 0))]
        + _lat_ctx_specs(d, tm, nlat)
        + [_mod_spec(b, nlat), _const_spec((1, D_HEADS)), _resident_spec((d, d)),
           _const_spec((1, d)), _resident_spec(w1.shape), _resident_spec(w2.shape)],
        out_specs=row(d),
        out_shape=jax.ShapeDtypeStruct((b, ttot, d), F32),
        compiler_params=_cparams("parallel", "parallel"),
    )(h2, h2, o, y_lat, y_ctx, x, ctx, mod, g_head, w_out, g2, w1, w2)


def _swap_halves(x):
    lane = lax.broadcasted_iota(jnp.int32, x.shape, 1)
    return jnp.where(lane % 64 < 32, pltpu.roll(x, 96, axis=1), pltpu.roll(x, 32, axis=1))


def _proj_odd_kernel(x_ref, mod_ref, g_ref, w_ref, cos_ref, sin_ref, q_ref, k_ref, v_ref, glu_ref):
    m = mod_ref[0]
    h = _norm_mod(x_ref[0], g_ref[...], m[0:1], m[1:2]).astype(BF16)
    u = _dot(h, w_ref[...])
    cos = cos_ref[...]
    sin = sin_ref[...]
    for hh in range(HEADS):
        sl = slice(hh * HEAD_W, (hh + 1) * HEAD_W)
        qh = u[:, sl]
        kh = u[:, D_HEADS + hh * HEAD_W:D_HEADS + (hh + 1) * HEAD_W]
        q_ref[0, :, sl] = ((qh * cos + _swap_halves(qh) * sin) * Q_SCALE_LOG2).astype(BF16)
        k_ref[0, :, sl] = (kh * cos + _swap_halves(kh) * sin).astype(BF16)
    v_ref[0] = u[:, 2 * D_HEADS:3 * D_HEADS].astype(BF16)
    a = u[:, 3 * D_HEADS:3 * D_HEADS + D_SIDE]
    gte = u[:, 3 * D_HEADS + D_SIDE:]
    glu_ref[0] = a * _sigmoid(gte)


def _rope_tables(t, ctx_len):
    rows = t // GRID_W
    row = np.repeat(np.arange(rows, dtype=np.float64), GRID_W)
    col = np.tile(np.arange(GRID_W, dtype=np.float64), rows)
    n_freq = DIFF_DH // 4
    inv = ROPE_BASE ** (-np.arange(n_freq, dtype=np.float64) / n_freq)
    ang = np.concatenate([row[:, None] * inv, col[:, None] * inv], axis=-1)
    cos, sin = np.cos(ang), np.sin(ang)
    cos = np.concatenate([cos, cos, cos, cos], axis=-1)
    sin = np.concatenate([-sin, sin, -sin, sin], axis=-1)
    cos = np.concatenate([cos, np.ones((ctx_len, 128))], axis=0)
    sin = np.concatenate([sin, np.zeros((ctx_len, 128))], axis=0)
    return jnp.asarray(cos, F32), jnp.asarray(sin, F32)


def _proj_odd(x, mod, g, w_all, cos, sin, nlat):
    b, ttot, d = x.shape
    tm = ROW_TILE
    nt = ttot // tm
    row = lambda w: pl.BlockSpec((1, tm, w), lambda bi, i: (bi, i, 0))
    tab = pl.BlockSpec((tm, 128), lambda bi, i: (i, 0))
    return pl.pallas_call(
        _proj_odd_kernel, name="proj_odd",
        grid=(b, nt),
        in_specs=[row(d), _mod_spec(b, nlat), _const_spec((1, d)), _const_spec(w_all.shape), tab, tab],
        out_specs=[row(D_HEADS), row(D_HEADS), row(D_HEADS), row(D_SIDE)],
        out_shape=[jax.ShapeDtypeStruct((b, ttot, D_HEADS), BF16)] * 3
        + [jax.ShapeDtypeStruct((b, ttot, D_SIDE), F32)],
        compiler_params=_cparams("parallel", "parallel"),
    )(x, mod, g, w_all, cos, sin)


def _attn_kernel(q_ref, k_ref, v_ref, lam_ref, gs_ref, o_ref, acc0, acc1, *, tk, nk, lam_init):
    q = q_ref[0]
    q0 = q[:, :DIFF_DH]
    q1 = q[:, DIFF_DH:]
    tq = q.shape[0]
    one_bf = (lax.broadcasted_iota(jnp.int32, (tk, HEAD_W), 1) == 0).astype(BF16)
    acc0[...] = jnp.zeros_like(acc0)
    acc1[...] = jnp.zeros_like(acc1)

    def online(s, m_old, acc, vext):
        m_new = jnp.maximum(m_old, jnp.max(s, axis=1, keepdims=True))
        p = jnp.exp2(s - m_new).astype(BF16)
        acc[...] = jnp.exp2(m_old - m_new) * acc[...] + _dot(p, vext)
        return m_new

    m0 = m1 = jnp.full((tq, 1), NEG, F32)
    for j in range(nk):
        kk = k_ref[0, j * tk:(j + 1) * tk, :]
        vext = jnp.concatenate([v_ref[0, j * tk:(j + 1) * tk, :], one_bf], axis=1)
        m0 = online(_dot_nt(q0, kk[:, :DIFF_DH]), m0, acc0, vext)
        m1 = online(_dot_nt(q1, kk[:, DIFF_DH:]), m1, acc1, vext)
    lp = lam_ref[...]
    lam = (jnp.exp(jnp.sum(lp[0:1] * lp[1:2], axis=1, keepdims=True))
           - jnp.exp(jnp.sum(lp[2:3] * lp[3:4], axis=1, keepdims=True)) + lam_init)
    a0 = acc0[...]
    a1 = acc1[...]
    o = a0[:, :HEAD_W] / a0[:, HEAD_W:HEAD_W + 1] - lam * (a1[:, :HEAD_W] / a1[:, HEAD_W:HEAD_W + 1])
    o_ref[0] = (_rms(o, gs_ref[...]) * (1.0 - lam_init)).astype(BF16)


def _diff_attention(q, k, v, lam_p, g_sub, t, lam_init):
    b, ttot, _ = q.shape
    tq = ATTN_TQ
    tk = max(n for n in range(128, ATTN_TK + 1, 128) if ttot % n == 0)
    return pl.pallas_call(
        functools.partial(_attn_kernel, tk=tk, nk=ttot // tk, lam_init=lam_init), name="diff_attn",
        grid=(b, HEADS, t // tq),
        in_specs=[pl.BlockSpec((1, tq, HEAD_W), lambda bi, h, i: (bi, i, h)),
                  pl.BlockSpec((1, ttot, HEAD_W), lambda bi, h, i: (bi, 0, h)),
                  pl.BlockSpec((1, ttot, HEAD_W), lambda bi, h, i: (bi, 0, h)),
                  _const_spec(lam_p.shape), _const_spec((1, HEAD_W))],
        out_specs=pl.BlockSpec((1, tq, HEAD_W), lambda bi, h, i: (bi, i, h)),
        out_shape=jax.ShapeDtypeStruct((b, t, D_HEADS), BF16),
        scratch_shapes=[pltpu.VMEM((tq, 2 * HEAD_W), F32)] * 2,
        compiler_params=_cparams("parallel", "parallel", "parallel"),
    )(q, k, v, lam_p, g_sub)


def _odd_finish_kernel(o_ref, glu_ref, prev_ref, next_ref, wdw_ref, gln_ref, bln_ref, x_ref, mod_ref,
                       w_ref, g2_ref, w1_ref, w2_ref, gf_ref, out_ref, xs, *, tm, nlat):
    i = pl.program_id(1)
    hal = CONV_HALO
    xs[0:hal, :] = jnp.where(i == 0, 0.0, prev_ref[0])
    xs[hal:hal + tm, :] = glu_ref[0]
    xs[hal + tm:, :] = jnp.where(i == nlat - 1, 0.0, next_ref[0])
    wdw = wdw_ref[...]
    z = jnp.zeros((tm, D_SIDE), F32)
    off = hal - (CONV_W - 1) // 2
    for r in range(SUBLANES):
        u = None
        for a in range((off + CONV_W - 1) // SUBLANES + 1):
            kk = SUBLANES * a + r - off
            if 0 <= kk < CONV_W:
                term = xs[SUBLANES * a:SUBLANES * a + tm + SUBLANES, :] * wdw[kk:kk + 1]
                u = term if u is None else u + term
        z = z + u[r:r + tm]
    mu = jnp.mean(z, axis=-1, keepdims=True)
    zc = z - mu
    var = jnp.mean(zc * zc, axis=-1, keepdims=True)
    z = zc * lax.rsqrt(var + EPS) * gln_ref[...] + bln_ref[...]
    z = z * _sigmoid(z)
    cat = jnp.concatenate([o_ref[0], z.astype(BF16)], axis=1)
    m = mod_ref[0]
    x3 = x_ref[0] + m[2:3] * _dot(cat, w_ref[...])
    out_ref[0] = _rms(_mlp_rows(x3, m, g2_ref[...], w1_ref, w2_ref), gf_ref[...])


def _odd_finish(o, glu, w_dw, g_ln, b_ln, x, mod, w_out, g2, w1, w2, g_final, t):
    b, _, d = x.shape
    tm = LATENT_ROW_TILE
    nlat = t // tm
    r = tm // CONV_HALO
    row = lambda w: pl.BlockSpec((1, tm, w), lambda bi, i: (bi, i, 0))
    return pl.pallas_call(
        functools.partial(_odd_finish_kernel, tm=tm, nlat=nlat), name="odd_finish_mlp",
        grid=(b, nlat),
        in_specs=[row(D_HEADS), row(D_SIDE),
                  pl.BlockSpec((1, CONV_HALO, D_SIDE), lambda bi, i: (bi, jnp.maximum(i * r - 1, 0), 0)),
                  pl.BlockSpec((1, CONV_HALO, D_SIDE), lambda bi, i: (bi, (i + 1) * r, 0)),
                  _const_spec((CONV_W, D_SIDE)), _const_spec((1, D_SIDE)), _const_spec((1, D_SIDE)),
                  row(d), _mod_spec(b, nlat), _resident_spec((d, d)),
                  _const_spec((1, d)), _resident_spec(w1.shape), _resident_spec(w2.shape),
                  _const_spec((1, d))],
        out_specs=row(d),
        out_shape=jax.ShapeDtypeStruct((b, t, d), F32),
        scratch_shapes=[pltpu.VMEM((tm + 2 * CONV_HALO, D_SIDE), F32)],
        compiler_params=_cparams("parallel", "parallel"),
    )(o, glu, glu, glu, w_dw, g_ln, b_ln, x, mod, w_out, g2, w1, w2, g_final)


def kernel(x, c, ctx, c_ctx, w_mod, b_mod, g_norm, w_in_even, b_gate, w_qk_conv, g_mlstm_head,
           w_out_even, w_in_odd, lam_p, g_subln, w_dw, g_conv_ln, b_conv_ln, w_out_odd,
           w_ff1, w_ff2, g_final):
    b, t, d = x.shape
    ctx_len = ctx.shape[1]
    assert w_mod.shape[0] == 2 and d == 1024 and ctx_len == ROW_TILE and t % (FFT_N1 * 8) == 0
    nlat = t // ROW_TILE
    mod = _mod_vectors(c, c_ctx, w_mod, b_mod)

    we = w_in_even[0]
    g0 = 4 * D_HEADS
    w_gate = jnp.zeros((d, 256), F32).at[:, 0:12].set(we[:, g0:g0 + 12]).at[:, 128:140].set(we[:, g0 + 12:g0 + 24])
    w_all = jnp.concatenate([we[:, :g0], w_gate, we[:, g0 + 24:]], axis=1).astype(BF16)
    bg = jnp.zeros((1, 256), F32).at[0, 0:12].set(b_gate[0, :12]).at[0, 128:140].set(b_gate[0, 12:])
    q, k, v, o, gates, p = _proj_even(x, ctx, mod[0], g_norm[0, 0][None], w_all, bg, _channel_dft(),
                                      w_qk_conv[0], nlat)
    h2 = _mlstm(q, k, v, gates, t)
    y_lat, y_ctx = _fourier(p, t, ctx_len)
    x2 = _even_finish(h2, o, y_lat, y_ctx, x, ctx, mod[0], g_mlstm_head[0][None],
                      w_out_even[0].astype(BF16), g_norm[0, 1][None], w_ff1[0].astype(BF16),
                      w_ff2[0].astype(BF16), nlat)

    lam_init = 0.8 - 0.6 * math.exp(-0.3 * 1)
    cos, sin = _rope_tables(t, ctx_len)
    qa, ka, va, glu = _proj_odd(x2, mod[1], g_norm[1, 0][None], w_in_odd[0].astype(BF16), cos, sin, nlat)
    oa = _diff_attention(qa, ka, va, lam_p[0], g_subln[0][None], t, lam_init)
    return _odd_finish(oa, glu, w_dw[0], g_conv_ln[0][None], b_conv_ln[0][None], x2, mod[1],
                       w_out_odd[0].astype(BF16), g_norm[1, 1][None], w_ff1[1].astype(BF16),
                       w_ff2[1].astype(BF16), g_final[None], t)
```

```python
import functools
import math

import numpy as np
import jax
import jax.numpy as jnp
from jax import lax
from jax.experimental import pallas as pl
from jax.experimental.pallas import tpu as pltpu

F32 = jnp.float32
BF16 = jnp.bfloat16

EPS = 1e-6
SUBLANES = 8
HEADS = 6
HEAD_W = 128
D_HEADS = HEADS * HEAD_W
D_SIDE = 256
FNET_GC = 64
QK_W = 2 * D_HEADS
CONV_W = 31
CONV_HALO = 16
GRID_W = 64
ROPE_BASE = 10000.0
DIFF_DH = 64
FFT_N1 = 128
ROW_TILE = 256
LATENT_ROW_TILE = 512
MLSTM_CHUNK = 256
MLSTM_SUM_ROWS = 16
ATTN_TQ = 1024
ATTN_TK = 1408
NEG = -1e30
Q_SCALE_LOG2 = (DIFF_DH ** -0.5) * math.log2(math.e)
VMEM_LIMIT = 56 * 1024 * 1024


def _cparams(*sem, flags=None):
    return pltpu.CompilerParams(dimension_semantics=sem, vmem_limit_bytes=VMEM_LIMIT, flags=flags)


def _dot(a, b):
    return jnp.dot(a, b, preferred_element_type=F32)


def _dot_nt(a, b):
    return lax.dot_general(a, b, (((1,), (1,)), ((), ())), preferred_element_type=F32)


def _dot_tn(a, b):
    return lax.dot_general(a, b, (((0,), (0,)), ((), ())), preferred_element_type=F32)


def _sigmoid(x):
    return 1.0 / (1.0 + jnp.exp(-x))


def _rms(x, g):
    return x * lax.rsqrt(jnp.mean(x * x, axis=-1, keepdims=True) + EPS) * g


def _norm_mod(x, g, shift, scale):
    return _rms(x, g) * (1.0 + scale) + shift


def _mod_kernel(s_ref, w_ref, b_ref, o_ref):
    s = s_ref[...]
    s = s * _sigmoid(s)
    o_ref[0] = _dot(s.astype(BF16), w_ref[0].astype(BF16)) + b_ref[0]


def _mod_vectors(c, c_ctx, w_mod, b_mod):
    depth, d, n = w_mod.shape
    b = c.shape[0]
    s = jnp.zeros((8, d), F32).at[:b].set(c).at[b].set(c_ctx)
    tn = 1536
    out = pl.pallas_call(
        _mod_kernel, name="adaln_mod",
        grid=(depth, n // tn),
        in_specs=[pl.BlockSpec((8, d), lambda l, j: (0, 0)),
                  pl.BlockSpec((1, d, tn), lambda l, j: (l, 0, j)),
                  pl.BlockSpec((1, 1, tn), lambda l, j: (l, 0, j))],
        out_specs=pl.BlockSpec((1, 8, tn), lambda l, j: (l, 0, j)),
        out_shape=jax.ShapeDtypeStruct((depth, 8, n), F32),
        compiler_params=_cparams("parallel", "parallel"),
    )(s, w_mod, b_mod.reshape(depth, 1, n))
    return out.reshape(depth, 8, 6, d)


def _mod_spec(b, nlat):
    return pl.BlockSpec((1, 6, 1024), lambda bi, i: (jnp.where(i >= nlat, b, bi), 0, 0))


def _const_spec(shape):
    nd = len(shape)
    return pl.BlockSpec(shape, lambda *_: (0,) * nd)


def _tile_rows(x_ref, ctx_ref, nlat):
    return jnp.where(pl.program_id(1) >= nlat, ctx_ref[0], x_ref[0])


def _lat_ctx_specs(d, tm, nlat):
    return [pl.BlockSpec((1, tm, d), lambda bi, i: (bi, jnp.minimum(i, nlat - 1), 0)),
            pl.BlockSpec((1, tm, d), lambda bi, i: (bi, jnp.maximum(i - nlat, 0), 0))]


def _proj_even_kernel(x_ref, ctx_ref, xp_ref, xn_ref, mod_ref, g_ref, w_ref, bg_ref, dft_ref, wc_ref,
                      q_ref, k_ref, vt_ref, o_ref, gt_ref, p_ref, *, tm, nlat):
    i = pl.program_id(1)
    m = mod_ref[0]
    g = g_ref[...]
    h = _norm_mod(_tile_rows(x_ref, ctx_ref, nlat), g, m[0:1], m[1:2]).astype(BF16)
    xq = _dot(h, w_ref[:, :QK_W])
    u = _dot(h, w_ref[:, QK_W:])
    halo = jnp.concatenate([xp_ref[0], xn_ref[0]], axis=0)
    uh = _dot(_norm_mod(halo, g, m[0:1], m[1:2]).astype(BF16), w_ref[:, :QK_W])
    prow = jnp.where(jnp.logical_or(i == 0, i >= nlat), 0.0, uh[SUBLANES - 1:SUBLANES])
    nrow = jnp.where(i >= nlat - 1, 0.0, uh[SUBLANES:SUBLANES + 1])
    rid = lax.broadcasted_iota(jnp.int32, xq.shape, 0)
    xm = jnp.where(rid == 0, prow, pltpu.roll(xq, 1, axis=0))
    xp = jnp.where(rid == tm - 1, nrow, pltpu.roll(xq, tm - 1, axis=0))
    wc = wc_ref[...]
    y = xm * wc[0:1] + xq * wc[1:2] + xp * wc[2:3]
    y = y * _sigmoid(y)
    q_ref[0] = (y[:, :D_HEADS] * (HEAD_W ** -0.5)).astype(BF16)
    k_ref[0] = y[:, D_HEADS:].astype(BF16)
    vt_ref[0] = u[:, :D_HEADS].T.astype(BF16)
    o_ref[0] = u[:, D_HEADS:2 * D_HEADS]
    c0 = 2 * D_HEADS
    gt_ref[0] = u[:, c0:c0 + 256] + bg_ref[...]
    f = u[:, c0 + 256:c0 + 512].astype(BF16)
    p_ref[0] = _dot(f, dft_ref[...].astype(BF16))


def _proj_even(x, ctx, mod, g, w_all, bg, dftc, w_conv, nlat):
    b, t, d = x.shape
    ttot = t + ctx.shape[1]
    tm = ROW_TILE
    nt = ttot // tm
    r8 = tm // SUBLANES
    last8 = t // SUBLANES - 1
    row = lambda w: pl.BlockSpec((1, tm, w), lambda bi, i: (bi, i, 0))
    halo = lambda f: pl.BlockSpec((1, SUBLANES, d), lambda bi, i: (bi, jnp.clip(f(i), 0, last8), 0))
    return pl.pallas_call(
        functools.partial(_proj_even_kernel, tm=tm, nlat=nlat), name="proj_even",
        grid=(b, nt),
        in_specs=_lat_ctx_specs(d, tm, nlat)
        + [halo(lambda i: i * r8 - 1), halo(lambda i: (i + 1) * r8),
           _mod_spec(b, nlat), _const_spec((1, d)), _const_spec(w_all.shape), _const_spec((1, 256)),
           _const_spec(dftc.shape), _const_spec((3, QK_W))],
        out_specs=[row(D_HEADS), row(D_HEADS), pl.BlockSpec((1, D_HEADS, tm), lambda bi, i: (bi, 0, i)),
                   row(D_HEADS), row(256), row(512)],
        out_shape=[jax.ShapeDtypeStruct((b, ttot, D_HEADS), BF16),
                   jax.ShapeDtypeStruct((b, ttot, D_HEADS), BF16),
                   jax.ShapeDtypeStruct((b, D_HEADS, ttot), BF16),
                   jax.ShapeDtypeStruct((b, ttot, D_HEADS), F32),
                   jax.ShapeDtypeStruct((b, ttot, 256), F32),
                   jax.ShapeDtypeStruct((b, ttot, 512), F32)],
        compiler_params=_cparams("parallel", "parallel"),
    )(x, ctx, x, x, mod, g, w_all, bg, dftc, w_conv)


def _mlstm_kernel(qf_ref, kf_ref, vtf_ref, gf_ref, qb_ref, kb_ref, vtb_ref, gb_ref, hf_ref, hb_ref,
                  c_scr, m_scr, *, chunk):
    c = pl.program_id(1)

    @pl.when(c == 0)
    def _():
        c_scr[...] = jnp.zeros_like(c_scr)
        m_scr[...] = jnp.zeros_like(m_scr)

    row = lax.broadcasted_iota(jnp.int32, (chunk, chunk), 0)
    col = lax.broadcasted_iota(jnp.int32, (chunk, chunk), 1)
    lower, upper = col <= row, col >= row
    _mlstm_direction(qf_ref, kf_ref, vtf_ref, gf_ref, hf_ref, c_scr.at[0], m_scr.at[0], lower, upper, chunk)
    _mlstm_direction(qb_ref, kb_ref, vtb_ref, gb_ref, hb_ref, c_scr.at[1], m_scr.at[1], upper, lower, chunk)


def _mlstm_direction(q_ref, k_ref, vt_ref, g_ref, h_ref, c_scr, m_scr, before, mask_t, chunk):
    gates = g_ref[0]
    logf = jnp.minimum(gates, 0.0) - jnp.log(1.0 + jnp.exp(-jnp.abs(gates)))
    csum = before.astype(BF16)
    f_hi = logf.astype(BF16)
    f_mid = (logf - f_hi.astype(F32)).astype(BF16)
    f_lo = (logf - f_hi.astype(F32) - f_mid.astype(F32)).astype(BF16)
    a_all = _dot(csum, f_hi) + _dot(csum, f_mid) + _dot(csum, f_lo)
    r_all = gates - pltpu.roll(a_all, 128 - HEADS, axis=1)
    lane = lax.broadcasted_iota(jnp.int32, (chunk, 128), 1)
    rows = jnp.where(lane < HEADS, r_all, a_all).T
    ones = jnp.ones((MLSTM_SUM_ROWS, chunk), BF16)

    for h in range(HEADS):
        sl = slice(h * HEAD_W, (h + 1) * HEAD_W)
        qh = q_ref[0, :, sl]
        kh = k_ref[0, :, sl]
        vext = jnp.concatenate([vt_ref[0, sl, :], ones], axis=0)
        m_prev = m_scr[h:h + 1, 0:1]
        r_row = rows[h:h + 1, :]
        a_row = rows[HEADS + h:HEADS + h + 1, :]
        rm = jnp.where(mask_t, r_all[:, h:h + 1], NEG)
        mx = jnp.maximum(jnp.max(rm, axis=0, keepdims=True), m_prev)
        st = _dot_nt(kh, qh) * jnp.exp(rm - mx)
        ce = c_scr[h]
        nd = _dot(vext, st.astype(BF16)) + jnp.exp(m_prev - mx) * _dot_nt(ce.astype(BF16), qh)
        den = jnp.maximum(jnp.abs(nd[HEAD_W:HEAD_W + 1]), jnp.exp(-(a_row + mx)))
        h_ref[0, sl, :] = (nd[:HEAD_W] / den).astype(BF16)
        mx_last = jnp.max(mx, axis=1, keepdims=True)
        a_last = jnp.min(a_row, axis=1, keepdims=True)
        wv = (vext.astype(F32) * jnp.exp(r_row - mx_last)).astype(BF16)
        c_scr[h] = jnp.exp(m_prev - mx_last) * ce + _dot(wv, kh)
        m_scr[h:h + 1, :] = jnp.broadcast_to(a_last + mx_last, (1, 128))


def _mlstm(q, k, vt, gates, nlat_rows):
    b, ttot, _ = q.shape
    L = MLSTM_CHUNK
    nc = ttot // L
    ncl = nlat_rows // L
    ncc = nc - ncl

    def fwd(c):
        return jnp.where(c < ncc, ncl + c, c - ncc)

    def bwd(c):
        return jnp.where(c < ncc, nc - 1 - c, ncl - 1 - (c - ncc))

    def specs(blk, d):
        head = pl.BlockSpec((1, L, D_HEADS), lambda bi, c: (bi, blk(c), 0))
        return [head, head, pl.BlockSpec((1, D_HEADS, L), lambda bi, c: (bi, 0, blk(c))),
                pl.BlockSpec((1, L, 128), lambda bi, c: (bi, blk(c), d))]

    out = lambda blk: pl.BlockSpec((1, D_HEADS, L), lambda bi, c: (bi, 0, blk(c)))
    return pl.pallas_call(
        functools.partial(_mlstm_kernel, chunk=L), name="mlstm_scan",
        grid=(b, nc),
        in_specs=specs(fwd, 0) + specs(bwd, 1),
        out_specs=[out(fwd), out(bwd)],
        out_shape=[jax.ShapeDtypeStruct((b, D_HEADS, ttot), BF16)] * 2,
        scratch_shapes=[pltpu.VMEM((2, HEADS, HEAD_W + MLSTM_SUM_ROWS, HEAD_W), F32),
                        pltpu.VMEM((2, 8, 128), F32)],
        compiler_params=_cparams("parallel", "arbitrary"),
    )(q, k, vt, gates, q, k, vt, gates)


def _fft1_kernel(p_ref, g_ref, o_ref):
    for r in range(SUBLANES):
        x = p_ref[0, :, 0, r, :].astype(BF16)
        bb = _dot(g_ref[r].astype(BF16), jnp.concatenate([x[:, :256], x[:, 256:]], axis=0))
        o_ref[0, 0, :, r, :] = bb[:FFT_N1]
        o_ref[0, 1, :, r, :] = bb[FFT_N1:]


def _fft2_kernel(b_ref, t_ref, o_ref, *, kb, scale):
    tc = t_ref[0].astype(BF16)
    ts = t_ref[1].astype(BF16)
    for j in range(kb):
        y = _dot(tc, b_ref[0, 0, j].astype(BF16)) + _dot(ts, b_ref[0, 1, j].astype(BF16))
        o_ref[0, :, j, :] = y * scale


def _dft_ctx_kernel(p_ref, t_ref, o_ref, *, scale):
    p = p_ref[0].astype(BF16)
    st = jnp.concatenate([p[:, :256], p[:, 256:]], axis=0)
    o_ref[0] = _dot(t_ref[...].astype(BF16), st) * scale


def _fft_tables(t):
    n1, n2 = FFT_N1, t // FFT_N1
    k1 = np.arange(n1, dtype=np.int64)[None, :, None]
    nn = (n2 * np.arange(n1, dtype=np.int64)[None, None, :] + np.arange(n2, dtype=np.int64)[:, None, None])
    ang = 2.0 * np.pi * ((k1 * nn) % t).astype(np.float64) / t
    gr, gi = np.cos(ang), -np.sin(ang)
    g = np.concatenate([np.concatenate([gr, -gi], axis=2), np.concatenate([gi, gr], axis=2)], axis=1)
    a2 = 2.0 * np.pi * ((np.arange(n2)[:, None] * np.arange(n2)[None, :]) % n2) / n2
    t2 = np.stack([np.cos(a2), np.sin(a2)])
    return jnp.asarray(g, F32), jnp.asarray(t2, F32)


def _dft_matrix_cs(n):
    a = 2.0 * np.pi * ((np.arange(n)[:, None] * np.arange(n)[None, :]) % n) / n
    return np.cos(a), np.sin(a)


def _channel_dft():
    c, s = _dft_matrix_cs(FNET_GC)
    eye = np.eye(D_SIDE // FNET_GC)
    return jnp.asarray(np.concatenate([np.kron(eye, c), -np.kron(eye, s)], axis=1), F32)


def _fourier(p, t, ctx_len):
    b = p.shape[0]
    n1, n2 = FFT_N1, t // FFT_N1
    g, t2 = _fft_tables(t)
    kb = 16
    ttot = p.shape[1]
    p5 = p.reshape(b, ttot // n2, n2 // SUBLANES, SUBLANES, 512)
    st1 = pl.pallas_call(
        _fft1_kernel, name="fft_stage1",
        grid=(b, n2 // SUBLANES),
        in_specs=[pl.BlockSpec((1, n1, 1, SUBLANES, 512), lambda bi, j: (bi, 0, j, 0, 0)),
                  pl.BlockSpec((SUBLANES, 256, 256), lambda bi, j: (j, 0, 0))],
        out_specs=pl.BlockSpec((1, 2, n1, SUBLANES, 256), lambda bi, j: (bi, 0, 0, j, 0)),
        out_shape=jax.ShapeDtypeStruct((b, 2, n1, n2, 256), F32),
        compiler_params=_cparams("parallel", "parallel"),
    )(p5, g)
    y = pl.pallas_call(
        functools.partial(_fft2_kernel, kb=kb, scale=1.0 / math.sqrt(t * FNET_GC)), name="fft_stage2",
        grid=(b, n1 // kb),
        in_specs=[pl.BlockSpec((1, 2, kb, n2, 256), lambda bi, j: (bi, 0, j, 0, 0)),
                  _const_spec((2, n2, n2))],
        out_specs=pl.BlockSpec((1, n2, kb, 256), lambda bi, j: (bi, 0, j, 0)),
        out_shape=jax.ShapeDtypeStruct((b, n2, n1, 256), F32),
        compiler_params=_cparams("parallel", "parallel"),
    )(st1, t2)
    y_lat = y.reshape(b, t, 256)
    cc, sc = _dft_matrix_cs(ctx_len)
    tc = jnp.asarray(np.concatenate([cc, sc], axis=1), F32)
    y_ctx = pl.pallas_call(
        functools.partial(_dft_ctx_kernel, scale=1.0 / math.sqrt(ctx_len * FNET_GC)), name="dft_ctx",
        grid=(b,),
        in_specs=[pl.BlockSpec((1, ctx_len, 512), lambda bi: (bi, t // ctx_len, 0)),
                  _const_spec((ctx_len, 2 * ctx_len))],
        out_specs=pl.BlockSpec((1, ctx_len, 256), lambda bi: (bi, 0, 0)),
        out_shape=jax.ShapeDtypeStruct((b, ctx_len, 256), F32),
        compiler_params=_cparams("parallel"),
    )(p, tc)
    return y_lat, y_ctx


def _even_finish_kernel(hf_ref, hb_ref, o_ref, yl_ref, yc_ref, x_ref, ctx_ref, mod_ref, gh_ref, w_ref,
                        g2_ref, w1_ref, w2_ref, out_ref, *, nlat):
    i = pl.program_id(1)
    hs = (hf_ref[0].astype(F32) + hb_ref[0].astype(F32)).T
    gate = _sigmoid(o_ref[0])
    gh = gh_ref[...]
    parts = []
    for h in range(HEADS):
        sl = slice(h * HEAD_W, (h + 1) * HEAD_W)
        parts.append((_rms(hs[:, sl], gh[:, sl]) * gate[:, sl]).astype(BF16))
    y = jnp.where(i >= nlat, yc_ref[0], yl_ref[0])
    parts.append(y.astype(BF16))
    out = _dot(jnp.concatenate(parts, axis=1), w_ref[...])
    m = mod_ref[0]
    x1 = _tile_rows(x_ref, ctx_ref, nlat) + m[2:3] * out
    out_ref[0] = _mlp_rows(x1, m, g2_ref[...], w1_ref, w2_ref)


def _mlp_rows(x, m, g, w1_ref, w2_ref):
    h = _norm_mod(x, g, m[3:4], m[4:5]).astype(BF16)
    a = jnp.maximum(_dot(h, w1_ref[...]), 0.0)
    return x + m[5:6] * _dot((a * a).astype(BF16), w2_ref[...])


def _resident_spec(shape):
    nd = len(shape)
    return pl.BlockSpec(shape, lambda *_: (0,) * nd, pipeline_mode=pl.Buffered(1))


def _even_finish(hf, hb, o, y_lat, y_ctx, x, ctx, mod, g_head, w_out, g2, w1, w2, nlat):
    b, t, d = x.shape
    ttot = t + ctx.shape[1]
    tm = ROW_TILE
    nt = ttot // tm
    row = lambda w: pl.BlockSpec((1, tm, w), lambda bi, i: (bi, i, 0))
    return pl.pallas_call(
        functools.partial(_even_finish_kernel, nlat=nlat), name="even_finish_mlp",
        grid=(b, nt),
        in_specs=[pl.BlockSpec((1, D_HEADS, tm), lambda bi, i: (bi, 0, i)),
                  pl.BlockSpec((1, D_HEADS, tm), lambda bi, i: (bi, 0, i)),
                  row(D_HEADS),
                  pl.BlockSpec((1, tm, 256), lambda bi, i: (bi, jnp.minimum(i, nlat - 1), 0)),
                  pl.BlockSpec((1, tm, 256), lambda bi, i: (bi, jnp.maximum(i - nlat, 0), 0))]
        + _lat_ctx_specs(d, tm, nlat)
        + [_mod_spec(b, nlat), _const_spec((1, D_HEADS)), _resident_spec((d, d)),
           _const_spec((1, d)), _resident_spec(w1.shape), _resident_spec(w2.shape)],
        out_specs=row(d),
        out_shape=jax.ShapeDtypeStruct((b, ttot, d), F32),
        compiler_params=_cparams("parallel", "parallel"),
    )(hf, hb, o, y_lat, y_ctx, x, ctx, mod, g_head, w_out, g2, w1, w2)


def _swap_halves(x):
    lane = lax.broadcasted_iota(jnp.int32, x.shape, 1)
    return jnp.where(lane % 64 < 32, pltpu.roll(x, 96, axis=1), pltpu.roll(x, 32, axis=1))


def _proj_odd_kernel(x_ref, mod_ref, g_ref, w_ref, cos_ref, sin_ref, q_ref, k_ref, v_ref, glu_ref):
    m = mod_ref[0]
    h = _norm_mod(x_ref[0], g_ref[...], m[0:1], m[1:2]).astype(BF16)
    u = _dot(h, w_ref[...])
    cos = cos_ref[...]
    sin = sin_ref[...]
    for hh in range(HEADS):
        sl = slice(hh * HEAD_W, (hh + 1) * HEAD_W)
        qh = u[:, sl]
        kh = u[:, D_HEADS + hh * HEAD_W:D_HEADS + (hh + 1) * HEAD_W]
        q_ref[0, :, sl] = ((qh * cos + _swap_halves(qh) * sin) * Q_SCALE_LOG2).astype(BF16)
        k_ref[0, :, sl] = (kh * cos + _swap_halves(kh) * sin).astype(BF16)
    v_ref[0] = u[:, 2 * D_HEADS:3 * D_HEADS].astype(BF16)
    a = u[:, 3 * D_HEADS:3 * D_HEADS + D_SIDE]
    gte = u[:, 3 * D_HEADS + D_SIDE:]
    glu_ref[0] = a * _sigmoid(gte)


def _rope_tables(t, ctx_len):
    rows = t // GRID_W
    row = np.repeat(np.arange(rows, dtype=np.float64), GRID_W)
    col = np.tile(np.arange(GRID_W, dtype=np.float64), rows)
    n_freq = DIFF_DH // 4
    inv = ROPE_BASE ** (-np.arange(n_freq, dtype=np.float64) / n_freq)
    ang = np.concatenate([row[:, None] * inv, col[:, None] * inv], axis=-1)
    cos, sin = np.cos(ang), np.sin(ang)
    cos = np.concatenate([cos, cos, cos, cos], axis=-1)
    sin = np.concatenate([-sin, sin, -sin, sin], axis=-1)
    cos = np.concatenate([cos, np.ones((ctx_len, 128))], axis=0)
    sin = np.concatenate([sin, np.zeros((ctx_len, 128))], axis=0)
    return jnp.asarray(cos, F32), jnp.asarray(sin, F32)


def _proj_odd(x, mod, g, w_all, cos, sin, nlat):
    b, ttot, d = x.shape
    tm = ROW_TILE
    nt = ttot // tm
    row = lambda w: pl.BlockSpec((1, tm, w), lambda bi, i: (bi, i, 0))
    tab = pl.BlockSpec((tm, 128), lambda bi, i: (i, 0))
    return pl.pallas_call(
        _proj_odd_kernel, name="proj_odd",
        grid=(b, nt),
        in_specs=[row(d), _mod_spec(b, nlat), _const_spec((1, d)), _const_spec(w_all.shape), tab, tab],
        out_specs=[row(D_HEADS), row(D_HEADS), row(D_HEADS), row(D_SIDE)],
        out_shape=[jax.ShapeDtypeStruct((b, ttot, D_HEADS), BF16)] * 3
        + [jax.ShapeDtypeStruct((b, ttot, D_SIDE), F32)],
        compiler_params=_cparams("parallel", "parallel"),
    )(x, mod, g, w_all, cos, sin)


def _attn_kernel(q_ref, k_ref, v_ref, lam_ref, gs_ref, o_ref, acc0, acc1, *, tk, nk, lam_init):
    q = q_ref[0]
    q0 = q[:, :DIFF_DH]
    q1 = q[:, DIFF_DH:]
    tq = q.shape[0]
    one_bf = (lax.broadcasted_iota(jnp.int32, (tk, HEAD_W), 1) == 0).astype(BF16)
    acc0[...] = jnp.zeros_like(acc0)
    acc1[...] = jnp.zeros_like(acc1)

    def online(s, m_old, acc, vext):
        m_new = jnp.maximum(m_old, jnp.max(s, axis=1, keepdims=True))
        p = jnp.exp2(s - m_new).astype(BF16)
        acc[...] = jnp.exp2(m_old - m_new) * acc[...] + _dot(p, vext)
        return m_new

    m0 = m1 = jnp.full((tq, 1), NEG, F32)
    for j in range(nk):
        kk = k_ref[0, j * tk:(j + 1) * tk, :]
        vext = jnp.concatenate([v_ref[0, j * tk:(j + 1) * tk, :], one_bf], axis=1)
        m0 = online(_dot_nt(q0, kk[:, :DIFF_DH]), m0, acc0, vext)
        m1 = online(_dot_nt(q1, kk[:, DIFF_DH:]), m1, acc1, vext)
    lp = lam_ref[...]
    lam = (jnp.exp(jnp.sum(lp[0:1] * lp[1:2], axis=1, keepdims=True))
           - jnp.exp(jnp.sum(lp[2:3] * lp[3:4], axis=1, keepdims=True)) + lam_init)
    a0 = acc0[...]
    a1 = acc1[...]
    o = a0[:, :HEAD_W] / a0[:, HEAD_W:HEAD_W + 1] - lam * (a1[:, :HEAD_W] / a1[:, HEAD_W:HEAD_W + 1])
    o_ref[0] = (_rms(o, gs_ref[...]) * (1.0 - lam_init)).astype(BF16)


def _diff_attention(q, k, v, lam_p, g_sub, t, lam_init):
    b, ttot, _ = q.shape
    tq = ATTN_TQ
    tk = max(n for n in range(128, ATTN_TK + 1, 128) if ttot % n == 0)
    return pl.pallas_call(
        functools.partial(_attn_kernel, tk=tk, nk=ttot // tk, lam_init=lam_init), name="diff_attn",
        grid=(b, HEADS, t // tq),
        in_specs=[pl.BlockSpec((1, tq, HEAD_W), lambda bi, h, i: (bi, i, h)),
                  pl.BlockSpec((1, ttot, HEAD_W), lambda bi, h, i: (bi, 0, h)),
                  pl.BlockSpec((1, ttot, HEAD_W), lambda bi, h, i: (bi, 0, h)),
                  _const_spec(lam_p.shape), _const_spec((1, HEAD_W))],
        out_specs=pl.BlockSpec((1, tq, HEAD_W), lambda bi, h, i: (bi, i, h)),
        out_shape=jax.ShapeDtypeStruct((b, t, D_HEADS), BF16),
        scratch_shapes=[pltpu.VMEM((tq, 2 * HEAD_W), F32)] * 2,
        compiler_params=_cparams("parallel", "parallel", "parallel"),
    )(q, k, v, lam_p, g_sub)


def _odd_finish_kernel(o_ref, glu_ref, prev_ref, next_ref, wdw_ref, gln_ref, bln_ref, x_ref, mod_ref,
                       w_ref, g2_ref, w1_ref, w2_ref, gf_ref, out_ref, xs, *, tm, nlat):
    i = pl.program_id(1)
    hal = CONV_HALO
    xs[0:hal, :] = jnp.where(i == 0, 0.0, prev_ref[0])
    xs[hal:hal + tm, :] = glu_ref[0]
    xs[hal + tm:, :] = jnp.where(i == nlat - 1, 0.0, next_ref[0])
    wdw = wdw_ref[...]
    z = jnp.zeros((tm, D_SIDE), F32)
    off = hal - (CONV_W - 1) // 2
    for r in range(SUBLANES):
        u = None
        for a in range((off + CONV_W - 1) // SUBLANES + 1):
            kk = SUBLANES * a + r - off
            if 0 <= kk < CONV_W:
                term = xs[SUBLANES * a:SUBLANES * a + tm + SUBLANES, :] * wdw[kk:kk + 1]
                u = term if u is None else u + term
        z = z + u[r:r + tm]
    mu = jnp.mean(z, axis=-1, keepdims=True)
    zc = z - mu
    var = jnp.mean(zc * zc, axis=-1, keepdims=True)
    z = zc * lax.rsqrt(var + EPS) * gln_ref[...] + bln_ref[...]
    z = z * _sigmoid(z)
    cat = jnp.concatenate([o_ref[0], z.astype(BF16)], axis=1)
    m = mod_ref[0]
    x3 = x_ref[0] + m[2:3] * _dot(cat, w_ref[...])
    out_ref[0] = _rms(_mlp_rows(x3, m, g2_ref[...], w1_ref, w2_ref), gf_ref[...])


def _odd_finish(o, glu, w_dw, g_ln, b_ln, x, mod, w_out, g2, w1, w2, g_final, t):
    b, _, d = x.shape
    tm = LATENT_ROW_TILE
    nlat = t // tm
    r = tm // CONV_HALO
    row = lambda w: pl.BlockSpec((1, tm, w), lambda bi, i: (bi, i, 0))
    return pl.pallas_call(
        functools.partial(_odd_finish_kernel, tm=tm, nlat=nlat), name="odd_finish_mlp",
        grid=(b, nlat),
        in_specs=[row(D_HEADS), row(D_SIDE),
                  pl.BlockSpec((1, CONV_HALO, D_SIDE), lambda bi, i: (bi, jnp.maximum(i * r - 1, 0), 0)),
                  pl.BlockSpec((1, CONV_HALO, D_SIDE), lambda bi, i: (bi, (i + 1) * r, 0)),
                  _const_spec((CONV_W, D_SIDE)), _const_spec((1, D_SIDE)), _const_spec((1, D_SIDE)),
                  row(d), _mod_spec(b, nlat), _resident_spec((d, d)),
                  _const_spec((1, d)), _resident_spec(w1.shape), _resident_spec(w2.shape),
                  _const_spec((1, d))],
        out_specs=row(d),
        out_shape=jax.ShapeDtypeStruct((b, t, d), F32),
        scratch_shapes=[pltpu.VMEM((tm + 2 * CONV_HALO, D_SIDE), F32)],
        compiler_params=_cparams("parallel", "parallel"),
    )(o, glu, glu, glu, w_dw, g_ln, b_ln, x, mod, w_out, g2, w1, w2, g_final)


def kernel(x, c, ctx, c_ctx, w_mod, b_mod, g_norm, w_in_even, b_gate, w_qk_conv, g_mlstm_head,
           w_out_even, w_in_odd, lam_p, g_subln, w_dw, g_conv_ln, b_conv_ln, w_out_odd,
           w_ff1, w_ff2, g_final):
    b, t, d = x.shape
    ctx_len = ctx.shape[1]
    assert w_mod.shape[0] == 2 and d == 1024 and ctx_len == ROW_TILE and t % (FFT_N1 * 8) == 0
    nlat = t // ROW_TILE
    mod = _mod_vectors(c, c_ctx, w_mod, b_mod)

    we = w_in_even[0]
    g0 = 4 * D_HEADS
    w_gate = jnp.zeros((d, 256), F32).at[:, 0:12].set(we[:, g0:g0 + 12]).at[:, 128:140].set(we[:, g0 + 12:g0 + 24])
    w_all = jnp.concatenate([we[:, :g0], w_gate, we[:, g0 + 24:]], axis=1).astype(BF16)
    bg = jnp.zeros((1, 256), F32).at[0, 0:12].set(b_gate[0, :12]).at[0, 128:140].set(b_gate[0, 12:])
    q, k, v, o, gates, p = _proj_even(x, ctx, mod[0], g_norm[0, 0][None], w_all, bg, _channel_dft(),
                                      w_qk_conv[0], nlat)
    hf, hb = _mlstm(q, k, v, gates, t)
    y_lat, y_ctx = _fourier(p, t, ctx_len)
    x2 = _even_finish(hf, hb, o, y_lat, y_ctx, x, ctx, mod[0], g_mlstm_head[0][None],
                      w_out_even[0].astype(BF16), g_norm[0, 1][None], w_ff1[0].astype(BF16),
                      w_ff2[0].astype(BF16), nlat)

    lam_init = 0.8 - 0.6 * math.exp(-0.3 * 1)
    cos, sin = _rope_tables(t, ctx_len)
    qa, ka, va, glu = _proj_odd(x2, mod[1], g_norm[1, 0][None], w_in_odd[0].astype(BF16), cos, sin, nlat)
    oa = _diff_attention(qa, ka, va, lam_p[0], g_subln[0][None], t, lam_init)
    return _odd_finish(oa, glu, w_dw[0], g_conv_ln[0][None], b_conv_ln[0][None], x2, mod[1],
                       w_out_odd[0].astype(BF16), g_norm[1, 1][None], w_ff1[1].astype(BF16),
                       w_ff2[1].astype(BF16), g_final[None], t)
```

```python
import functools
import math

import numpy as np
import jax
import jax.numpy as jnp
from jax import lax
from jax.experimental import pallas as pl
from jax.experimental.pallas import tpu as pltpu

F32 = jnp.float32
BF16 = jnp.bfloat16

EPS = 1e-6
SUBLANES = 8
HEADS = 6
HEAD_W = 128
D_HEADS = HEADS * HEAD_W
D_SIDE = 256
FNET_GC = 64
QK_W = 2 * D_HEADS
CONV_W = 31
CONV_HALO = 16
GRID_W = 64
ROPE_BASE = 10000.0
DIFF_DH = 64
FFT_N1 = 128
ROW_TILE = 256
LATENT_ROW_TILE = 512
MLSTM_CHUNK = 256
MLSTM_SUM_ROWS = 16
ATTN_TQ = 1024
ATTN_TK = 768
ATTN_SUM_ROWS = 16
NEG = -1e30
Q_SCALE_LOG2 = (DIFF_DH ** -0.5) * math.log2(math.e)
VMEM_LIMIT = 56 * 1024 * 1024


def _cparams(*sem, flags=None):
    return pltpu.CompilerParams(dimension_semantics=sem, vmem_limit_bytes=VMEM_LIMIT, flags=flags)


def _dot(a, b):
    return jnp.dot(a, b, preferred_element_type=F32)


def _dot_nt(a, b):
    return lax.dot_general(a, b, (((1,), (1,)), ((), ())), preferred_element_type=F32)


def _dot_tn(a, b):
    return lax.dot_general(a, b, (((0,), (0,)), ((), ())), preferred_element_type=F32)


def _sigmoid(x):
    return 1.0 / (1.0 + jnp.exp(-x))


def _rms(x, g):
    return x * lax.rsqrt(jnp.mean(x * x, axis=-1, keepdims=True) + EPS) * g


def _norm_mod(x, g, shift, scale):
    return _rms(x, g) * (1.0 + scale) + shift


def _mod_kernel(s_ref, w_ref, b_ref, o_ref):
    s = s_ref[...]
    s = s * _sigmoid(s)
    o_ref[0] = _dot(s.astype(BF16), w_ref[0].astype(BF16)) + b_ref[0]


def _mod_vectors(c, c_ctx, w_mod, b_mod):
    depth, d, n = w_mod.shape
    b = c.shape[0]
    s = jnp.zeros((8, d), F32).at[:b].set(c).at[b].set(c_ctx)
    tn = 1536
    out = pl.pallas_call(
        _mod_kernel, name="adaln_mod",
        grid=(depth, n // tn),
        in_specs=[pl.BlockSpec((8, d), lambda l, j: (0, 0)),
                  pl.BlockSpec((1, d, tn), lambda l, j: (l, 0, j)),
                  pl.BlockSpec((1, 1, tn), lambda l, j: (l, 0, j))],
        out_specs=pl.BlockSpec((1, 8, tn), lambda l, j: (l, 0, j)),
        out_shape=jax.ShapeDtypeStruct((depth, 8, n), F32),
        compiler_params=_cparams("parallel", "parallel"),
    )(s, w_mod, b_mod.reshape(depth, 1, n))
    return out.reshape(depth, 8, 6, d)


def _mod_spec(b, nlat):
    return pl.BlockSpec((1, 6, 1024), lambda bi, i: (jnp.where(i >= nlat, b, bi), 0, 0))


def _const_spec(shape):
    nd = len(shape)
    return pl.BlockSpec(shape, lambda *_: (0,) * nd)


def _tile_rows(x_ref, ctx_ref, nlat):
    return jnp.where(pl.program_id(1) >= nlat, ctx_ref[0], x_ref[0])


def _lat_ctx_specs(d, tm, nlat):
    return [pl.BlockSpec((1, tm, d), lambda bi, i: (bi, jnp.minimum(i, nlat - 1), 0)),
            pl.BlockSpec((1, tm, d), lambda bi, i: (bi, jnp.maximum(i - nlat, 0), 0))]


def _proj_even_kernel(x_ref, ctx_ref, xp_ref, xn_ref, mod_ref, g_ref, w_ref, bg_ref, dft_ref, wc_ref,
                      q_ref, k_ref, vt_ref, o_ref, gt_ref, p_ref, *, tm, nlat):
    i = pl.program_id(1)
    m = mod_ref[0]
    g = g_ref[...]
    h = _norm_mod(_tile_rows(x_ref, ctx_ref, nlat), g, m[0:1], m[1:2]).astype(BF16)
    xq = _dot(h, w_ref[:, :QK_W])
    u = _dot(h, w_ref[:, QK_W:])
    halo = jnp.concatenate([xp_ref[0], xn_ref[0]], axis=0)
    uh = _dot(_norm_mod(halo, g, m[0:1], m[1:2]).astype(BF16), w_ref[:, :QK_W])
    prow = jnp.where(jnp.logical_or(i == 0, i >= nlat), 0.0, uh[SUBLANES - 1:SUBLANES])
    nrow = jnp.where(i >= nlat - 1, 0.0, uh[SUBLANES:SUBLANES + 1])
    rid = lax.broadcasted_iota(jnp.int32, xq.shape, 0)
    xm = jnp.where(rid == 0, prow, pltpu.roll(xq, 1, axis=0))
    xp = jnp.where(rid == tm - 1, nrow, pltpu.roll(xq, tm - 1, axis=0))
    wc = wc_ref[...]
    y = xm * wc[0:1] + xq * wc[1:2] + xp * wc[2:3]
    y = y * _sigmoid(y)
    q_ref[0] = (y[:, :D_HEADS] * (HEAD_W ** -0.5)).astype(BF16)
    k_ref[0] = y[:, D_HEADS:].astype(BF16)
    vt_ref[0] = u[:, :D_HEADS].T.astype(BF16)
    o_ref[0] = u[:, D_HEADS:2 * D_HEADS]
    c0 = 2 * D_HEADS
    gt_ref[0] = u[:, c0:c0 + 256] + bg_ref[...]
    f = u[:, c0 + 256:c0 + 512].astype(BF16)
    p_ref[0] = _dot(f, dft_ref[...].astype(BF16))


def _proj_even(x, ctx, mod, g, w_all, bg, dftc, w_conv, nlat):
    b, t, d = x.shape
    ttot = t + ctx.shape[1]
    tm = ROW_TILE
    nt = ttot // tm
    r8 = tm // SUBLANES
    last8 = t // SUBLANES - 1
    row = lambda w: pl.BlockSpec((1, tm, w), lambda bi, i: (bi, i, 0))
    halo = lambda f: pl.BlockSpec((1, SUBLANES, d), lambda bi, i: (bi, jnp.clip(f(i), 0, last8), 0))
    return pl.pallas_call(
        functools.partial(_proj_even_kernel, tm=tm, nlat=nlat), name="proj_even",
        grid=(b, nt),
        in_specs=_lat_ctx_specs(d, tm, nlat)
        + [halo(lambda i: i * r8 - 1), halo(lambda i: (i + 1) * r8),
           _mod_spec(b, nlat), _const_spec((1, d)), _const_spec(w_all.shape), _const_spec((1, 256)),
           _const_spec(dftc.shape), _const_spec((3, QK_W))],
        out_specs=[row(D_HEADS), row(D_HEADS), pl.BlockSpec((1, D_HEADS, tm), lambda bi, i: (bi, 0, i)),
                   row(D_HEADS), row(256), row(512)],
        out_shape=[jax.ShapeDtypeStruct((b, ttot, D_HEADS), BF16),
                   jax.ShapeDtypeStruct((b, ttot, D_HEADS), BF16),
                   jax.ShapeDtypeStruct((b, D_HEADS, ttot), BF16),
                   jax.ShapeDtypeStruct((b, ttot, D_HEADS), F32),
                   jax.ShapeDtypeStruct((b, ttot, 256), F32),
                   jax.ShapeDtypeStruct((b, ttot, 512), F32)],
        compiler_params=_cparams("parallel", "parallel"),
    )(x, ctx, x, x, mod, g, w_all, bg, dftc, w_conv)


def _mlstm_kernel(qf_ref, kf_ref, vtf_ref, gf_ref, qb_ref, kb_ref, vtb_ref, gb_ref, hf_ref, hb_ref,
                  c_scr, m_scr, *, chunk):
    c = pl.program_id(1)

    @pl.when(c == 0)
    def _():
        c_scr[...] = jnp.zeros_like(c_scr)
        m_scr[...] = jnp.zeros_like(m_scr)

    row = lax.broadcasted_iota(jnp.int32, (chunk, chunk), 0)
    col = lax.broadcasted_iota(jnp.int32, (chunk, chunk), 1)
    lower, upper = col <= row, col >= row
    _mlstm_direction(qf_ref, kf_ref, vtf_ref, gf_ref, hf_ref, c_scr.at[0], m_scr.at[0], lower, upper, chunk)
    _mlstm_direction(qb_ref, kb_ref, vtb_ref, gb_ref, hb_ref, c_scr.at[1], m_scr.at[1], upper, lower, chunk)


def _mlstm_direction(q_ref, k_ref, vt_ref, g_ref, h_ref, c_scr, m_scr, before, mask_t, chunk):
    gates = g_ref[0]
    logf = jnp.minimum(gates, 0.0) - jnp.log(1.0 + jnp.exp(-jnp.abs(gates)))
    csum = before.astype(BF16)
    f_hi = logf.astype(BF16)
    f_mid = (logf - f_hi.astype(F32)).astype(BF16)
    f_lo = (logf - f_hi.astype(F32) - f_mid.astype(F32)).astype(BF16)
    a_all = _dot(csum, f_hi) + _dot(csum, f_mid) + _dot(csum, f_lo)
    r_all = gates - pltpu.roll(a_all, 128 - HEADS, axis=1)
    lane = lax.broadcasted_iota(jnp.int32, (chunk, 128), 1)
    rows = jnp.where(lane < HEADS, r_all, a_all).T
    ones = jnp.ones((MLSTM_SUM_ROWS, chunk), BF16)

    for h in range(HEADS):
        sl = slice(h * HEAD_W, (h + 1) * HEAD_W)
        qh = q_ref[0, :, sl]
        kh = k_ref[0, :, sl]
        vext = jnp.concatenate([vt_ref[0, sl, :], ones], axis=0)
        m_prev = m_scr[h:h + 1, 0:1]
        r_row = rows[h:h + 1, :]
        a_row = rows[HEADS + h:HEADS + h + 1, :]
        rm = jnp.where(mask_t, r_all[:, h:h + 1], NEG)
        mx = jnp.maximum(jnp.max(rm, axis=0, keepdims=True), m_prev)
        st = _dot_nt(kh, qh) * jnp.exp(rm - mx)
        ce = c_scr[h]
        nd = _dot(vext, st.astype(BF16)) + jnp.exp(m_prev - mx) * _dot_nt(ce.astype(BF16), qh)
        den = jnp.maximum(jnp.abs(nd[HEAD_W:HEAD_W + 1]), jnp.exp(-(a_row + mx)))
        h_ref[0, sl, :] = (nd[:HEAD_W] / den).astype(BF16)
        mx_last = jnp.max(mx, axis=1, keepdims=True)
        a_last = jnp.min(a_row, axis=1, keepdims=True)
        wv = (vext.astype(F32) * jnp.exp(r_row - mx_last)).astype(BF16)
        c_scr[h] = jnp.exp(m_prev - mx_last) * ce + _dot(wv, kh)
        m_scr[h:h + 1, :] = jnp.broadcast_to(a_last + mx_last, (1, 128))


def _mlstm(q, k, vt, gates, nlat_rows):
    b, ttot, _ = q.shape
    L = MLSTM_CHUNK
    nc = ttot // L
    ncl = nlat_rows // L
    ncc = nc - ncl

    def fwd(c):
        return jnp.where(c < ncc, ncl + c, c - ncc)

    def bwd(c):
        return jnp.where(c < ncc, nc - 1 - c, ncl - 1 - (c - ncc))

    def specs(blk, d):
        head = pl.BlockSpec((1, L, D_HEADS), lambda bi, c: (bi, blk(c), 0))
        return [head, head, pl.BlockSpec((1, D_HEADS, L), lambda bi, c: (bi, 0, blk(c))),
                pl.BlockSpec((1, L, 128), lambda bi, c: (bi, blk(c), d))]

    out = lambda blk: pl.BlockSpec((1, D_HEADS, L), lambda bi, c: (bi, 0, blk(c)))
    return pl.pallas_call(
        functools.partial(_mlstm_kernel, chunk=L), name="mlstm_scan",
        grid=(b, nc),
        in_specs=specs(fwd, 0) + specs(bwd, 1),
        out_specs=[out(fwd), out(bwd)],
        out_shape=[jax.ShapeDtypeStruct((b, D_HEADS, ttot), BF16)] * 2,
        scratch_shapes=[pltpu.VMEM((2, HEADS, HEAD_W + MLSTM_SUM_ROWS, HEAD_W), F32),
                        pltpu.VMEM((2, 8, 128), F32)],
        compiler_params=_cparams("parallel", "arbitrary"),
    )(q, k, vt, gates, q, k, vt, gates)


def _fft1_kernel(p_ref, g_ref, o_ref):
    for r in range(SUBLANES):
        x = p_ref[0, :, 0, r, :].astype(BF16)
        bb = _dot(g_ref[r].astype(BF16), jnp.concatenate([x[:, :256], x[:, 256:]], axis=0))
        o_ref[0, 0, :, r, :] = bb[:FFT_N1]
        o_ref[0, 1, :, r, :] = bb[FFT_N1:]


def _fft2_kernel(b_ref, t_ref, o_ref, *, kb, scale):
    tc = t_ref[0].astype(BF16)
    ts = t_ref[1].astype(BF16)
    for j in range(kb):
        y = _dot(tc, b_ref[0, 0, j].astype(BF16)) + _dot(ts, b_ref[0, 1, j].astype(BF16))
        o_ref[0, :, j, :] = y * scale


def _dft_ctx_kernel(p_ref, t_ref, o_ref, *, scale):
    p = p_ref[0].astype(BF16)
    st = jnp.concatenate([p[:, :256], p[:, 256:]], axis=0)
    o_ref[0] = _dot(t_ref[...].astype(BF16), st) * scale


def _fft_tables(t):
    n1, n2 = FFT_N1, t // FFT_N1
    k1 = np.arange(n1, dtype=np.int64)[None, :, None]
    nn = (n2 * np.arange(n1, dtype=np.int64)[None, None, :] + np.arange(n2, dtype=np.int64)[:, None, None])
    ang = 2.0 * np.pi * ((k1 * nn) % t).astype(np.float64) / t
    gr, gi = np.cos(ang), -np.sin(ang)
    g = np.concatenate([np.concatenate([gr, -gi], axis=2), np.concatenate([gi, gr], axis=2)], axis=1)
    a2 = 2.0 * np.pi * ((np.arange(n2)[:, None] * np.arange(n2)[None, :]) % n2) / n2
    t2 = np.stack([np.cos(a2), np.sin(a2)])
    return jnp.asarray(g, F32), jnp.asarray(t2, F32)


def _dft_matrix_cs(n):
    a = 2.0 * np.pi * ((np.arange(n)[:, None] * np.arange(n)[None, :]) % n) / n
    return np.cos(a), np.sin(a)


def _channel_dft():
    c, s = _dft_matrix_cs(FNET_GC)
    eye = np.eye(D_SIDE // FNET_GC)
    return jnp.asarray(np.concatenate([np.kron(eye, c), -np.kron(eye, s)], axis=1), F32)


def _fourier(p, t, ctx_len):
    b = p.shape[0]
    n1, n2 = FFT_N1, t // FFT_N1
    g, t2 = _fft_tables(t)
    kb = 16
    ttot = p.shape[1]
    p5 = p.reshape(b, ttot // n2, n2 // SUBLANES, SUBLANES, 512)
    st1 = pl.pallas_call(
        _fft1_kernel, name="fft_stage1",
        grid=(b, n2 // SUBLANES),
        in_specs=[pl.BlockSpec((1, n1, 1, SUBLANES, 512), lambda bi, j: (bi, 0, j, 0, 0)),
                  pl.BlockSpec((SUBLANES, 256, 256), lambda bi, j: (j, 0, 0))],
        out_specs=pl.BlockSpec((1, 2, n1, SUBLANES, 256), lambda bi, j: (bi, 0, 0, j, 0)),
        out_shape=jax.ShapeDtypeStruct((b, 2, n1, n2, 256), F32),
        compiler_params=_cparams("parallel", "parallel"),
    )(p5, g)
    y = pl.pallas_call(
        functools.partial(_fft2_kernel, kb=kb, scale=1.0 / math.sqrt(t * FNET_GC)), name="fft_stage2",
        grid=(b, n1 // kb),
        in_specs=[pl.BlockSpec((1, 2, kb, n2, 256), lambda bi, j: (bi, 0, j, 0, 0)),
                  _const_spec((2, n2, n2))],
        out_specs=pl.BlockSpec((1, n2, kb, 256), lambda bi, j: (bi, 0, j, 0)),
        out_shape=jax.ShapeDtypeStruct((b, n2, n1, 256), F32),
        compiler_params=_cparams("parallel", "parallel"),
    )(st1, t2)
    y_lat = y.reshape(b, t, 256)
    cc, sc = _dft_matrix_cs(ctx_len)
    tc = jnp.asarray(np.concatenate([cc, sc], axis=1), F32)
    y_ctx = pl.pallas_call(
        functools.partial(_dft_ctx_kernel, scale=1.0 / math.sqrt(ctx_len * FNET_GC)), name="dft_ctx",
        grid=(b,),
        in_specs=[pl.BlockSpec((1, ctx_len, 512), lambda bi: (bi, t // ctx_len, 0)),
                  _const_spec((ctx_len, 2 * ctx_len))],
        out_specs=pl.BlockSpec((1, ctx_len, 256), lambda bi: (bi, 0, 0)),
        out_shape=jax.ShapeDtypeStruct((b, ctx_len, 256), F32),
        compiler_params=_cparams("parallel"),
    )(p, tc)
    return y_lat, y_ctx


def _even_finish_kernel(hf_ref, hb_ref, o_ref, yl_ref, yc_ref, x_ref, ctx_ref, mod_ref, gh_ref, w_ref,
                        g2_ref, w1_ref, w2_ref, out_ref, *, nlat):
    i = pl.program_id(1)
    hs = (hf_ref[0].astype(F32) + hb_ref[0].astype(F32)).T
    gate = _sigmoid(o_ref[0])
    gh = gh_ref[...]
    parts = []
    for h in range(HEADS):
        sl = slice(h * HEAD_W, (h + 1) * HEAD_W)
        parts.append((_rms(hs[:, sl], gh[:, sl]) * gate[:, sl]).astype(BF16))
    y = jnp.where(i >= nlat, yc_ref[0], yl_ref[0])
    parts.append(y.astype(BF16))
    out = _dot(jnp.concatenate(parts, axis=1), w_ref[...])
    m = mod_ref[0]
    x1 = _tile_rows(x_ref, ctx_ref, nlat) + m[2:3] * out
    out_ref[0] = _mlp_rows(x1, m, g2_ref[...], w1_ref, w2_ref)


def _mlp_rows(x, m, g, w1_ref, w2_ref):
    h = _norm_mod(x, g, m[3:4], m[4:5]).astype(BF16)
    a = jnp.maximum(_dot(h, w1_ref[...]), 0.0)
    return x + m[5:6] * _dot((a * a).astype(BF16), w2_ref[...])


def _resident_spec(shape):
    nd = len(shape)
    return pl.BlockSpec(shape, lambda *_: (0,) * nd, pipeline_mode=pl.Buffered(1))


def _even_finish(hf, hb, o, y_lat, y_ctx, x, ctx, mod, g_head, w_out, g2, w1, w2, nlat):
    b, t, d = x.shape
    ttot = t + ctx.shape[1]
    tm = ROW_TILE
    nt = ttot // tm
    row = lambda w: pl.BlockSpec((1, tm, w), lambda bi, i: (bi, i, 0))
    return pl.pallas_call(
        functools.partial(_even_finish_kernel, nlat=nlat), name="even_finish_mlp",
        grid=(b, nt),
        in_specs=[pl.BlockSpec((1, D_HEADS, tm), lambda bi, i: (bi, 0, i)),
                  pl.BlockSpec((1, D_HEADS, tm), lambda bi, i: (bi, 0, i)),
                  row(D_HEADS),
                  pl.BlockSpec((1, tm, 256), lambda bi, i: (bi, jnp.minimum(i, nlat - 1), 0)),
                  pl.BlockSpec((1, tm, 256), lambda bi, i: (bi, jnp.maximum(i - nlat, 0), 0))]
        + _lat_ctx_specs(d, tm, nlat)
        + [_mod_spec(b, nlat), _const_spec((1, D_HEADS)), _resident_spec((d, d)),
           _const_spec((1, d)), _resident_spec(w1.shape), _resident_spec(w2.shape)],
        out_specs=row(d),
        out_shape=jax.ShapeDtypeStruct((b, ttot, d), F32),
        compiler_params=_cparams("parallel", "parallel"),
    )(hf, hb, o, y_lat, y_ctx, x, ctx, mod, g_head, w_out, g2, w1, w2)


def _swap_halves(x):
    lane = lax.broadcasted_iota(jnp.int32, x.shape, 1)
    return jnp.where(lane % 64 < 32, pltpu.roll(x, 96, axis=1), pltpu.roll(x, 32, axis=1))


def _proj_odd_kernel(x_ref, mod_ref, g_ref, w_ref, cos_ref, sin_ref, q_ref, k_ref, vt_ref, glu_ref):
    m = mod_ref[0]
    h = _norm_mod(x_ref[0], g_ref[...], m[0:1], m[1:2]).astype(BF16)
    u = _dot(h, w_ref[...])
    cos = cos_ref[...]
    sin = sin_ref[...]
    for hh in range(HEADS):
        sl = slice(hh * HEAD_W, (hh + 1) * HEAD_W)
        qh = u[:, sl]
        kh = u[:, D_HEADS + hh * HEAD_W:D_HEADS + (hh + 1) * HEAD_W]
        q_ref[0, :, sl] = ((qh * cos + _swap_halves(qh) * sin) * Q_SCALE_LOG2).astype(BF16)
        k_ref[0, :, sl] = (kh * cos + _swap_halves(kh) * sin).astype(BF16)
    vt_ref[0] = u[:, 2 * D_HEADS:3 * D_HEADS].T.astype(BF16)
    a = u[:, 3 * D_HEADS:3 * D_HEADS + D_SIDE]
    gte = u[:, 3 * D_HEADS + D_SIDE:]
    glu_ref[0] = a * _sigmoid(gte)


def _rope_tables(t, ctx_len):
    rows = t // GRID_W
    row = np.repeat(np.arange(rows, dtype=np.float64), GRID_W)
    col = np.tile(np.arange(GRID_W, dtype=np.float64), rows)
    n_freq = DIFF_DH // 4
    inv = ROPE_BASE ** (-np.arange(n_freq, dtype=np.float64) / n_freq)
    ang = np.concatenate([row[:, None] * inv, col[:, None] * inv], axis=-1)
    cos, sin = np.cos(ang), np.sin(ang)
    cos = np.concatenate([cos, cos, cos, cos], axis=-1)
    sin = np.concatenate([-sin, sin, -sin, sin], axis=-1)
    cos = np.concatenate([cos, np.ones((ctx_len, 128))], axis=0)
    sin = np.concatenate([sin, np.zeros((ctx_len, 128))], axis=0)
    return jnp.asarray(cos, F32), jnp.asarray(sin, F32)


def _proj_odd(x, mod, g, w_all, cos, sin, nlat):
    b, ttot, d = x.shape
    tm = ROW_TILE
    nt = ttot // tm
    row = lambda w: pl.BlockSpec((1, tm, w), lambda bi, i: (bi, i, 0))
    tab = pl.BlockSpec((tm, 128), lambda bi, i: (i, 0))
    return pl.pallas_call(
        _proj_odd_kernel, name="proj_odd",
        grid=(b, nt),
        in_specs=[row(d), _mod_spec(b, nlat), _const_spec((1, d)), _const_spec(w_all.shape), tab, tab],
        out_specs=[row(D_HEADS), row(D_HEADS),
                   pl.BlockSpec((1, D_HEADS, tm), lambda bi, i: (bi, 0, i)), row(D_SIDE)],
        out_shape=[jax.ShapeDtypeStruct((b, ttot, D_HEADS), BF16)] * 2
        + [jax.ShapeDtypeStruct((b, D_HEADS, ttot), BF16), jax.ShapeDtypeStruct((b, ttot, D_SIDE), F32)],
        compiler_params=_cparams("parallel", "parallel"),
    )(x, mod, g, w_all, cos, sin)


def _attn_kernel(q_ref, k_ref, vt_ref, lam_ref, gs_ref, o_ref, acc0, acc1, s_a, s_b, *, tk, nk, lam_init):
    q = q_ref[0]
    q0 = q[:, :DIFF_DH]
    q1 = q[:, DIFF_DH:]
    tq = q.shape[0]
    ones = jnp.ones((ATTN_SUM_ROWS, tk), BF16)
    acc0[...] = jnp.zeros_like(acc0)
    acc1[...] = jnp.zeros_like(acc1)

    def online(st, m_old, acc, vext):
        m_new = jnp.maximum(m_old, jnp.max(st, axis=0, keepdims=True))
        p = jnp.exp2(st - m_new).astype(BF16)
        acc[...] = jnp.exp2(m_old - m_new) * acc[...] + _dot(vext, p)
        return m_new

    def scores(j, s_buf):
        kk = k_ref[0, j * tk:(j + 1) * tk, :]
        s_buf[0] = _dot_nt(kk[:, :DIFF_DH], q0)
        s_buf[1] = _dot_nt(kk[:, DIFF_DH:], q1)

    def consume(j, s_buf, m0, m1):
        vext = jnp.concatenate([vt_ref[0, :, j * tk:(j + 1) * tk], ones], axis=0)
        return online(s_buf[0], m0, acc0, vext), online(s_buf[1], m1, acc1, vext)

    bufs = (s_a, s_b)
    init = jnp.full((1, tq), NEG, F32)
    carry = (init, init)
    scores(0, bufs[0])
    for j in range(nk):
        if j + 1 < nk:
            scores(j + 1, bufs[(j + 1) % 2])
        carry = consume(j, bufs[j % 2], *carry)
    lp = lam_ref[...]
    lam = (jnp.exp(jnp.sum(lp[0:1] * lp[1:2], axis=1, keepdims=True))
           - jnp.exp(jnp.sum(lp[2:3] * lp[3:4], axis=1, keepdims=True)) + lam_init)
    a0 = acc0[...]
    a1 = acc1[...]
    o = a0[:HEAD_W] / a0[HEAD_W:HEAD_W + 1] - lam * (a1[:HEAD_W] / a1[HEAD_W:HEAD_W + 1])
    o = o * lax.rsqrt(jnp.mean(o * o, axis=0, keepdims=True) + EPS)
    o_ref[0] = (o.T * gs_ref[...] * (1.0 - lam_init)).astype(BF16)


def _diff_attention(q, k, v, lam_p, g_sub, t, lam_init):
    b, ttot, _ = q.shape
    tq = ATTN_TQ
    tk = max(n for n in range(256, ATTN_TK + 1, 256) if ttot % n == 0)
    return pl.pallas_call(
        functools.partial(_attn_kernel, tk=tk, nk=ttot // tk, lam_init=lam_init), name="diff_attn",
        grid=(b, HEADS, t // tq),
        in_specs=[pl.BlockSpec((1, tq, HEAD_W), lambda bi, h, i: (bi, i, h)),
                  pl.BlockSpec((1, ttot, HEAD_W), lambda bi, h, i: (bi, 0, h)),
                  pl.BlockSpec((1, HEAD_W, ttot), lambda bi, h, i: (bi, h, 0)),
                  _const_spec(lam_p.shape), _const_spec((1, HEAD_W))],
        out_specs=pl.BlockSpec((1, tq, HEAD_W), lambda bi, h, i: (bi, i, h)),
        out_shape=jax.ShapeDtypeStruct((b, t, D_HEADS), BF16),
        scratch_shapes=[pltpu.VMEM((HEAD_W + ATTN_SUM_ROWS, tq), F32)] * 2
        + [pltpu.VMEM((2, tk, tq), F32)] * 2,
        compiler_params=_cparams("parallel", "parallel", "parallel"),
    )(q, k, v, lam_p, g_sub)


def _odd_finish_kernel(o_ref, glu_ref, prev_ref, next_ref, wdw_ref, gln_ref, bln_ref, x_ref, mod_ref,
                       w_ref, g2_ref, w1_ref, w2_ref, gf_ref, out_ref, xs, *, tm, nlat):
    i = pl.program_id(1)
    hal = CONV_HALO
    xs[0:hal, :] = jnp.where(i == 0, 0.0, prev_ref[0])
    xs[hal:hal + tm, :] = glu_ref[0]
    xs[hal + tm:, :] = jnp.where(i == nlat - 1, 0.0, next_ref[0])
    wdw = wdw_ref[...]
    z = jnp.zeros((tm, D_SIDE), F32)
    off = hal - (CONV_W - 1) // 2
    for r in range(SUBLANES):
        u = None
        for a in range((off + CONV_W - 1) // SUBLANES + 1):
            kk = SUBLANES * a + r - off
            if 0 <= kk < CONV_W:
                term = xs[SUBLANES * a:SUBLANES * a + tm + SUBLANES, :] * wdw[kk:kk + 1]
                u = term if u is None else u + term
        z = z + u[r:r + tm]
    mu = jnp.mean(z, axis=-1, keepdims=True)
    zc = z - mu
    var = jnp.mean(zc * zc, axis=-1, keepdims=True)
    z = zc * lax.rsqrt(var + EPS) * gln_ref[...] + bln_ref[...]
    z = z * _sigmoid(z)
    cat = jnp.concatenate([o_ref[0], z.astype(BF16)], axis=1)
    m = mod_ref[0]
    x3 = x_ref[0] + m[2:3] * _dot(cat, w_ref[...])
    out_ref[0] = _rms(_mlp_rows(x3, m, g2_ref[...], w1_ref, w2_ref), gf_ref[...])


def _odd_finish(o, glu, w_dw, g_ln, b_ln, x, mod, w_out, g2, w1, w2, g_final, t):
    b, _, d = x.shape
    tm = LATENT_ROW_TILE
    nlat = t // tm
    r = tm // CONV_HALO
    row = lambda w: pl.BlockSpec((1, tm, w), lambda bi, i: (bi, i, 0))
    return pl.pallas_call(
        functools.partial(_odd_finish_kernel, tm=tm, nlat=nlat), name="odd_finish_mlp",
        grid=(b, nlat),
        in_specs=[row(D_HEADS), row(D_SIDE),
                  pl.BlockSpec((1, CONV_HALO, D_SIDE), lambda bi, i: (bi, jnp.maximum(i * r - 1, 0), 0)),
                  pl.BlockSpec((1, CONV_HALO, D_SIDE), lambda bi, i: (bi, (i + 1) * r, 0)),
                  _const_spec((CONV_W, D_SIDE)), _const_spec((1, D_SIDE)), _const_spec((1, D_SIDE)),
                  row(d), _mod_spec(b, nlat), _resident_spec((d, d)),
                  _const_spec((1, d)), _resident_spec(w1.shape), _resident_spec(w2.shape),
                  _const_spec((1, d))],
        out_specs=row(d),
        out_shape=jax.ShapeDtypeStruct((b, t, d), F32),
        scratch_shapes=[pltpu.VMEM((tm + 2 * CONV_HALO, D_SIDE), F32)],
        compiler_params=_cparams("parallel", "parallel"),
    )(o, glu, glu, glu, w_dw, g_ln, b_ln, x, mod, w_out, g2, w1, w2, g_final)


def kernel(x, c, ctx, c_ctx, w_mod, b_mod, g_norm, w_in_even, b_gate, w_qk_conv, g_mlstm_head,
           w_out_even, w_in_odd, lam_p, g_subln, w_dw, g_conv_ln, b_conv_ln, w_out_odd,
           w_ff1, w_ff2, g_final):
    b, t, d = x.shape
    ctx_len = ctx.shape[1]
    assert w_mod.shape[0] == 2 and d == 1024 and ctx_len == ROW_TILE and t % (FFT_N1 * 8) == 0
    nlat = t // ROW_TILE
    mod = _mod_vectors(c, c_ctx, w_mod, b_mod)

    we = w_in_even[0]
    g0 = 4 * D_HEADS
    w_gate = jnp.zeros((d, 256), F32).at[:, 0:12].set(we[:, g0:g0 + 12]).at[:, 128:140].set(we[:, g0 + 12:g0 + 24])
    w_all = jnp.concatenate([we[:, :g0], w_gate, we[:, g0 + 24:]], axis=1).astype(BF16)
    bg = jnp.zeros((1, 256), F32).at[0, 0:12].set(b_gate[0, :12]).at[0, 128:140].set(b_gate[0, 12:])
    q, k, v, o, gates, p = _proj_even(x, ctx, mod[0], g_norm[0, 0][None], w_all, bg, _channel_dft(),
                                      w_qk_conv[0], nlat)
    hf, hb = _mlstm(q, k, v, gates, t)
    y_lat, y_ctx = _fourier(p, t, ctx_len)
    x2 = _even_finish(hf, hb, o, y_lat, y_ctx, x, ctx, mod[0], g_mlstm_head[0][None],
                      w_out_even[0].astype(BF16), g_norm[0, 1][None], w_ff1[0].astype(BF16),
                      w_ff2[0].astype(BF16), nlat)

    lam_init = 0.8 - 0.6 * math.exp(-0.3 * 1)
    cos, sin = _rope_tables(t, ctx_len)
    qa, ka, va, glu = _proj_odd(x2, mod[1], g_norm[1, 0][None], w_in_odd[0].astype(BF16), cos, sin, nlat)
    oa = _diff_attention(qa, ka, va, lam_p[0], g_subln[0][None], t, lam_init)
    return _odd_finish(oa, glu, w_dw[0], g_conv_ln[0][None], b_conv_ln[0][None], x2, mod[1],
                       w_out_odd[0].astype(BF16), g_norm[1, 1][None], w_ff1[1].astype(BF16),
                       w_ff2[1].astype(BF16), g_final[None], t)
```

```python
import functools
import math

import numpy as np
import jax
import jax.numpy as jnp
from jax import lax
from jax.experimental import pallas as pl
from jax.experimental.pallas import tpu as pltpu

F32 = jnp.float32
BF16 = jnp.bfloat16

EPS = 1e-6
SUBLANES = 8
HEADS = 6
HEAD_W = 128
D_HEADS = HEADS * HEAD_W
D_SIDE = 256
FNET_GC = 64
QK_W = 2 * D_HEADS
CONV_W = 31
CONV_HALO = 16
GRID_W = 64
ROPE_BASE = 10000.0
DIFF_DH = 64
FFT_N1 = 128
ROW_TILE = 256
LATENT_ROW_TILE = 512
MLSTM_CHUNK = 256
MLSTM_SUM_ROWS = 16
ATTN_TQ = 1024
ATTN_TK = 768
ATTN_SCORE_BUFS = 2
ATTN_SUM_ROWS = 16
NEG = -1e30
Q_SCALE_LOG2 = (DIFF_DH ** -0.5) * math.log2(math.e)
VMEM_LIMIT = 56 * 1024 * 1024


def _cparams(*sem, flags=None):
    return pltpu.CompilerParams(dimension_semantics=sem, vmem_limit_bytes=VMEM_LIMIT, flags=flags)


def _dot(a, b):
    return jnp.dot(a, b, preferred_element_type=F32)


def _dot_nt(a, b):
    return lax.dot_general(a, b, (((1,), (1,)), ((), ())), preferred_element_type=F32)


def _dot_tn(a, b):
    return lax.dot_general(a, b, (((0,), (0,)), ((), ())), preferred_element_type=F32)


def _sigmoid(x):
    return 1.0 / (1.0 + jnp.exp(-x))


def _rms(x, g):
    return x * lax.rsqrt(jnp.mean(x * x, axis=-1, keepdims=True) + EPS) * g


def _norm_mod(x, g, shift, scale):
    return _rms(x, g) * (1.0 + scale) + shift


def _mod_kernel(s_ref, w_ref, b_ref, o_ref):
    s = s_ref[...]
    s = s * _sigmoid(s)
    o_ref[0] = _dot(s.astype(BF16), w_ref[0].astype(BF16)) + b_ref[0]


def _mod_vectors(c, c_ctx, w_mod, b_mod):
    depth, d, n = w_mod.shape
    b = c.shape[0]
    s = jnp.zeros((8, d), F32).at[:b].set(c).at[b].set(c_ctx)
    tn = 1536
    out = pl.pallas_call(
        _mod_kernel, name="adaln_mod",
        grid=(depth, n // tn),
        in_specs=[pl.BlockSpec((8, d), lambda l, j: (0, 0)),
                  pl.BlockSpec((1, d, tn), lambda l, j: (l, 0, j)),
                  pl.BlockSpec((1, 1, tn), lambda l, j: (l, 0, j))],
        out_specs=pl.BlockSpec((1, 8, tn), lambda l, j: (l, 0, j)),
        out_shape=jax.ShapeDtypeStruct((depth, 8, n), F32),
        compiler_params=_cparams("parallel", "parallel"),
    )(s, w_mod, b_mod.reshape(depth, 1, n))
    return out.reshape(depth, 8, 6, d)


def _mod_spec(b, nlat):
    return pl.BlockSpec((1, 6, 1024), lambda bi, i: (jnp.where(i >= nlat, b, bi), 0, 0))


def _const_spec(shape):
    nd = len(shape)
    return pl.BlockSpec(shape, lambda *_: (0,) * nd)


def _tile_rows(x_ref, ctx_ref, nlat):
    return jnp.where(pl.program_id(1) >= nlat, ctx_ref[0], x_ref[0])


def _lat_ctx_specs(d, tm, nlat):
    return [pl.BlockSpec((1, tm, d), lambda bi, i: (bi, jnp.minimum(i, nlat - 1), 0)),
            pl.BlockSpec((1, tm, d), lambda bi, i: (bi, jnp.maximum(i - nlat, 0), 0))]


def _proj_even_kernel(x_ref, ctx_ref, xp_ref, xn_ref, mod_ref, g_ref, w_ref, bg_ref, dft_ref, wc_ref,
                      q_ref, k_ref, vt_ref, o_ref, gt_ref, p_ref, *, tm, nlat):
    i = pl.program_id(1)
    m = mod_ref[0]
    g = g_ref[...]
    h = _norm_mod(_tile_rows(x_ref, ctx_ref, nlat), g, m[0:1], m[1:2]).astype(BF16)
    xq = _dot(h, w_ref[:, :QK_W])
    u = _dot(h, w_ref[:, QK_W:])
    halo = jnp.concatenate([xp_ref[0], xn_ref[0]], axis=0)
    uh = _dot(_norm_mod(halo, g, m[0:1], m[1:2]).astype(BF16), w_ref[:, :QK_W])
    prow = jnp.where(jnp.logical_or(i == 0, i >= nlat), 0.0, uh[SUBLANES - 1:SUBLANES])
    nrow = jnp.where(i >= nlat - 1, 0.0, uh[SUBLANES:SUBLANES + 1])
    rid = lax.broadcasted_iota(jnp.int32, xq.shape, 0)
    xm = jnp.where(rid == 0, prow, pltpu.roll(xq, 1, axis=0))
    xp = jnp.where(rid == tm - 1, nrow, pltpu.roll(xq, tm - 1, axis=0))
    wc = wc_ref[...]
    y = xm * wc[0:1] + xq * wc[1:2] + xp * wc[2:3]
    y = y * _sigmoid(y)
    q_ref[0] = (y[:, :D_HEADS] * (HEAD_W ** -0.5)).astype(BF16)
    k_ref[0] = y[:, D_HEADS:].astype(BF16)
    vt_ref[0] = u[:, :D_HEADS].T.astype(BF16)
    o_ref[0] = u[:, D_HEADS:2 * D_HEADS]
    c0 = 2 * D_HEADS
    gt_ref[0] = u[:, c0:c0 + 256] + bg_ref[...]
    f = u[:, c0 + 256:c0 + 512].astype(BF16)
    p_ref[0] = _dot(f, dft_ref[...].astype(BF16))


def _proj_even(x, ctx, mod, g, w_all, bg, dftc, w_conv, nlat):
    b, t, d = x.shape
    ttot = t + ctx.shape[1]
    tm = ROW_TILE
    nt = ttot // tm
    r8 = tm // SUBLANES
    last8 = t // SUBLANES - 1
    row = lambda w: pl.BlockSpec((1, tm, w), lambda bi, i: (bi, i, 0))
    halo = lambda f: pl.BlockSpec((1, SUBLANES, d), lambda bi, i: (bi, jnp.clip(f(i), 0, last8), 0))
    return pl.pallas_call(
        functools.partial(_proj_even_kernel, tm=tm, nlat=nlat), name="proj_even",
        grid=(b, nt),
        in_specs=_lat_ctx_specs(d, tm, nlat)
        + [halo(lambda i: i * r8 - 1), halo(lambda i: (i + 1) * r8),
           _mod_spec(b, nlat), _const_spec((1, d)), _const_spec(w_all.shape), _const_spec((1, 256)),
           _const_spec(dftc.shape), _const_spec((3, QK_W))],
        out_specs=[row(D_HEADS), row(D_HEADS), pl.BlockSpec((1, D_HEADS, tm), lambda bi, i: (bi, 0, i)),
                   row(D_HEADS), row(256), row(512)],
        out_shape=[jax.ShapeDtypeStruct((b, ttot, D_HEADS), BF16),
                   jax.ShapeDtypeStruct((b, ttot, D_HEADS), BF16),
                   jax.ShapeDtypeStruct((b, D_HEADS, ttot), BF16),
                   jax.ShapeDtypeStruct((b, ttot, D_HEADS), F32),
                   jax.ShapeDtypeStruct((b, ttot, 256), F32),
                   jax.ShapeDtypeStruct((b, ttot, 512), F32)],
        compiler_params=_cparams("parallel", "parallel"),
    )(x, ctx, x, x, mod, g, w_all, bg, dftc, w_conv)


def _mlstm_kernel(qf_ref, kf_ref, vtf_ref, gf_ref, qb_ref, kb_ref, vtb_ref, gb_ref, hf_ref, hb_ref,
                  c_scr, m_scr, *, chunk):
    c = pl.program_id(1)

    @pl.when(c == 0)
    def _():
        c_scr[...] = jnp.zeros_like(c_scr)
        m_scr[...] = jnp.zeros_like(m_scr)

    row = lax.broadcasted_iota(jnp.int32, (chunk, chunk), 0)
    col = lax.broadcasted_iota(jnp.int32, (chunk, chunk), 1)
    lower, upper = col <= row, col >= row
    _mlstm_direction(qf_ref, kf_ref, vtf_ref, gf_ref, hf_ref, c_scr.at[0], m_scr.at[0], lower, upper, chunk)
    _mlstm_direction(qb_ref, kb_ref, vtb_ref, gb_ref, hb_ref, c_scr.at[1], m_scr.at[1], upper, lower, chunk)


def _mlstm_direction(q_ref, k_ref, vt_ref, g_ref, h_ref, c_scr, m_scr, before, mask_t, chunk):
    gates = g_ref[0]
    logf = jnp.minimum(gates, 0.0) - jnp.log(1.0 + jnp.exp(-jnp.abs(gates)))
    csum = before.astype(BF16)
    f_hi = logf.astype(BF16)
    f_mid = (logf - f_hi.astype(F32)).astype(BF16)
    f_lo = (logf - f_hi.astype(F32) - f_mid.astype(F32)).astype(BF16)
    a_all = _dot(csum, f_hi) + _dot(csum, f_mid) + _dot(csum, f_lo)
    r_all = gates - pltpu.roll(a_all, 128 - HEADS, axis=1)
    lane = lax.broadcasted_iota(jnp.int32, (chunk, 128), 1)
    rows = jnp.where(lane < HEADS, r_all, a_all).T
    ones = jnp.ones((MLSTM_SUM_ROWS, chunk), BF16)

    for h in range(HEADS):
        sl = slice(h * HEAD_W, (h + 1) * HEAD_W)
        qh = q_ref[0, :, sl]
        kh = k_ref[0, :, sl]
        vext = jnp.concatenate([vt_ref[0, sl, :], ones], axis=0)
        m_prev = m_scr[h:h + 1, 0:1]
        r_row = rows[h:h + 1, :]
        a_row = rows[HEADS + h:HEADS + h + 1, :]
        rm = jnp.where(mask_t, r_all[:, h:h + 1], NEG)
        mx = jnp.maximum(jnp.max(rm, axis=0, keepdims=True), m_prev)
        st = _dot_nt(kh, qh) * jnp.exp(rm - mx)
        ce = c_scr[h]
        nd = _dot(vext, st.astype(BF16)) + jnp.exp(m_prev - mx) * _dot_nt(ce.astype(BF16), qh)
        den = jnp.maximum(jnp.abs(nd[HEAD_W:HEAD_W + 1]), jnp.exp(-(a_row + mx)))
        h_ref[0, sl, :] = (nd[:HEAD_W] / den).astype(BF16)
        mx_last = jnp.max(mx, axis=1, keepdims=True)
        a_last = jnp.min(a_row, axis=1, keepdims=True)
        wv = (vext.astype(F32) * jnp.exp(r_row - mx_last)).astype(BF16)
        c_scr[h] = jnp.exp(m_prev - mx_last) * ce + _dot(wv, kh)
        m_scr[h:h + 1, :] = jnp.broadcast_to(a_last + mx_last, (1, 128))


def _mlstm(q, k, vt, gates, nlat_rows):
    b, ttot, _ = q.shape
    L = MLSTM_CHUNK
    nc = ttot // L
    ncl = nlat_rows // L
    ncc = nc - ncl

    def fwd(c):
        return jnp.where(c < ncc, ncl + c, c - ncc)

    def bwd(c):
        return jnp.where(c < ncc, nc - 1 - c, ncl - 1 - (c - ncc))

    def specs(blk, d):
        head = pl.BlockSpec((1, L, D_HEADS), lambda bi, c: (bi, blk(c), 0))
        return [head, head, pl.BlockSpec((1, D_HEADS, L), lambda bi, c: (bi, 0, blk(c))),
                pl.BlockSpec((1, L, 128), lambda bi, c: (bi, blk(c), d))]

    out = lambda blk: pl.BlockSpec((1, D_HEADS, L), lambda bi, c: (bi, 0, blk(c)))
    return pl.pallas_call(
        functools.partial(_mlstm_kernel, chunk=L), name="mlstm_scan",
        grid=(b, nc),
        in_specs=specs(fwd, 0) + specs(bwd, 1),
        out_specs=[out(fwd), out(bwd)],
        out_shape=[jax.ShapeDtypeStruct((b, D_HEADS, ttot), BF16)] * 2,
        scratch_shapes=[pltpu.VMEM((2, HEADS, HEAD_W + MLSTM_SUM_ROWS, HEAD_W), F32),
                        pltpu.VMEM((2, 8, 128), F32)],
        compiler_params=_cparams("parallel", "arbitrary"),
    )(q, k, vt, gates, q, k, vt, gates)


def _fft1_kernel(p_ref, g_ref, o_ref):
    xt = jnp.swapaxes(p_ref[0, :, 0, :, :], 0, 1)
    out = []
    for r in range(SUBLANES):
        x = xt[r].astype(BF16)
        out.append(_dot(g_ref[r].astype(BF16), jnp.concatenate([x[:, :256], x[:, 256:]], axis=0)))
    bb = jnp.swapaxes(jnp.stack(out), 0, 1)
    o_ref[0, 0] = bb[:FFT_N1]
    o_ref[0, 1] = bb[FFT_N1:]


def _fft2_kernel(b_ref, t_ref, o_ref, *, kb, scale):
    tc = t_ref[0].astype(BF16)
    ts = t_ref[1].astype(BF16)
    ys = [(_dot(tc, b_ref[0, 0, j].astype(BF16)) + _dot(ts, b_ref[0, 1, j].astype(BF16))) * scale
          for j in range(kb)]
    o_ref[0] = jnp.swapaxes(jnp.stack(ys), 0, 1)


def _dft_ctx_kernel(p_ref, t_ref, o_ref, *, scale):
    p = p_ref[0].astype(BF16)
    st = jnp.concatenate([p[:, :256], p[:, 256:]], axis=0)
    o_ref[0] = _dot(t_ref[...].astype(BF16), st) * scale


def _fft_tables(t):
    n1, n2 = FFT_N1, t // FFT_N1
    k1 = np.arange(n1, dtype=np.int64)[None, :, None]
    nn = (n2 * np.arange(n1, dtype=np.int64)[None, None, :] + np.arange(n2, dtype=np.int64)[:, None, None])
    ang = 2.0 * np.pi * ((k1 * nn) % t).astype(np.float64) / t
    gr, gi = np.cos(ang), -np.sin(ang)
    g = np.concatenate([np.concatenate([gr, -gi], axis=2), np.concatenate([gi, gr], axis=2)], axis=1)
    a2 = 2.0 * np.pi * ((np.arange(n2)[:, None] * np.arange(n2)[None, :]) % n2) / n2
    t2 = np.stack([np.cos(a2), np.sin(a2)])
    return jnp.asarray(g, F32), jnp.asarray(t2, F32)


def _dft_matrix_cs(n):
    a = 2.0 * np.pi * ((np.arange(n)[:, None] * np.arange(n)[None, :]) % n) / n
    return np.cos(a), np.sin(a)


def _channel_dft():
    c, s = _dft_matrix_cs(FNET_GC)
    eye = np.eye(D_SIDE // FNET_GC)
    return jnp.asarray(np.concatenate([np.kron(eye, c), -np.kron(eye, s)], axis=1), F32)


def _fourier(p, t, ctx_len):
    b = p.shape[0]
    n1, n2 = FFT_N1, t // FFT_N1
    g, t2 = _fft_tables(t)
    kb = 16
    ttot = p.shape[1]
    p5 = p.reshape(b, ttot // n2, n2 // SUBLANES, SUBLANES, 512)
    st1 = pl.pallas_call(
        _fft1_kernel, name="fft_stage1",
        grid=(b, n2 // SUBLANES),
        in_specs=[pl.BlockSpec((1, n1, 1, SUBLANES, 512), lambda bi, j: (bi, 0, j, 0, 0)),
                  pl.BlockSpec((SUBLANES, 256, 256), lambda bi, j: (j, 0, 0))],
        out_specs=pl.BlockSpec((1, 2, n1, SUBLANES, 256), lambda bi, j: (bi, 0, 0, j, 0)),
        out_shape=jax.ShapeDtypeStruct((b, 2, n1, n2, 256), F32),
        compiler_params=_cparams("parallel", "parallel"),
    )(p5, g)
    y = pl.pallas_call(
        functools.partial(_fft2_kernel, kb=kb, scale=1.0 / math.sqrt(t * FNET_GC)), name="fft_stage2",
        grid=(b, n1 // kb),
        in_specs=[pl.BlockSpec((1, 2, kb, n2, 256), lambda bi, j: (bi, 0, j, 0, 0)),
                  _const_spec((2, n2, n2))],
        out_specs=pl.BlockSpec((1, n2, kb, 256), lambda bi, j: (bi, 0, j, 0)),
        out_shape=jax.ShapeDtypeStruct((b, n2, n1, 256), F32),
        compiler_params=_cparams("parallel", "parallel"),
    )(st1, t2)
    y_lat = y.reshape(b, t, 256)
    cc, sc = _dft_matrix_cs(ctx_len)
    tc = jnp.asarray(np.concatenate([cc, sc], axis=1), F32)
    y_ctx = pl.pallas_call(
        functools.partial(_dft_ctx_kernel, scale=1.0 / math.sqrt(ctx_len * FNET_GC)), name="dft_ctx",
        grid=(b,),
        in_specs=[pl.BlockSpec((1, ctx_len, 512), lambda bi: (bi, t // ctx_len, 0)),
                  _const_spec((ctx_len, 2 * ctx_len))],
        out_specs=pl.BlockSpec((1, ctx_len, 256), lambda bi: (bi, 0, 0)),
        out_shape=jax.ShapeDtypeStruct((b, ctx_len, 256), F32),
        compiler_params=_cparams("parallel"),
    )(p, tc)
    return y_lat, y_ctx


def _even_finish_kernel(hf_ref, hb_ref, o_ref, yl_ref, yc_ref, x_ref, ctx_ref, mod_ref, gh_ref, w_ref,
                        g2_ref, w1_ref, w2_ref, out_ref, *, nlat):
    i = pl.program_id(1)
    hs = (hf_ref[0].astype(F32) + hb_ref[0].astype(F32)).T
    gate = _sigmoid(o_ref[0])
    gh = gh_ref[...]
    parts = []
    for h in range(HEADS):
        sl = slice(h * HEAD_W, (h + 1) * HEAD_W)
        parts.append((_rms(hs[:, sl], gh[:, sl]) * gate[:, sl]).astype(BF16))
    y = jnp.where(i >= nlat, yc_ref[0], yl_ref[0])
    parts.append(y.astype(BF16))
    out = _dot(jnp.concatenate(parts, axis=1), w_ref[...])
    m = mod_ref[0]
    x1 = _tile_rows(x_ref, ctx_ref, nlat) + m[2:3] * out
    out_ref[0] = _mlp_rows(x1, m, g2_ref[...], w1_ref, w2_ref)


def _mlp_rows(x, m, g, w1_ref, w2_ref):
    h = _norm_mod(x, g, m[3:4], m[4:5]).astype(BF16)
    a = jnp.maximum(_dot(h, w1_ref[...]), 0.0)
    return x + m[5:6] * _dot((a * a).astype(BF16), w2_ref[...])


def _resident_spec(shape):
    nd = len(shape)
    return pl.BlockSpec(shape, lambda *_: (0,) * nd, pipeline_mode=pl.Buffered(1))


def _even_finish(hf, hb, o, y_lat, y_ctx, x, ctx, mod, g_head, w_out, g2, w1, w2, nlat):
    b, t, d = x.shape
    ttot = t + ctx.shape[1]
    tm = ROW_TILE
    nt = ttot // tm
    row = lambda w: pl.BlockSpec((1, tm, w), lambda bi, i: (bi, i, 0))
    return pl.pallas_call(
        functools.partial(_even_finish_kernel, nlat=nlat), name="even_finish_mlp",
        grid=(b, nt),
        in_specs=[pl.BlockSpec((1, D_HEADS, tm), lambda bi, i: (bi, 0, i)),
                  pl.BlockSpec((1, D_HEADS, tm), lambda bi, i: (bi, 0, i)),
                  row(D_HEADS),
                  pl.BlockSpec((1, tm, 256), lambda bi, i: (bi, jnp.minimum(i, nlat - 1), 0)),
                  pl.BlockSpec((1, tm, 256), lambda bi, i: (bi, jnp.maximum(i - nlat, 0), 0))]
        + _lat_ctx_specs(d, tm, nlat)
        + [_mod_spec(b, nlat), _const_spec((1, D_HEADS)), _resident_spec((d, d)),
           _const_spec((1, d)), _resident_spec(w1.shape), _resident_spec(w2.shape)],
        out_specs=row(d),
        out_shape=jax.ShapeDtypeStruct((b, ttot, d), F32),
        compiler_params=_cparams("parallel", "parallel"),
    )(hf, hb, o, y_lat, y_ctx, x, ctx, mod, g_head, w_out, g2, w1, w2)


def _swap_halves(x):
    lane = lax.broadcasted_iota(jnp.int32, x.shape, 1)
    return jnp.where(lane % 64 < 32, pltpu.roll(x, 96, axis=1), pltpu.roll(x, 32, axis=1))


def _proj_odd_kernel(x_ref, mod_ref, g_ref, w_ref, cos_ref, sin_ref, q_ref, k_ref, vt_ref, glu_ref):
    m = mod_ref[0]
    h = _norm_mod(x_ref[0], g_ref[...], m[0:1], m[1:2]).astype(BF16)
    u = _dot(h, w_ref[...])
    cos = cos_ref[...]
    sin = sin_ref[...]
    for hh in range(HEADS):
        sl = slice(hh * HEAD_W, (hh + 1) * HEAD_W)
        qh = u[:, sl]
        kh = u[:, D_HEADS + hh * HEAD_W:D_HEADS + (hh + 1) * HEAD_W]
        q_ref[0, :, sl] = ((qh * cos + _swap_halves(qh) * sin) * Q_SCALE_LOG2).astype(BF16)
        k_ref[0, :, sl] = (kh * cos + _swap_halves(kh) * sin).astype(BF16)
    vt_ref[0] = u[:, 2 * D_HEADS:3 * D_HEADS].T.astype(BF16)
    a = u[:, 3 * D_HEADS:3 * D_HEADS + D_SIDE]
    gte = u[:, 3 * D_HEADS + D_SIDE:]
    glu_ref[0] = a * _sigmoid(gte)


def _rope_tables(t, ctx_len):
    rows = t // GRID_W
    row = np.repeat(np.arange(rows, dtype=np.float64), GRID_W)
    col = np.tile(np.arange(GRID_W, dtype=np.float64), rows)
    n_freq = DIFF_DH // 4
    inv = ROPE_BASE ** (-np.arange(n_freq, dtype=np.float64) / n_freq)
    ang = np.concatenate([row[:, None] * inv, col[:, None] * inv], axis=-1)
    cos, sin = np.cos(ang), np.sin(ang)
    cos = np.concatenate([cos, cos, cos, cos], axis=-1)
    sin = np.concatenate([-sin, sin, -sin, sin], axis=-1)
    cos = np.concatenate([cos, np.ones((ctx_len, 128))], axis=0)
    sin = np.concatenate([sin, np.zeros((ctx_len, 128))], axis=0)
    return jnp.asarray(cos, F32), jnp.asarray(sin, F32)


def _proj_odd(x, mod, g, w_all, cos, sin, nlat):
    b, ttot, d = x.shape
    tm = ROW_TILE
    nt = ttot // tm
    row = lambda w: pl.BlockSpec((1, tm, w), lambda bi, i: (bi, i, 0))
    tab = pl.BlockSpec((tm, 128), lambda bi, i: (i, 0))
    return pl.pallas_call(
        _proj_odd_kernel, name="proj_odd",
        grid=(b, nt),
        in_specs=[row(d), _mod_spec(b, nlat), _const_spec((1, d)), _const_spec(w_all.shape), tab, tab],
        out_specs=[row(D_HEADS), row(D_HEADS),
                   pl.BlockSpec((1, D_HEADS, tm), lambda bi, i: (bi, 0, i)), row(D_SIDE)],
        out_shape=[jax.ShapeDtypeStruct((b, ttot, D_HEADS), BF16)] * 2
        + [jax.ShapeDtypeStruct((b, D_HEADS, ttot), BF16), jax.ShapeDtypeStruct((b, ttot, D_SIDE), F32)],
        compiler_params=_cparams("parallel", "parallel"),
    )(x, mod, g, w_all, cos, sin)


def _attn_kernel(q_ref, k_ref, vt_ref, lam_ref, gs_ref, o_ref, acc0, acc1, *bufs, tk, nk, lam_init):
    q = q_ref[0]
    q0 = q[:, :DIFF_DH]
    q1 = q[:, DIFF_DH:]
    tq = q.shape[0]
    ones = jnp.ones((ATTN_SUM_ROWS, tk), BF16)
    acc0[...] = jnp.zeros_like(acc0)
    acc1[...] = jnp.zeros_like(acc1)

    def online(st, m_old, acc, vext):
        m_new = jnp.maximum(m_old, jnp.max(st, axis=0, keepdims=True))
        p = jnp.exp2(st - m_new).astype(BF16)
        acc[...] = jnp.exp2(m_old - m_new) * acc[...] + _dot(vext, p)
        return m_new

    def scores(j, s_buf):
        kk = k_ref[0, j * tk:(j + 1) * tk, :]
        s_buf[0] = _dot_nt(kk[:, :DIFF_DH], q0)
        s_buf[1] = _dot_nt(kk[:, DIFF_DH:], q1)

    def consume(j, s_buf, m0, m1):
        vext = jnp.concatenate([vt_ref[0, :, j * tk:(j + 1) * tk], ones], axis=0)
        return online(s_buf[0], m0, acc0, vext), online(s_buf[1], m1, acc1, vext)

    init = jnp.full((1, tq), NEG, F32)
    carry = (init, init)
    ahead = len(bufs) - 1
    for j in range(min(ahead, nk)):
        scores(j, bufs[j % len(bufs)])
    for j in range(nk):
        if j + ahead < nk:
            scores(j + ahead, bufs[(j + ahead) % len(bufs)])
        carry = consume(j, bufs[j % len(bufs)], *carry)
    lp = lam_ref[...]
    lam = (jnp.exp(jnp.sum(lp[0:1] * lp[1:2], axis=1, keepdims=True))
           - jnp.exp(jnp.sum(lp[2:3] * lp[3:4], axis=1, keepdims=True)) + lam_init)
    a0 = acc0[...]
    a1 = acc1[...]
    o = a0[:HEAD_W] / a0[HEAD_W:HEAD_W + 1] - lam * (a1[:HEAD_W] / a1[HEAD_W:HEAD_W + 1])
    o = o * lax.rsqrt(jnp.mean(o * o, axis=0, keepdims=True) + EPS)
    o_ref[0] = (o.T * gs_ref[...] * (1.0 - lam_init)).astype(BF16)


def _diff_attention(q, k, v, lam_p, g_sub, t, lam_init):
    b, ttot, _ = q.shape
    tq = ATTN_TQ
    tk = max(n for n in range(256, ATTN_TK + 1, 256) if ttot % n == 0)
    return pl.pallas_call(
        functools.partial(_attn_kernel, tk=tk, nk=ttot // tk, lam_init=lam_init), name="diff_attn",
        grid=(b, HEADS, t // tq),
        in_specs=[pl.BlockSpec((1, tq, HEAD_W), lambda bi, h, i: (bi, i, h)),
                  pl.BlockSpec((1, ttot, HEAD_W), lambda bi, h, i: (bi, 0, h)),
                  pl.BlockSpec((1, HEAD_W, ttot), lambda bi, h, i: (bi, h, 0)),
                  _const_spec(lam_p.shape), _const_spec((1, HEAD_W))],
        out_specs=pl.BlockSpec((1, tq, HEAD_W), lambda bi, h, i: (bi, i, h)),
        out_shape=jax.ShapeDtypeStruct((b, t, D_HEADS), BF16),
        scratch_shapes=[pltpu.VMEM((HEAD_W + ATTN_SUM_ROWS, tq), F32)] * 2
        + [pltpu.VMEM((2, tk, tq), F32)] * ATTN_SCORE_BUFS,
        compiler_params=_cparams("parallel", "parallel", "parallel"),
    )(q, k, v, lam_p, g_sub)


def _odd_finish_kernel(o_ref, glu_ref, prev_ref, next_ref, wdw_ref, gln_ref, bln_ref, x_ref, mod_ref,
                       w_ref, g2_ref, w1_ref, w2_ref, gf_ref, out_ref, xs, *, tm, nlat):
    i = pl.program_id(1)
    hal = CONV_HALO
    xs[0:hal, :] = jnp.where(i == 0, 0.0, prev_ref[0])
    xs[hal:hal + tm, :] = glu_ref[0]
    xs[hal + tm:, :] = jnp.where(i == nlat - 1, 0.0, next_ref[0])
    wdw = wdw_ref[...]
    z = jnp.zeros((tm, D_SIDE), F32)
    off = hal - (CONV_W - 1) // 2
    for r in range(SUBLANES):
        u = None
        for a in range((off + CONV_W - 1) // SUBLANES + 1):
            kk = SUBLANES * a + r - off
            if 0 <= kk < CONV_W:
                term = xs[SUBLANES * a:SUBLANES * a + tm + SUBLANES, :] * wdw[kk:kk + 1]
                u = term if u is None else u + term
        z = z + u[r:r + tm]
    mu = jnp.mean(z, axis=-1, keepdims=True)
    zc = z - mu
    var = jnp.mean(zc * zc, axis=-1, keepdims=True)
    z = zc * lax.rsqrt(var + EPS) * gln_ref[...] + bln_ref[...]
    z = z * _sigmoid(z)
    cat = jnp.concatenate([o_ref[0], z.astype(BF16)], axis=1)
    m = mod_ref[0]
    x3 = x_ref[0] + m[2:3] * _dot(cat, w_ref[...])
    out_ref[0] = _rms(_mlp_rows(x3, m, g2_ref[...], w1_ref, w2_ref), gf_ref[...])


def _odd_finish(o, glu, w_dw, g_ln, b_ln, x, mod, w_out, g2, w1, w2, g_final, t):
    b, _, d = x.shape
    tm = LATENT_ROW_TILE
    nlat = t // tm
    r = tm // CONV_HALO
    row = lambda w: pl.BlockSpec((1, tm, w), lambda bi, i: (bi, i, 0))
    return pl.pallas_call(
        functools.partial(_odd_finish_kernel, tm=tm, nlat=nlat), name="odd_finish_mlp",
        grid=(b, nlat),
        in_specs=[row(D_HEADS), row(D_SIDE),
                  pl.BlockSpec((1, CONV_HALO, D_SIDE), lambda bi, i: (bi, jnp.maximum(i * r - 1, 0), 0)),
                  pl.BlockSpec((1, CONV_HALO, D_SIDE), lambda bi, i: (bi, (i + 1) * r, 0)),
                  _const_spec((CONV_W, D_SIDE)), _const_spec((1, D_SIDE)), _const_spec((1, D_SIDE)),
                  row(d), _mod_spec(b, nlat), _resident_spec((d, d)),
                  _const_spec((1, d)), _resident_spec(w1.shape), _resident_spec(w2.shape),
                  _const_spec((1, d))],
        out_specs=row(d),
        out_shape=jax.ShapeDtypeStruct((b, t, d), F32),
        scratch_shapes=[pltpu.VMEM((tm + 2 * CONV_HALO, D_SIDE), F32)],
        compiler_params=_cparams("parallel", "parallel"),
    )(o, glu, glu, glu, w_dw, g_ln, b_ln, x, mod, w_out, g2, w1, w2, g_final)


def kernel(x, c, ctx, c_ctx, w_mod, b_mod, g_norm, w_in_even, b_gate, w_qk_conv, g_mlstm_head,
           w_out_even, w_in_odd, lam_p, g_subln, w_dw, g_conv_ln, b_conv_ln, w_out_odd,
           w_ff1, w_ff2, g_final):
    b, t, d = x.shape
    ctx_len = ctx.shape[1]
    assert w_mod.shape[0] == 2 and d == 1024 and ctx_len == ROW_TILE and t % (FFT_N1 * 8) == 0
    nlat = t // ROW_TILE
    mod = _mod_vectors(c, c_ctx, w_mod, b_mod)

    we = w_in_even[0]
    g0 = 4 * D_HEADS
    w_gate = jnp.zeros((d, 256), F32).at[:, 0:12].set(we[:, g0:g0 + 12]).at[:, 128:140].set(we[:, g0 + 12:g0 + 24])
    w_all = jnp.concatenate([we[:, :g0], w_gate, we[:, g0 + 24:]], axis=1).astype(BF16)
    bg = jnp.zeros((1, 256), F32).at[0, 0:12].set(b_gate[0, :12]).at[0, 128:140].set(b_gate[0, 12:])
    q, k, v, o, gates, p = _proj_even(x, ctx, mod[0], g_norm[0, 0][None], w_all, bg, _channel_dft(),
                                      w_qk_conv[0], nlat)
    hf, hb = _mlstm(q, k, v, gates, t)
    y_lat, y_ctx = _fourier(p, t, ctx_len)
    x2 = _even_finish(hf, hb, o, y_lat, y_ctx, x, ctx, mod[0], g_mlstm_head[0][None],
                      w_out_even[0].astype(BF16), g_norm[0, 1][None], w_ff1[0].astype(BF16),
                      w_ff2[0].astype(BF16), nlat)

    lam_init = 0.8 - 0.6 * math.exp(-0.3 * 1)
    cos, sin = _rope_tables(t, ctx_len)
    qa, ka, va, glu = _proj_odd(x2, mod[1], g_norm[1, 0][None], w_in_odd[0].astype(BF16), cos, sin, nlat)
    oa = _diff_attention(qa, ka, va, lam_p[0], g_subln[0][None], t, lam_init)
    return _odd_finish(oa, glu, w_dw[0], g_conv_ln[0][None], b_conv_ln[0][None], x2, mod[1],
                       w_out_odd[0].astype(BF16), g_norm[1, 1][None], w_ff1[1].astype(BF16),
                       w_ff2[1].astype(BF16), g_final[None], t)
```

```python
import functools
import math

import numpy as np
import jax
import jax.numpy as jnp
from jax import lax
from jax.experimental import pallas as pl
from jax.experimental.pallas import tpu as pltpu

F32 = jnp.float32
BF16 = jnp.bfloat16

EPS = 1e-6
SUBLANES = 8
HEADS = 6
HEAD_W = 128
D_HEADS = HEADS * HEAD_W
D_SIDE = 256
FNET_GC = 64
QK_W = 2 * D_HEADS
CONV_W = 31
CONV_HALO = 16
GRID_W = 64
ROPE_BASE = 10000.0
DIFF_DH = 64
FFT_N1 = 128
ROW_TILE = 256
LATENT_ROW_TILE = 512
MLSTM_CHUNK = 256
MLSTM_SUM_ROWS = 16
ATTN_TQ = 1024
ATTN_TK = 768
ATTN_SCORE_BUFS = 2
ATTN_SUM_ROWS = 16
NEG = -1e30
Q_SCALE_LOG2 = (DIFF_DH ** -0.5) * math.log2(math.e)
VMEM_LIMIT = 56 * 1024 * 1024


def _cparams(*sem, flags=None):
    return pltpu.CompilerParams(dimension_semantics=sem, vmem_limit_bytes=VMEM_LIMIT, flags=flags)


def _dot(a, b):
    return jnp.dot(a, b, preferred_element_type=F32)


def _dot_nt(a, b):
    return lax.dot_general(a, b, (((1,), (1,)), ((), ())), preferred_element_type=F32)


def _dot_tn(a, b):
    return lax.dot_general(a, b, (((0,), (0,)), ((), ())), preferred_element_type=F32)


def _sigmoid(x):
    return 1.0 / (1.0 + jnp.exp(-x))


def _rms(x, g):
    return x * lax.rsqrt(jnp.mean(x * x, axis=-1, keepdims=True) + EPS) * g


def _norm_mod(x, g, shift, scale):
    return _rms(x, g) * (1.0 + scale) + shift


def _mod_kernel(s_ref, w_ref, b_ref, o_ref):
    s = s_ref[...]
    s = s * _sigmoid(s)
    o_ref[0] = _dot(s.astype(BF16), w_ref[0].astype(BF16)) + b_ref[0]


def _mod_vectors(c, c_ctx, w_mod, b_mod):
    depth, d, n = w_mod.shape
    b = c.shape[0]
    s = jnp.zeros((8, d), F32).at[:b].set(c).at[b].set(c_ctx)
    tn = 1536
    out = pl.pallas_call(
        _mod_kernel, name="adaln_mod",
        grid=(depth, n // tn),
        in_specs=[pl.BlockSpec((8, d), lambda l, j: (0, 0)),
                  pl.BlockSpec((1, d, tn), lambda l, j: (l, 0, j)),
                  pl.BlockSpec((1, 1, tn), lambda l, j: (l, 0, j))],
        out_specs=pl.BlockSpec((1, 8, tn), lambda l, j: (l, 0, j)),
        out_shape=jax.ShapeDtypeStruct((depth, 8, n), F32),
        compiler_params=_cparams("parallel", "parallel"),
    )(s, w_mod, b_mod.reshape(depth, 1, n))
    return out.reshape(depth, 8, 6, d)


def _mod_spec(b, nlat):
    return pl.BlockSpec((1, 6, 1024), lambda bi, i: (jnp.where(i >= nlat, b, bi), 0, 0))


def _const_spec(shape):
    nd = len(shape)
    return pl.BlockSpec(shape, lambda *_: (0,) * nd)


def _tile_rows(x_ref, ctx_ref, nlat):
    return jnp.where(pl.program_id(1) >= nlat, ctx_ref[0], x_ref[0])


def _lat_ctx_specs(d, tm, nlat):
    return [pl.BlockSpec((1, tm, d), lambda bi, i: (bi, jnp.minimum(i, nlat - 1), 0)),
            pl.BlockSpec((1, tm, d), lambda bi, i: (bi, jnp.maximum(i - nlat, 0), 0))]


def _proj_even_kernel(x_ref, ctx_ref, xp_ref, xn_ref, mod_ref, g_ref, w_ref, bg_ref, dft_ref, wc_ref,
                      q_ref, k_ref, vt_ref, o_ref, gt_ref, p_ref, *, tm, nlat):
    i = pl.program_id(1)
    m = mod_ref[0]
    g = g_ref[...]
    h = _norm_mod(_tile_rows(x_ref, ctx_ref, nlat), g, m[0:1], m[1:2]).astype(BF16)
    xq = _dot(h, w_ref[:, :QK_W])
    u = _dot(h, w_ref[:, QK_W:])
    halo = jnp.concatenate([xp_ref[0], xn_ref[0]], axis=0)
    uh = _dot(_norm_mod(halo, g, m[0:1], m[1:2]).astype(BF16), w_ref[:, :QK_W])
    prow = jnp.where(jnp.logical_or(i == 0, i >= nlat), 0.0, uh[SUBLANES - 1:SUBLANES])
    nrow = jnp.where(i >= nlat - 1, 0.0, uh[SUBLANES:SUBLANES + 1])
    rid = lax.broadcasted_iota(jnp.int32, xq.shape, 0)
    xm = jnp.where(rid == 0, prow, pltpu.roll(xq, 1, axis=0))
    xp = jnp.where(rid == tm - 1, nrow, pltpu.roll(xq, tm - 1, axis=0))
    wc = wc_ref[...]
    y = xm * wc[0:1] + xq * wc[1:2] + xp * wc[2:3]
    y = y * _sigmoid(y)
    q_ref[0] = (y[:, :D_HEADS] * (HEAD_W ** -0.5)).astype(BF16)
    k_ref[0] = y[:, D_HEADS:].astype(BF16)
    vt_ref[0] = u[:, :D_HEADS].T.astype(BF16)
    o_ref[0] = u[:, D_HEADS:2 * D_HEADS]
    c0 = 2 * D_HEADS
    gt_ref[0] = u[:, c0:c0 + 256] + bg_ref[...]
    f = u[:, c0 + 256:c0 + 512].astype(BF16)
    p_ref[0] = _dot(f, dft_ref[...].astype(BF16))


def _proj_even(x, ctx, mod, g, w_all, bg, dftc, w_conv, nlat):
    b, t, d = x.shape
    ttot = t + ctx.shape[1]
    tm = ROW_TILE
    nt = ttot // tm
    r8 = tm // SUBLANES
    last8 = t // SUBLANES - 1
    row = lambda w: pl.BlockSpec((1, tm, w), lambda bi, i: (bi, i, 0))
    halo = lambda f: pl.BlockSpec((1, SUBLANES, d), lambda bi, i: (bi, jnp.clip(f(i), 0, last8), 0))
    return pl.pallas_call(
        functools.partial(_proj_even_kernel, tm=tm, nlat=nlat), name="proj_even",
        grid=(b, nt),
        in_specs=_lat_ctx_specs(d, tm, nlat)
        + [halo(lambda i: i * r8 - 1), halo(lambda i: (i + 1) * r8),
           _mod_spec(b, nlat), _const_spec((1, d)), _const_spec(w_all.shape), _const_spec((1, 256)),
           _const_spec(dftc.shape), _const_spec((3, QK_W))],
        out_specs=[row(D_HEADS), row(D_HEADS), pl.BlockSpec((1, D_HEADS, tm), lambda bi, i: (bi, 0, i)),
                   row(D_HEADS), row(256), row(512)],
        out_shape=[jax.ShapeDtypeStruct((b, ttot, D_HEADS), BF16),
                   jax.ShapeDtypeStruct((b, ttot, D_HEADS), BF16),
                   jax.ShapeDtypeStruct((b, D_HEADS, ttot), BF16),
                   jax.ShapeDtypeStruct((b, ttot, D_HEADS), F32),
                   jax.ShapeDtypeStruct((b, ttot, 256), F32),
                   jax.ShapeDtypeStruct((b, ttot, 512), F32)],
        compiler_params=_cparams("parallel", "parallel"),
    )(x, ctx, x, x, mod, g, w_all, bg, dftc, w_conv)


def _mlstm_kernel(qf_ref, kf_ref, vtf_ref, gf_ref, qb_ref, kb_ref, vtb_ref, gb_ref, hf_ref, hb_ref,
                  c_scr, m_scr, *, chunk):
    c = pl.program_id(1)

    @pl.when(c == 0)
    def _():
        c_scr[...] = jnp.zeros_like(c_scr)
        m_scr[...] = jnp.zeros_like(m_scr)

    row = lax.broadcasted_iota(jnp.int32, (chunk, chunk), 0)
    col = lax.broadcasted_iota(jnp.int32, (chunk, chunk), 1)
    lower, upper = col <= row, col >= row
    dirs = [_mlstm_gates(gf_ref, lower, chunk) + (qf_ref, kf_ref, vtf_ref, hf_ref, c_scr.at[0], m_scr.at[0], upper),
            _mlstm_gates(gb_ref, upper, chunk) + (qb_ref, kb_ref, vtb_ref, hb_ref, c_scr.at[1], m_scr.at[1], lower)]
    ones = jnp.ones((MLSTM_SUM_ROWS, chunk), BF16)
    chains = [(d, h) for h in range(HEADS) for d in dirs]

    stage1 = []
    for (r_all, rows, q_ref, k_ref, vt_ref, h_ref, cs, ms, mask_t), h in chains:
        sl = slice(h * HEAD_W, (h + 1) * HEAD_W)
        m_prev = ms[h:h + 1, 0:1]
        rm = jnp.where(mask_t, r_all[:, h:h + 1], NEG)
        mx = jnp.maximum(jnp.max(rm, axis=0, keepdims=True), m_prev)
        st = (_dot_nt(k_ref[0, :, sl], q_ref[0, :, sl]) * jnp.exp(rm - mx)).astype(BF16)
        stage1.append((m_prev, mx, st))

    for ((r_all, rows, q_ref, k_ref, vt_ref, h_ref, cs, ms, mask_t), h), (m_prev, mx, st) in zip(chains, stage1):
        sl = slice(h * HEAD_W, (h + 1) * HEAD_W)
        qh = q_ref[0, :, sl]
        kh = k_ref[0, :, sl]
        vext = jnp.concatenate([vt_ref[0, sl, :], ones], axis=0)
        r_row = rows[h:h + 1, :]
        a_row = rows[HEADS + h:HEADS + h + 1, :]
        ce = cs[h]
        nd = _dot(vext, st) + jnp.exp(m_prev - mx) * _dot_nt(ce.astype(BF16), qh)
        den = jnp.maximum(jnp.abs(nd[HEAD_W:HEAD_W + 1]), jnp.exp(-(a_row + mx)))
        h_ref[0, sl, :] = (nd[:HEAD_W] / den).astype(BF16)

    for ((r_all, rows, q_ref, k_ref, vt_ref, h_ref, cs, ms, mask_t), h), (m_prev, mx, st) in zip(chains, stage1):
        sl = slice(h * HEAD_W, (h + 1) * HEAD_W)
        kh = k_ref[0, :, sl]
        vext = jnp.concatenate([vt_ref[0, sl, :], ones], axis=0)
        r_row = rows[h:h + 1, :]
        a_row = rows[HEADS + h:HEADS + h + 1, :]
        ce = cs[h]
        mx_last = jnp.max(mx, axis=1, keepdims=True)
        a_last = jnp.min(a_row, axis=1, keepdims=True)
        wv = (vext.astype(F32) * jnp.exp(r_row - mx_last)).astype(BF16)
        cs[h] = jnp.exp(m_prev - mx_last) * ce + _dot(wv, kh)
        ms[h:h + 1, :] = jnp.broadcast_to(a_last + mx_last, (1, 128))


def _mlstm_gates(g_ref, before, chunk):
    gates = g_ref[0]
    logf = jnp.minimum(gates, 0.0) - jnp.log(1.0 + jnp.exp(-jnp.abs(gates)))
    csum = before.astype(BF16)
    f_hi = logf.astype(BF16)
    f_mid = (logf - f_hi.astype(F32)).astype(BF16)
    f_lo = (logf - f_hi.astype(F32) - f_mid.astype(F32)).astype(BF16)
    a_all = _dot(csum, f_hi) + _dot(csum, f_mid) + _dot(csum, f_lo)
    r_all = gates - pltpu.roll(a_all, 128 - HEADS, axis=1)
    lane = lax.broadcasted_iota(jnp.int32, (chunk, 128), 1)
    rows = jnp.where(lane < HEADS, r_all, a_all).T
    return r_all, rows


def _mlstm(q, k, vt, gates, nlat_rows):
    b, ttot, _ = q.shape
    L = MLSTM_CHUNK
    nc = ttot // L
    ncl = nlat_rows // L
    ncc = nc - ncl

    def fwd(c):
        return jnp.where(c < ncc, ncl + c, c - ncc)

    def bwd(c):
        return jnp.where(c < ncc, nc - 1 - c, ncl - 1 - (c - ncc))

    def specs(blk, d):
        head = pl.BlockSpec((1, L, D_HEADS), lambda bi, c: (bi, blk(c), 0))
        return [head, head, pl.BlockSpec((1, D_HEADS, L), lambda bi, c: (bi, 0, blk(c))),
                pl.BlockSpec((1, L, 128), lambda bi, c: (bi, blk(c), d))]

    out = lambda blk: pl.BlockSpec((1, D_HEADS, L), lambda bi, c: (bi, 0, blk(c)))
    return pl.pallas_call(
        functools.partial(_mlstm_kernel, chunk=L), name="mlstm_scan",
        grid=(b, nc),
        in_specs=specs(fwd, 0) + specs(bwd, 1),
        out_specs=[out(fwd), out(bwd)],
        out_shape=[jax.ShapeDtypeStruct((b, D_HEADS, ttot), BF16)] * 2,
        scratch_shapes=[pltpu.VMEM((2, HEADS, HEAD_W + MLSTM_SUM_ROWS, HEAD_W), F32),
                        pltpu.VMEM((2, 8, 128), F32)],
        compiler_params=_cparams("parallel", "arbitrary"),
    )(q, k, vt, gates, q, k, vt, gates)


def _fft1_kernel(p_ref, g_ref, o_ref):
    xt = jnp.swapaxes(p_ref[0, :, 0, :, :], 0, 1)
    out = []
    for r in range(SUBLANES):
        x = xt[r].astype(BF16)
        out.append(_dot(g_ref[r].astype(BF16), jnp.concatenate([x[:, :256], x[:, 256:]], axis=0)))
    bb = jnp.swapaxes(jnp.stack(out), 0, 1)
    o_ref[0, 0] = bb[:FFT_N1]
    o_ref[0, 1] = bb[FFT_N1:]


def _fft2_kernel(b_ref, t_ref, o_ref, *, kb, scale):
    tc = t_ref[0].astype(BF16)
    ts = t_ref[1].astype(BF16)
    ys = [(_dot(tc, b_ref[0, 0, j].astype(BF16)) + _dot(ts, b_ref[0, 1, j].astype(BF16))) * scale
          for j in range(kb)]
    o_ref[0] = jnp.swapaxes(jnp.stack(ys), 0, 1)


def _dft_ctx_kernel(p_ref, t_ref, o_ref, *, scale):
    p = p_ref[0].astype(BF16)
    st = jnp.concatenate([p[:, :256], p[:, 256:]], axis=0)
    o_ref[0] = _dot(t_ref[...].astype(BF16), st) * scale


def _fft_tables(t):
    n1, n2 = FFT_N1, t // FFT_N1
    k1 = np.arange(n1, dtype=np.int64)[None, :, None]
    nn = (n2 * np.arange(n1, dtype=np.int64)[None, None, :] + np.arange(n2, dtype=np.int64)[:, None, None])
    ang = 2.0 * np.pi * ((k1 * nn) % t).astype(np.float64) / t
    gr, gi = np.cos(ang), -np.sin(ang)
    g = np.concatenate([np.concatenate([gr, -gi], axis=2), np.concatenate([gi, gr], axis=2)], axis=1)
    a2 = 2.0 * np.pi * ((np.arange(n2)[:, None] * np.arange(n2)[None, :]) % n2) / n2
    t2 = np.stack([np.cos(a2), np.sin(a2)])
    return jnp.asarray(g, F32), jnp.asarray(t2, F32)


def _dft_matrix_cs(n):
    a = 2.0 * np.pi * ((np.arange(n)[:, None] * np.arange(n)[None, :]) % n) / n
    return np.cos(a), np.sin(a)


def _channel_dft():
    c, s = _dft_matrix_cs(FNET_GC)
    eye = np.eye(D_SIDE // FNET_GC)
    return jnp.asarray(np.concatenate([np.kron(eye, c), -np.kron(eye, s)], axis=1), F32)


def _fourier(p, t, ctx_len):
    b = p.shape[0]
    n1, n2 = FFT_N1, t // FFT_N1
    g, t2 = _fft_tables(t)
    kb = 16
    ttot = p.shape[1]
    p5 = p.reshape(b, ttot // n2, n2 // SUBLANES, SUBLANES, 512)
    st1 = pl.pallas_call(
        _fft1_kernel, name="fft_stage1",
        grid=(b, n2 // SUBLANES),
        in_specs=[pl.BlockSpec((1, n1, 1, SUBLANES, 512), lambda bi, j: (bi, 0, j, 0, 0)),
                  pl.BlockSpec((SUBLANES, 256, 256), lambda bi, j: (j, 0, 0))],
        out_specs=pl.BlockSpec((1, 2, n1, SUBLANES, 256), lambda bi, j: (bi, 0, 0, j, 0)),
        out_shape=jax.ShapeDtypeStruct((b, 2, n1, n2, 256), F32),
        compiler_params=_cparams("parallel", "parallel"),
    )(p5, g)
    y = pl.pallas_call(
        functools.partial(_fft2_kernel, kb=kb, scale=1.0 / math.sqrt(t * FNET_GC)), name="fft_stage2",
        grid=(b, n1 // kb),
        in_specs=[pl.BlockSpec((1, 2, kb, n2, 256), lambda bi, j: (bi, 0, j, 0, 0)),
                  _const_spec((2, n2, n2))],
        out_specs=pl.BlockSpec((1, n2, kb, 256), lambda bi, j: (bi, 0, j, 0)),
        out_shape=jax.ShapeDtypeStruct((b, n2, n1, 256), F32),
        compiler_params=_cparams("parallel", "parallel"),
    )(st1, t2)
    y_lat = y.reshape(b, t, 256)
    cc, sc = _dft_matrix_cs(ctx_len)
    tc = jnp.asarray(np.concatenate([cc, sc], axis=1), F32)
    y_ctx = pl.pallas_call(
        functools.partial(_dft_ctx_kernel, scale=1.0 / math.sqrt(ctx_len * FNET_GC)), name="dft_ctx",
        grid=(b,),
        in_specs=[pl.BlockSpec((1, ctx_len, 512), lambda bi: (bi, t // ctx_len, 0)),
                  _const_spec((ctx_len, 2 * ctx_len))],
        out_specs=pl.BlockSpec((1, ctx_len, 256), lambda bi: (bi, 0, 0)),
        out_shape=jax.ShapeDtypeStruct((b, ctx_len, 256), F32),
        compiler_params=_cparams("parallel"),
    )(p, tc)
    return y_lat, y_ctx


def _even_finish_kernel(hf_ref, hb_ref, o_ref, yl_ref, yc_ref, x_ref, ctx_ref, mod_ref, gh_ref, w_ref,
                        g2_ref, w1_ref, w2_ref, out_ref, *, nlat):
    i = pl.program_id(1)
    hs = (hf_ref[0].astype(F32) + hb_ref[0].astype(F32)).T
    gate = _sigmoid(o_ref[0])
    gh = gh_ref[...]
    parts = []
    for h in range(HEADS):
        sl = slice(h * HEAD_W, (h + 1) * HEAD_W)
        parts.append((_rms(hs[:, sl], gh[:, sl]) * gate[:, sl]).astype(BF16))
    y = jnp.where(i >= nlat, yc_ref[0], yl_ref[0])
    parts.append(y.astype(BF16))
    out = _dot(jnp.concatenate(parts, axis=1), w_ref[...])
    m = mod_ref[0]
    x1 = _tile_rows(x_ref, ctx_ref, nlat) + m[2:3] * out
    out_ref[0] = _mlp_rows(x1, m, g2_ref[...], w1_ref, w2_ref)


def _mlp_rows(x, m, g, w1_ref, w2_ref):
    h = _norm_mod(x, g, m[3:4], m[4:5]).astype(BF16)
    a = jnp.maximum(_dot(h, w1_ref[...]), 0.0)
    return x + m[5:6] * _dot((a * a).astype(BF16), w2_ref[...])


def _resident_spec(shape):
    nd = len(shape)
    return pl.BlockSpec(shape, lambda *_: (0,) * nd, pipeline_mode=pl.Buffered(1))


def _even_finish(hf, hb, o, y_lat, y_ctx, x, ctx, mod, g_head, w_out, g2, w1, w2, nlat):
    b, t, d = x.shape
    ttot = t + ctx.shape[1]
    tm = ROW_TILE
    nt = ttot // tm
    row = lambda w: pl.BlockSpec((1, tm, w), lambda bi, i: (bi, i, 0))
    return pl.pallas_call(
        functools.partial(_even_finish_kernel, nlat=nlat), name="even_finish_mlp",
        grid=(b, nt),
        in_specs=[pl.BlockSpec((1, D_HEADS, tm), lambda bi, i: (bi, 0, i)),
                  pl.BlockSpec((1, D_HEADS, tm), lambda bi, i: (bi, 0, i)),
                  row(D_HEADS),
                  pl.BlockSpec((1, tm, 256), lambda bi, i: (bi, jnp.minimum(i, nlat - 1), 0)),
                  pl.BlockSpec((1, tm, 256), lambda bi, i: (bi, jnp.maximum(i - nlat, 0), 0))]
        + _lat_ctx_specs(d, tm, nlat)
        + [_mod_spec(b, nlat), _const_spec((1, D_HEADS)), _resident_spec((d, d)),
           _const_spec((1, d)), _resident_spec(w1.shape), _resident_spec(w2.shape)],
        out_specs=row(d),
        out_shape=jax.ShapeDtypeStruct((b, ttot, d), F32),
        compiler_params=_cparams("parallel", "parallel"),
    )(hf, hb, o, y_lat, y_ctx, x, ctx, mod, g_head, w_out, g2, w1, w2)


def _swap_halves(x):
    lane = lax.broadcasted_iota(jnp.int32, x.shape, 1)
    return jnp.where(lane % 64 < 32, pltpu.roll(x, 96, axis=1), pltpu.roll(x, 32, axis=1))


def _proj_odd_kernel(x_ref, mod_ref, g_ref, w_ref, cos_ref, sin_ref, q_ref, k_ref, vt_ref, glu_ref):
    m = mod_ref[0]
    h = _norm_mod(x_ref[0], g_ref[...], m[0:1], m[1:2]).astype(BF16)
    u = _dot(h, w_ref[...])
    cos = cos_ref[...]
    sin = sin_ref[...]
    for hh in range(HEADS):
        sl = slice(hh * HEAD_W, (hh + 1) * HEAD_W)
        qh = u[:, sl]
        kh = u[:, D_HEADS + hh * HEAD_W:D_HEADS + (hh + 1) * HEAD_W]
        q_ref[0, :, sl] = ((qh * cos + _swap_halves(qh) * sin) * Q_SCALE_LOG2).astype(BF16)
        k_ref[0, :, sl] = (kh * cos + _swap_halves(kh) * sin).astype(BF16)
    vt_ref[0] = u[:, 2 * D_HEADS:3 * D_HEADS].T.astype(BF16)
    a = u[:, 3 * D_HEADS:3 * D_HEADS + D_SIDE]
    gte = u[:, 3 * D_HEADS + D_SIDE:]
    glu_ref[0] = a * _sigmoid(gte)


def _rope_tables(t, ctx_len):
    rows = t // GRID_W
    row = np.repeat(np.arange(rows, dtype=np.float64), GRID_W)
    col = np.tile(np.arange(GRID_W, dtype=np.float64), rows)
    n_freq = DIFF_DH // 4
    inv = ROPE_BASE ** (-np.arange(n_freq, dtype=np.float64) / n_freq)
    ang = np.concatenate([row[:, None] * inv, col[:, None] * inv], axis=-1)
    cos, sin = np.cos(ang), np.sin(ang)
    cos = np.concatenate([cos, cos, cos, cos], axis=-1)
    sin = np.concatenate([-sin, sin, -sin, sin], axis=-1)
    cos = np.concatenate([cos, np.ones((ctx_len, 128))], axis=0)
    sin = np.concatenate([sin, np.zeros((ctx_len, 128))], axis=0)
    return jnp.asarray(cos, F32), jnp.asarray(sin, F32)


def _proj_odd(x, mod, g, w_all, cos, sin, nlat):
    b, ttot, d = x.shape
    tm = ROW_TILE
    nt = ttot // tm
    row = lambda w: pl.BlockSpec((1, tm, w), lambda bi, i: (bi, i, 0))
    tab = pl.BlockSpec((tm, 128), lambda bi, i: (i, 0))
    return pl.pallas_call(
        _proj_odd_kernel, name="proj_odd",
        grid=(b, nt),
        in_specs=[row(d), _mod_spec(b, nlat), _const_spec((1, d)), _const_spec(w_all.shape), tab, tab],
        out_specs=[row(D_HEADS), row(D_HEADS),
                   pl.BlockSpec((1, D_HEADS, tm), lambda bi, i: (bi, 0, i)), row(D_SIDE)],
        out_shape=[jax.ShapeDtypeStruct((b, ttot, D_HEADS), BF16)] * 2
        + [jax.ShapeDtypeStruct((b, D_HEADS, ttot), BF16), jax.ShapeDtypeStruct((b, ttot, D_SIDE), F32)],
        compiler_params=_cparams("parallel", "parallel"),
    )(x, mod, g, w_all, cos, sin)


def _attn_kernel(q_ref, k_ref, vt_ref, lam_ref, gs_ref, o_ref, acc0, acc1, *bufs, tk, nk, lam_init):
    q = q_ref[0]
    q0 = q[:, :DIFF_DH]
    q1 = q[:, DIFF_DH:]
    tq = q.shape[0]
    ones = jnp.ones((ATTN_SUM_ROWS, tk), BF16)
    acc0[...] = jnp.zeros_like(acc0)
    acc1[...] = jnp.zeros_like(acc1)

    def online(st, m_old, acc, vext):
        m_new = jnp.maximum(m_old, jnp.max(st, axis=0, keepdims=True))
        p = jnp.exp2(st - m_new).astype(BF16)
        acc[...] = jnp.exp2(m_old - m_new) * acc[...] + _dot(vext, p)
        return m_new

    def scores(j, s_buf):
        kk = k_ref[0, j * tk:(j + 1) * tk, :]
        s_buf[0] = _dot_nt(kk[:, :DIFF_DH], q0)
        s_buf[1] = _dot_nt(kk[:, DIFF_DH:], q1)

    def consume(j, s_buf, m0, m1):
        vext = jnp.concatenate([vt_ref[0, :, j * tk:(j + 1) * tk], ones], axis=0)
        return online(s_buf[0], m0, acc0, vext), online(s_buf[1], m1, acc1, vext)

    init = jnp.full((1, tq), NEG, F32)
    carry = (init, init)
    ahead = len(bufs) - 1
    for j in range(min(ahead, nk)):
        scores(j, bufs[j % len(bufs)])
    for j in range(nk):
        if j + ahead < nk:
            scores(j + ahead, bufs[(j + ahead) % len(bufs)])
        carry = consume(j, bufs[j % len(bufs)], *carry)
    lp = lam_ref[...]
    lam = (jnp.exp(jnp.sum(lp[0:1] * lp[1:2], axis=1, keepdims=True))
           - jnp.exp(jnp.sum(lp[2:3] * lp[3:4], axis=1, keepdims=True)) + lam_init)
    a0 = acc0[...]
    a1 = acc1[...]
    o = a0[:HEAD_W] / a0[HEAD_W:HEAD_W + 1] - lam * (a1[:HEAD_W] / a1[HEAD_W:HEAD_W + 1])
    o = o * lax.rsqrt(jnp.mean(o * o, axis=0, keepdims=True) + EPS)
    o_ref[0] = (o.T * gs_ref[...] * (1.0 - lam_init)).astype(BF16)


def _diff_attention(q, k, v, lam_p, g_sub, t, lam_init):
    b, ttot, _ = q.shape
    tq = ATTN_TQ
    tk = max(n for n in range(256, ATTN_TK + 1, 256) if ttot % n == 0)
    return pl.pallas_call(
        functools.partial(_attn_kernel, tk=tk, nk=ttot // tk, lam_init=lam_init), name="diff_attn",
        grid=(b, HEADS, t // tq),
        in_specs=[pl.BlockSpec((1, tq, HEAD_W), lambda bi, h, i: (bi, i, h)),
                  pl.BlockSpec((1, ttot, HEAD_W), lambda bi, h, i: (bi, 0, h)),
                  pl.BlockSpec((1, HEAD_W, ttot), lambda bi, h, i: (bi, h, 0)),
                  _const_spec(lam_p.shape), _const_spec((1, HEAD_W))],
        out_specs=pl.BlockSpec((1, tq, HEAD_W), lambda bi, h, i: (bi, i, h)),
        out_shape=jax.ShapeDtypeStruct((b, t, D_HEADS), BF16),
        scratch_shapes=[pltpu.VMEM((HEAD_W + ATTN_SUM_ROWS, tq), F32)] * 2
        + [pltpu.VMEM((2, tk, tq), F32)] * ATTN_SCORE_BUFS,
        compiler_params=_cparams("parallel", "parallel", "parallel"),
    )(q, k, v, lam_p, g_sub)


def _odd_finish_kernel(o_ref, glu_ref, prev_ref, next_ref, wdw_ref, gln_ref, bln_ref, x_ref, mod_ref,
                       w_ref, g2_ref, w1_ref, w2_ref, gf_ref, out_ref, xs, *, tm, nlat):
    i = pl.program_id(1)
    hal = CONV_HALO
    xs[0:hal, :] = jnp.where(i == 0, 0.0, prev_ref[0])
    xs[hal:hal + tm, :] = glu_ref[0]
    xs[hal + tm:, :] = jnp.where(i == nlat - 1, 0.0, next_ref[0])
    wdw = wdw_ref[...]
    z = jnp.zeros((tm, D_SIDE), F32)
    off = hal - (CONV_W - 1) // 2
    for r in range(SUBLANES):
        u = None
        for a in range((off + CONV_W - 1) // SUBLANES + 1):
            kk = SUBLANES * a + r - off
            if 0 <= kk < CONV_W:
                term = xs[SUBLANES * a:SUBLANES * a + tm + SUBLANES, :] * wdw[kk:kk + 1]
                u = term if u is None else u + term
        z = z + u[r:r + tm]
    mu = jnp.mean(z, axis=-1, keepdims=True)
    zc = z - mu
    var = jnp.mean(zc * zc, axis=-1, keepdims=True)
    z = zc * lax.rsqrt(var + EPS) * gln_ref[...] + bln_ref[...]
    z = z * _sigmoid(z)
    cat = jnp.concatenate([o_ref[0], z.astype(BF16)], axis=1)
    m = mod_ref[0]
    x3 = x_ref[0] + m[2:3] * _dot(cat, w_ref[...])
    out_ref[0] = _rms(_mlp_rows(x3, m, g2_ref[...], w1_ref, w2_ref), gf_ref[...])


def _odd_finish(o, glu, w_dw, g_ln, b_ln, x, mod, w_out, g2, w1, w2, g_final, t):
    b, _, d = x.shape
    tm = LATENT_ROW_TILE
    nlat = t // tm
    r = tm // CONV_HALO
    row = lambda w: pl.BlockSpec((1, tm, w), lambda bi, i: (bi, i, 0))
    return pl.pallas_call(
        functools.partial(_odd_finish_kernel, tm=tm, nlat=nlat), name="odd_finish_mlp",
        grid=(b, nlat),
        in_specs=[row(D_HEADS), row(D_SIDE),
                  pl.BlockSpec((1, CONV_HALO, D_SIDE), lambda bi, i: (bi, jnp.maximum(i * r - 1, 0), 0)),
                  pl.BlockSpec((1, CONV_HALO, D_SIDE), lambda bi, i: (bi, (i + 1) * r, 0)),
                  _const_spec((CONV_W, D_SIDE)), _const_spec((1, D_SIDE)), _const_spec((1, D_SIDE)),
                  row(d), _mod_spec(b, nlat), _resident_spec((d, d)),
                  _const_spec((1, d)), _resident_spec(w1.shape), _resident_spec(w2.shape),
                  _const_spec((1, d))],
        out_specs=row(d),
        out_shape=jax.ShapeDtypeStruct((b, t, d), F32),
        scratch_shapes=[pltpu.VMEM((tm + 2 * CONV_HALO, D_SIDE), F32)],
        compiler_params=_cparams("parallel", "parallel"),
    )(o, glu, glu, glu, w_dw, g_ln, b_ln, x, mod, w_out, g2, w1, w2, g_final)


def kernel(x, c, ctx, c_ctx, w_mod, b_mod, g_norm, w_in_even, b_gate, w_qk_conv, g_mlstm_head,
           w_out_even, w_in_odd, lam_p, g_subln, w_dw, g_conv_ln, b_conv_ln, w_out_odd,
           w_ff1, w_ff2, g_final):
    b, t, d = x.shape
    ctx_len = ctx.shape[1]
    assert w_mod.shape[0] == 2 and d == 1024 and ctx_len == ROW_TILE and t % (FFT_N1 * 8) == 0
    nlat = t // ROW_TILE
    mod = _mod_vectors(c, c_ctx, w_mod, b_mod)

    we = w_in_even[0]
    g0 = 4 * D_HEADS
    w_gate = jnp.zeros((d, 256), F32).at[:, 0:12].set(we[:, g0:g0 + 12]).at[:, 128:140].set(we[:, g0 + 12:g0 + 24])
    w_all = jnp.concatenate([we[:, :g0], w_gate, we[:, g0 + 24:]], axis=1).astype(BF16)
    bg = jnp.zeros((1, 256), F32).at[0, 0:12].set(b_gate[0, :12]).at[0, 128:140].set(b_gate[0, 12:])
    q, k, v, o, gates, p = _proj_even(x, ctx, mod[0], g_norm[0, 0][None], w_all, bg, _channel_dft(),
                                      w_qk_conv[0], nlat)
    hf, hb = _mlstm(q, k, v, gates, t)
    y_lat, y_ctx = _fourier(p, t, ctx_len)
    x2 = _even_finish(hf, hb, o, y_lat, y_ctx, x, ctx, mod[0], g_mlstm_head[0][None],
                      w_out_even[0].astype(BF16), g_norm[0, 1][None], w_ff1[0].astype(BF16),
                      w_ff2[0].astype(BF16), nlat)

    lam_init = 0.8 - 0.6 * math.exp(-0.3 * 1)
    cos, sin = _rope_tables(t, ctx_len)
    qa, ka, va, glu = _proj_odd(x2, mod[1], g_norm[1, 0][None], w_in_odd[0].astype(BF16), cos, sin, nlat)
    oa = _diff_attention(qa, ka, va, lam_p[0], g_subln[0][None], t, lam_init)
    return _odd_finish(oa, glu, w_dw[0], g_conv_ln[0][None], b_conv_ln[0][None], x2, mod[1],
                       w_out_odd[0].astype(BF16), g_norm[1, 1][None], w_ff1[1].astype(BF16),
                       w_ff2[1].astype(BF16), g_final[None], t)
```

```python
import functools
import math

import numpy as np
import jax
import jax.numpy as jnp
from jax import lax
from jax.experimental import pallas as pl
from jax.experimental.pallas import tpu as pltpu

F32 = jnp.float32
BF16 = jnp.bfloat16

EPS = 1e-6
SUBLANES = 8
HEADS = 6
HEAD_W = 128
D_HEADS = HEADS * HEAD_W
D_SIDE = 256
FNET_GC = 64
QK_W = 2 * D_HEADS
CONV_W = 31
CONV_HALO = 16
GRID_W = 64
ROPE_BASE = 10000.0
DIFF_DH = 64
FFT_N1 = 128
ROW_TILE = 256
LATENT_ROW_TILE = 512
MLSTM_CHUNK = 256
MLSTM_SUM_ROWS = 16
ATTN_TQ = 2048
ATTN_TK = 768
ATTN_SCORE_BUFS = 2
ATTN_SUM_ROWS = 16
NEG = -1e30
Q_SCALE_LOG2 = (DIFF_DH ** -0.5) * math.log2(math.e)
VMEM_LIMIT = 56 * 1024 * 1024


def _cparams(*sem, flags=None):
    return pltpu.CompilerParams(dimension_semantics=sem, vmem_limit_bytes=VMEM_LIMIT, flags=flags)


def _dot(a, b):
    return jnp.dot(a, b, preferred_element_type=F32)


def _dot_nt(a, b):
    return lax.dot_general(a, b, (((1,), (1,)), ((), ())), preferred_element_type=F32)


def _dot_tn(a, b):
    return lax.dot_general(a, b, (((0,), (0,)), ((), ())), preferred_element_type=F32)


def _sigmoid(x):
    return 1.0 / (1.0 + jnp.exp(-x))


def _rms(x, g):
    return x * lax.rsqrt(jnp.mean(x * x, axis=-1, keepdims=True) + EPS) * g


def _norm_mod(x, g, shift, scale):
    return _rms(x, g) * (1.0 + scale) + shift


def _mod_kernel(s_ref, w_ref, b_ref, o_ref):
    s = s_ref[...]
    s = s * _sigmoid(s)
    o_ref[0] = _dot(s.astype(BF16), w_ref[0].astype(BF16)) + b_ref[0]


def _mod_vectors(c, c_ctx, w_mod, b_mod):
    depth, d, n = w_mod.shape
    b = c.shape[0]
    s = jnp.zeros((8, d), F32).at[:b].set(c).at[b].set(c_ctx)
    tn = 1536
    out = pl.pallas_call(
        _mod_kernel, name="adaln_mod",
        grid=(depth, n // tn),
        in_specs=[pl.BlockSpec((8, d), lambda l, j: (0, 0)),
                  pl.BlockSpec((1, d, tn), lambda l, j: (l, 0, j)),
                  pl.BlockSpec((1, 1, tn), lambda l, j: (l, 0, j))],
        out_specs=pl.BlockSpec((1, 8, tn), lambda l, j: (l, 0, j)),
        out_shape=jax.ShapeDtypeStruct((depth, 8, n), F32),
        compiler_params=_cparams("parallel", "parallel"),
    )(s, w_mod, b_mod.reshape(depth, 1, n))
    return out.reshape(depth, 8, 6, d)


def _mod_spec(b, nlat):
    return pl.BlockSpec((1, 6, 1024), lambda bi, i: (jnp.where(i >= nlat, b, bi), 0, 0))


def _const_spec(shape):
    nd = len(shape)
    return pl.BlockSpec(shape, lambda *_: (0,) * nd)


def _tile_rows(x_ref, ctx_ref, nlat):
    return jnp.where(pl.program_id(1) >= nlat, ctx_ref[0], x_ref[0])


def _lat_ctx_specs(d, tm, nlat):
    return [pl.BlockSpec((1, tm, d), lambda bi, i: (bi, jnp.minimum(i, nlat - 1), 0)),
            pl.BlockSpec((1, tm, d), lambda bi, i: (bi, jnp.maximum(i - nlat, 0), 0))]


def _proj_even_kernel(x_ref, ctx_ref, xp_ref, xn_ref, mod_ref, g_ref, w_ref, bg_ref, dft_ref, wc_ref,
                      q_ref, k_ref, vt_ref, o_ref, gt_ref, p_ref, *, tm, nlat):
    i = pl.program_id(1)
    m = mod_ref[0]
    g = g_ref[...]
    h = _norm_mod(_tile_rows(x_ref, ctx_ref, nlat), g, m[0:1], m[1:2]).astype(BF16)
    xq = _dot(h, w_ref[:, :QK_W])
    u = _dot(h, w_ref[:, QK_W:])
    halo = jnp.concatenate([xp_ref[0], xn_ref[0]], axis=0)
    uh = _dot(_norm_mod(halo, g, m[0:1], m[1:2]).astype(BF16), w_ref[:, :QK_W])
    prow = jnp.where(jnp.logical_or(i == 0, i >= nlat), 0.0, uh[SUBLANES - 1:SUBLANES])
    nrow = jnp.where(i >= nlat - 1, 0.0, uh[SUBLANES:SUBLANES + 1])
    rid = lax.broadcasted_iota(jnp.int32, xq.shape, 0)
    xm = jnp.where(rid == 0, prow, pltpu.roll(xq, 1, axis=0))
    xp = jnp.where(rid == tm - 1, nrow, pltpu.roll(xq, tm - 1, axis=0))
    wc = wc_ref[...]
    y = xm * wc[0:1] + xq * wc[1:2] + xp * wc[2:3]
    y = y * _sigmoid(y)
    q_ref[0] = (y[:, :D_HEADS] * (HEAD_W ** -0.5)).astype(BF16)
    k_ref[0] = y[:, D_HEADS:].astype(BF16)
    vt_ref[0] = u[:, :D_HEADS].T.astype(BF16)
    o_ref[0] = u[:, D_HEADS:2 * D_HEADS]
    c0 = 2 * D_HEADS
    gt_ref[0] = u[:, c0:c0 + 256] + bg_ref[...]
    f = u[:, c0 + 256:c0 + 512].astype(BF16)
    p_ref[0] = _dot(f, dft_ref[...].astype(BF16))


def _proj_even(x, ctx, mod, g, w_all, bg, dftc, w_conv, nlat):
    b, t, d = x.shape
    ttot = t + ctx.shape[1]
    tm = ROW_TILE
    nt = ttot // tm
    r8 = tm // SUBLANES
    last8 = t // SUBLANES - 1
    row = lambda w: pl.BlockSpec((1, tm, w), lambda bi, i: (bi, i, 0))
    halo = lambda f: pl.BlockSpec((1, SUBLANES, d), lambda bi, i: (bi, jnp.clip(f(i), 0, last8), 0))
    return pl.pallas_call(
        functools.partial(_proj_even_kernel, tm=tm, nlat=nlat), name="proj_even",
        grid=(b, nt),
        in_specs=_lat_ctx_specs(d, tm, nlat)
        + [halo(lambda i: i * r8 - 1), halo(lambda i: (i + 1) * r8),
           _mod_spec(b, nlat), _const_spec((1, d)), _const_spec(w_all.shape), _const_spec((1, 256)),
           _const_spec(dftc.shape), _const_spec((3, QK_W))],
        out_specs=[row(D_HEADS), row(D_HEADS), pl.BlockSpec((1, D_HEADS, tm), lambda bi, i: (bi, 0, i)),
                   row(D_HEADS), row(256), row(512)],
        out_shape=[jax.ShapeDtypeStruct((b, ttot, D_HEADS), BF16),
                   jax.ShapeDtypeStruct((b, ttot, D_HEADS), BF16),
                   jax.ShapeDtypeStruct((b, D_HEADS, ttot), BF16),
                   jax.ShapeDtypeStruct((b, ttot, D_HEADS), F32),
                   jax.ShapeDtypeStruct((b, ttot, 256), F32),
                   jax.ShapeDtypeStruct((b, ttot, 512), F32)],
        compiler_params=_cparams("parallel", "parallel"),
    )(x, ctx, x, x, mod, g, w_all, bg, dftc, w_conv)


def _mlstm_kernel(qf_ref, kf_ref, vtf_ref, gf_ref, qb_ref, kb_ref, vtb_ref, gb_ref, hf_ref, hb_ref,
                  c_scr, m_scr, *, chunk):
    c = pl.program_id(1)

    @pl.when(c == 0)
    def _():
        c_scr[...] = jnp.zeros_like(c_scr)
        m_scr[...] = jnp.zeros_like(m_scr)

    row = lax.broadcasted_iota(jnp.int32, (chunk, chunk), 0)
    col = lax.broadcasted_iota(jnp.int32, (chunk, chunk), 1)
    lower, upper = col <= row, col >= row
    dirs = [_mlstm_gates(gf_ref, lower, chunk) + (qf_ref, kf_ref, vtf_ref, hf_ref, c_scr.at[0], m_scr.at[0], upper),
            _mlstm_gates(gb_ref, upper, chunk) + (qb_ref, kb_ref, vtb_ref, hb_ref, c_scr.at[1], m_scr.at[1], lower)]
    ones = jnp.ones((MLSTM_SUM_ROWS, chunk), BF16)
    chains = [(d, h) for h in range(HEADS) for d in dirs]

    stage1 = []
    for (r_all, rows, q_ref, k_ref, vt_ref, h_ref, cs, ms, mask_t), h in chains:
        sl = slice(h * HEAD_W, (h + 1) * HEAD_W)
        m_prev = ms[h:h + 1, 0:1]
        rm = jnp.where(mask_t, r_all[:, h:h + 1], NEG)
        mx = jnp.maximum(jnp.max(rm, axis=0, keepdims=True), m_prev)
        st = (_dot_nt(k_ref[0, :, sl], q_ref[0, :, sl]) * jnp.exp(rm - mx)).astype(BF16)
        stage1.append((m_prev, mx, st))

    for ((r_all, rows, q_ref, k_ref, vt_ref, h_ref, cs, ms, mask_t), h), (m_prev, mx, st) in zip(chains, stage1):
        sl = slice(h * HEAD_W, (h + 1) * HEAD_W)
        qh = q_ref[0, :, sl]
        kh = k_ref[0, :, sl]
        vext = jnp.concatenate([vt_ref[0, sl, :], ones], axis=0)
        r_row = rows[h:h + 1, :]
        a_row = rows[HEADS + h:HEADS + h + 1, :]
        ce = cs[h]
        nd = _dot(vext, st) + jnp.exp(m_prev - mx) * _dot_nt(ce.astype(BF16), qh)
        den = jnp.maximum(jnp.abs(nd[HEAD_W:HEAD_W + 1]), jnp.exp(-(a_row + mx)))
        h_ref[0, sl, :] = (nd[:HEAD_W] / den).astype(BF16)

    for ((r_all, rows, q_ref, k_ref, vt_ref, h_ref, cs, ms, mask_t), h), (m_prev, mx, st) in zip(chains, stage1):
        sl = slice(h * HEAD_W, (h + 1) * HEAD_W)
        kh = k_ref[0, :, sl]
        vext = jnp.concatenate([vt_ref[0, sl, :], ones], axis=0)
        r_row = rows[h:h + 1, :]
        a_row = rows[HEADS + h:HEADS + h + 1, :]
        ce = cs[h]
        mx_last = jnp.max(mx, axis=1, keepdims=True)
        a_last = jnp.min(a_row, axis=1, keepdims=True)
        wv = (vext.astype(F32) * jnp.exp(r_row - mx_last)).astype(BF16)
        cs[h] = jnp.exp(m_prev - mx_last) * ce + _dot(wv, kh)
        ms[h:h + 1, :] = jnp.broadcast_to(a_last + mx_last, (1, 128))


def _mlstm_gates(g_ref, before, chunk):
    gates = g_ref[0]
    logf = jnp.minimum(gates, 0.0) - jnp.log(1.0 + jnp.exp(-jnp.abs(gates)))
    csum = before.astype(BF16)
    f_hi = logf.astype(BF16)
    f_mid = (logf - f_hi.astype(F32)).astype(BF16)
    f_lo = (logf - f_hi.astype(F32) - f_mid.astype(F32)).astype(BF16)
    a_all = _dot(csum, f_hi) + _dot(csum, f_mid) + _dot(csum, f_lo)
    r_all = gates - pltpu.roll(a_all, 128 - HEADS, axis=1)
    lane = lax.broadcasted_iota(jnp.int32, (chunk, 128), 1)
    rows = jnp.where(lane < HEADS, r_all, a_all).T
    return r_all, rows


def _mlstm(q, k, vt, gates, nlat_rows):
    b, ttot, _ = q.shape
    L = MLSTM_CHUNK
    nc = ttot // L
    ncl = nlat_rows // L
    ncc = nc - ncl

    def fwd(c):
        return jnp.where(c < ncc, ncl + c, c - ncc)

    def bwd(c):
        return jnp.where(c < ncc, nc - 1 - c, ncl - 1 - (c - ncc))

    def specs(blk, d):
        head = pl.BlockSpec((1, L, D_HEADS), lambda bi, c: (bi, blk(c), 0))
        return [head, head, pl.BlockSpec((1, D_HEADS, L), lambda bi, c: (bi, 0, blk(c))),
                pl.BlockSpec((1, L, 128), lambda bi, c: (bi, blk(c), d))]

    out = lambda blk: pl.BlockSpec((1, D_HEADS, L), lambda bi, c: (bi, 0, blk(c)))
    return pl.pallas_call(
        functools.partial(_mlstm_kernel, chunk=L), name="mlstm_scan",
        grid=(b, nc),
        in_specs=specs(fwd, 0) + specs(bwd, 1),
        out_specs=[out(fwd), out(bwd)],
        out_shape=[jax.ShapeDtypeStruct((b, D_HEADS, ttot), BF16)] * 2,
        scratch_shapes=[pltpu.VMEM((2, HEADS, HEAD_W + MLSTM_SUM_ROWS, HEAD_W), F32),
                        pltpu.VMEM((2, 8, 128), F32)],
        compiler_params=_cparams("parallel", "arbitrary"),
    )(q, k, vt, gates, q, k, vt, gates)


def _fft1_kernel(p_ref, g_ref, o_ref):
    xt = jnp.swapaxes(p_ref[0, :, 0, :, :], 0, 1)
    out = []
    for r in range(SUBLANES):
        x = xt[r].astype(BF16)
        out.append(_dot(g_ref[r].astype(BF16), jnp.concatenate([x[:, :256], x[:, 256:]], axis=0)))
    bb = jnp.swapaxes(jnp.stack(out), 0, 1)
    o_ref[0, 0] = bb[:FFT_N1]
    o_ref[0, 1] = bb[FFT_N1:]


def _fft2_kernel(b_ref, t_ref, o_ref, *, kb, scale):
    tc = t_ref[0].astype(BF16)
    ts = t_ref[1].astype(BF16)
    ys = [(_dot(tc, b_ref[0, 0, j].astype(BF16)) + _dot(ts, b_ref[0, 1, j].astype(BF16))) * scale
          for j in range(kb)]
    o_ref[0] = jnp.swapaxes(jnp.stack(ys), 0, 1)


def _dft_ctx_kernel(p_ref, t_ref, o_ref, *, scale):
    p = p_ref[0].astype(BF16)
    st = jnp.concatenate([p[:, :256], p[:, 256:]], axis=0)
    o_ref[0] = _dot(t_ref[...].astype(BF16), st) * scale


def _fft_tables(t):
    n1, n2 = FFT_N1, t // FFT_N1
    k1 = np.arange(n1, dtype=np.int64)[None, :, None]
    nn = (n2 * np.arange(n1, dtype=np.int64)[None, None, :] + np.arange(n2, dtype=np.int64)[:, None, None])
    ang = 2.0 * np.pi * ((k1 * nn) % t).astype(np.float64) / t
    gr, gi = np.cos(ang), -np.sin(ang)
    g = np.concatenate([np.concatenate([gr, -gi], axis=2), np.concatenate([gi, gr], axis=2)], axis=1)
    a2 = 2.0 * np.pi * ((np.arange(n2)[:, None] * np.arange(n2)[None, :]) % n2) / n2
    t2 = np.stack([np.cos(a2), np.sin(a2)])
    return jnp.asarray(g, F32), jnp.asarray(t2, F32)


def _dft_matrix_cs(n):
    a = 2.0 * np.pi * ((np.arange(n)[:, None] * np.arange(n)[None, :]) % n) / n
    return np.cos(a), np.sin(a)


def _channel_dft():
    c, s = _dft_matrix_cs(FNET_GC)
    eye = np.eye(D_SIDE // FNET_GC)
    return jnp.asarray(np.concatenate([np.kron(eye, c), -np.kron(eye, s)], axis=1), F32)


def _fourier(p, t, ctx_len):
    b = p.shape[0]
    n1, n2 = FFT_N1, t // FFT_N1
    g, t2 = _fft_tables(t)
    kb = 16
    ttot = p.shape[1]
    p5 = p.reshape(b, ttot // n2, n2 // SUBLANES, SUBLANES, 512)
    st1 = pl.pallas_call(
        _fft1_kernel, name="fft_stage1",
        grid=(b, n2 // SUBLANES),
        in_specs=[pl.BlockSpec((1, n1, 1, SUBLANES, 512), lambda bi, j: (bi, 0, j, 0, 0)),
                  pl.BlockSpec((SUBLANES, 256, 256), lambda bi, j: (j, 0, 0))],
        out_specs=pl.BlockSpec((1, 2, n1, SUBLANES, 256), lambda bi, j: (bi, 0, 0, j, 0)),
        out_shape=jax.ShapeDtypeStruct((b, 2, n1, n2, 256), F32),
        compiler_params=_cparams("parallel", "parallel"),
    )(p5, g)
    y = pl.pallas_call(
        functools.partial(_fft2_kernel, kb=kb, scale=1.0 / math.sqrt(t * FNET_GC)), name="fft_stage2",
        grid=(b, n1 // kb),
        in_specs=[pl.BlockSpec((1, 2, kb, n2, 256), lambda bi, j: (bi, 0, j, 0, 0)),
                  _const_spec((2, n2, n2))],
        out_specs=pl.BlockSpec((1, n2, kb, 256), lambda bi, j: (bi, 0, j, 0)),
        out_shape=jax.ShapeDtypeStruct((b, n2, n1, 256), F32),
        compiler_params=_cparams("parallel", "parallel"),
    )(st1, t2)
    y_lat = y.reshape(b, t, 256)
    cc, sc = _dft_matrix_cs(ctx_len)
    tc = jnp.asarray(np.concatenate([cc, sc], axis=1), F32)
    y_ctx = pl.pallas_call(
        functools.partial(_dft_ctx_kernel, scale=1.0 / math.sqrt(ctx_len * FNET_GC)), name="dft_ctx",
        grid=(b,),
        in_specs=[pl.BlockSpec((1, ctx_len, 512), lambda bi: (bi, t // ctx_len, 0)),
                  _const_spec((ctx_len, 2 * ctx_len))],
        out_specs=pl.BlockSpec((1, ctx_len, 256), lambda bi: (bi, 0, 0)),
        out_shape=jax.ShapeDtypeStruct((b, ctx_len, 256), F32),
        compiler_params=_cparams("parallel"),
    )(p, tc)
    return y_lat, y_ctx


def _even_finish_kernel(hf_ref, hb_ref, o_ref, yl_ref, yc_ref, x_ref, ctx_ref, mod_ref, gh_ref, w_ref,
                        g2_ref, w1_ref, w2_ref, out_ref, *, nlat):
    i = pl.program_id(1)
    hs = (hf_ref[0].astype(F32) + hb_ref[0].astype(F32)).T
    gate = _sigmoid(o_ref[0])
    gh = gh_ref[...]
    parts = []
    for h in range(HEADS):
        sl = slice(h * HEAD_W, (h + 1) * HEAD_W)
        parts.append((_rms(hs[:, sl], gh[:, sl]) * gate[:, sl]).astype(BF16))
    y = jnp.where(i >= nlat, yc_ref[0], yl_ref[0])
    parts.append(y.astype(BF16))
    out = _dot(jnp.concatenate(parts, axis=1), w_ref[...])
    m = mod_ref[0]
    x1 = _tile_rows(x_ref, ctx_ref, nlat) + m[2:3] * out
    out_ref[0] = _mlp_rows(x1, m, g2_ref[...], w1_ref, w2_ref)


def _mlp_rows(x, m, g, w1_ref, w2_ref):
    h = _norm_mod(x, g, m[3:4], m[4:5]).astype(BF16)
    a = jnp.maximum(_dot(h, w1_ref[...]), 0.0)
    return x + m[5:6] * _dot((a * a).astype(BF16), w2_ref[...])


def _resident_spec(shape):
    nd = len(shape)
    return pl.BlockSpec(shape, lambda *_: (0,) * nd, pipeline_mode=pl.Buffered(1))


def _even_finish(hf, hb, o, y_lat, y_ctx, x, ctx, mod, g_head, w_out, g2, w1, w2, nlat):
    b, t, d = x.shape
    ttot = t + ctx.shape[1]
    tm = ROW_TILE
    nt = ttot // tm
    row = lambda w: pl.BlockSpec((1, tm, w), lambda bi, i: (bi, i, 0))
    return pl.pallas_call(
        functools.partial(_even_finish_kernel, nlat=nlat), name="even_finish_mlp",
        grid=(b, nt),
        in_specs=[pl.BlockSpec((1, D_HEADS, tm), lambda bi, i: (bi, 0, i)),
                  pl.BlockSpec((1, D_HEADS, tm), lambda bi, i: (bi, 0, i)),
                  row(D_HEADS),
                  pl.BlockSpec((1, tm, 256), lambda bi, i: (bi, jnp.minimum(i, nlat - 1), 0)),
                  pl.BlockSpec((1, tm, 256), lambda bi, i: (bi, jnp.maximum(i - nlat, 0), 0))]
        + _lat_ctx_specs(d, tm, nlat)
        + [_mod_spec(b, nlat), _const_spec((1, D_HEADS)), _resident_spec((d, d)),
           _const_spec((1, d)), _resident_spec(w1.shape), _resident_spec(w2.shape)],
        out_specs=row(d),
        out_shape=jax.ShapeDtypeStruct((b, ttot, d), F32),
        compiler_params=_cparams("parallel", "parallel"),
    )(hf, hb, o, y_lat, y_ctx, x, ctx, mod, g_head, w_out, g2, w1, w2)


def _swap_halves(x):
    lane = lax.broadcasted_iota(jnp.int32, x.shape, 1)
    return jnp.where(lane % 64 < 32, pltpu.roll(x, 96, axis=1), pltpu.roll(x, 32, axis=1))


def _proj_odd_kernel(x_ref, mod_ref, g_ref, w_ref, cos_ref, sin_ref, q_ref, k_ref, vt_ref, glu_ref):
    m = mod_ref[0]
    h = _norm_mod(x_ref[0], g_ref[...], m[0:1], m[1:2]).astype(BF16)
    u = _dot(h, w_ref[...])
    cos = cos_ref[...]
    sin = sin_ref[...]
    for hh in range(HEADS):
        sl = slice(hh * HEAD_W, (hh + 1) * HEAD_W)
        qh = u[:, sl]
        kh = u[:, D_HEADS + hh * HEAD_W:D_HEADS + (hh + 1) * HEAD_W]
        q_ref[0, :, sl] = ((qh * cos + _swap_halves(qh) * sin) * Q_SCALE_LOG2).astype(BF16)
        k_ref[0, :, sl] = (kh * cos + _swap_halves(kh) * sin).astype(BF16)
    vt_ref[0] = u[:, 2 * D_HEADS:3 * D_HEADS].T.astype(BF16)
    a = u[:, 3 * D_HEADS:3 * D_HEADS + D_SIDE]
    gte = u[:, 3 * D_HEADS + D_SIDE:]
    glu_ref[0] = a * _sigmoid(gte)


def _rope_tables(t, ctx_len):
    rows = t // GRID_W
    row = np.repeat(np.arange(rows, dtype=np.float64), GRID_W)
    col = np.tile(np.arange(GRID_W, dtype=np.float64), rows)
    n_freq = DIFF_DH // 4
    inv = ROPE_BASE ** (-np.arange(n_freq, dtype=np.float64) / n_freq)
    ang = np.concatenate([row[:, None] * inv, col[:, None] * inv], axis=-1)
    cos, sin = np.cos(ang), np.sin(ang)
    cos = np.concatenate([cos, cos, cos, cos], axis=-1)
    sin = np.concatenate([-sin, sin, -sin, sin], axis=-1)
    cos = np.concatenate([cos, np.ones((ctx_len, 128))], axis=0)
    sin = np.concatenate([sin, np.zeros((ctx_len, 128))], axis=0)
    return jnp.asarray(cos, F32), jnp.asarray(sin, F32)


def _proj_odd(x, mod, g, w_all, cos, sin, nlat):
    b, ttot, d = x.shape
    tm = ROW_TILE
    nt = ttot // tm
    row = lambda w: pl.BlockSpec((1, tm, w), lambda bi, i: (bi, i, 0))
    tab = pl.BlockSpec((tm, 128), lambda bi, i: (i, 0))
    return pl.pallas_call(
        _proj_odd_kernel, name="proj_odd",
        grid=(b, nt),
        in_specs=[row(d), _mod_spec(b, nlat), _const_spec((1, d)), _const_spec(w_all.shape), tab, tab],
        out_specs=[row(D_HEADS), row(D_HEADS),
                   pl.BlockSpec((1, D_HEADS, tm), lambda bi, i: (bi, 0, i)), row(D_SIDE)],
        out_shape=[jax.ShapeDtypeStruct((b, ttot, D_HEADS), BF16)] * 2
        + [jax.ShapeDtypeStruct((b, D_HEADS, ttot), BF16), jax.ShapeDtypeStruct((b, ttot, D_SIDE), F32)],
        compiler_params=_cparams("parallel", "parallel"),
    )(x, mod, g, w_all, cos, sin)


def _attn_kernel(q_ref, k_ref, vt_ref, lam_ref, gs_ref, o_ref, acc0, acc1, *bufs, tk, nk, lam_init):
    q = q_ref[0]
    q0 = q[:, :DIFF_DH]
    q1 = q[:, DIFF_DH:]
    tq = q.shape[0]
    ones = jnp.ones((ATTN_SUM_ROWS, tk), BF16)
    acc0[...] = jnp.zeros_like(acc0)
    acc1[...] = jnp.zeros_like(acc1)

    def online(st, m_old, acc, vext):
        m_new = jnp.maximum(m_old, jnp.max(st, axis=0, keepdims=True))
        p = jnp.exp2(st - m_new).astype(BF16)
        acc[...] = jnp.exp2(m_old - m_new) * acc[...] + _dot(vext, p)
        return m_new

    def scores(j, s_buf):
        kk = k_ref[0, j * tk:(j + 1) * tk, :]
        s_buf[0] = _dot_nt(kk[:, :DIFF_DH], q0)
        s_buf[1] = _dot_nt(kk[:, DIFF_DH:], q1)

    def consume(j, s_buf, m0, m1):
        vext = jnp.concatenate([vt_ref[0, :, j * tk:(j + 1) * tk], ones], axis=0)
        return online(s_buf[0], m0, acc0, vext), online(s_buf[1], m1, acc1, vext)

    init = jnp.full((1, tq), NEG, F32)
    carry = (init, init)
    ahead = len(bufs) - 1
    for j in range(min(ahead, nk)):
        scores(j, bufs[j % len(bufs)])
    for j in range(nk):
        if j + ahead < nk:
            scores(j + ahead, bufs[(j + ahead) % len(bufs)])
        carry = consume(j, bufs[j % len(bufs)], *carry)
    lp = lam_ref[...]
    lam = (jnp.exp(jnp.sum(lp[0:1] * lp[1:2], axis=1, keepdims=True))
           - jnp.exp(jnp.sum(lp[2:3] * lp[3:4], axis=1, keepdims=True)) + lam_init)
    a0 = acc0[...]
    a1 = acc1[...]
    o = a0[:HEAD_W] / a0[HEAD_W:HEAD_W + 1] - lam * (a1[:HEAD_W] / a1[HEAD_W:HEAD_W + 1])
    o = o * lax.rsqrt(jnp.mean(o * o, axis=0, keepdims=True) + EPS)
    o_ref[0] = (o.T * gs_ref[...] * (1.0 - lam_init)).astype(BF16)


def _diff_attention(q, k, v, lam_p, g_sub, t, lam_init):
    b, ttot, _ = q.shape
    tq = ATTN_TQ
    tk = max(n for n in range(256, ATTN_TK + 1, 256) if ttot % n == 0)
    return pl.pallas_call(
        functools.partial(_attn_kernel, tk=tk, nk=ttot // tk, lam_init=lam_init), name="diff_attn",
        grid=(b, HEADS, t // tq),
        in_specs=[pl.BlockSpec((1, tq, HEAD_W), lambda bi, h, i: (bi, i, h)),
                  pl.BlockSpec((1, ttot, HEAD_W), lambda bi, h, i: (bi, 0, h)),
                  pl.BlockSpec((1, HEAD_W, ttot), lambda bi, h, i: (bi, h, 0)),
                  _const_spec(lam_p.shape), _const_spec((1, HEAD_W))],
        out_specs=pl.BlockSpec((1, tq, HEAD_W), lambda bi, h, i: (bi, i, h)),
        out_shape=jax.ShapeDtypeStruct((b, t, D_HEADS), BF16),
        scratch_shapes=[pltpu.VMEM((HEAD_W + ATTN_SUM_ROWS, tq), F32)] * 2
        + [pltpu.VMEM((2, tk, tq), F32)] * ATTN_SCORE_BUFS,
        compiler_params=_cparams("parallel", "parallel", "parallel"),
    )(q, k, v, lam_p, g_sub)


def _odd_finish_kernel(o_ref, glu_ref, prev_ref, next_ref, wdw_ref, gln_ref, bln_ref, x_ref, mod_ref,
                       w_ref, g2_ref, w1_ref, w2_ref, gf_ref, out_ref, xs, *, tm, nlat):
    i = pl.program_id(1)
    hal = CONV_HALO
    xs[0:hal, :] = jnp.where(i == 0, 0.0, prev_ref[0])
    xs[hal:hal + tm, :] = glu_ref[0]
    xs[hal + tm:, :] = jnp.where(i == nlat - 1, 0.0, next_ref[0])
    wdw = wdw_ref[...]
    z = jnp.zeros((tm, D_SIDE), F32)
    off = hal - (CONV_W - 1) // 2
    for r in range(SUBLANES):
        u = None
        for a in range((off + CONV_W - 1) // SUBLANES + 1):
            kk = SUBLANES * a + r - off
            if 0 <= kk < CONV_W:
                term = xs[SUBLANES * a:SUBLANES * a + tm + SUBLANES, :] * wdw[kk:kk + 1]
                u = term if u is None else u + term
        z = z + u[r:r + tm]
    mu = jnp.mean(z, axis=-1, keepdims=True)
    zc = z - mu
    var = jnp.mean(zc * zc, axis=-1, keepdims=True)
    z = zc * lax.rsqrt(var + EPS) * gln_ref[...] + bln_ref[...]
    z = z * _sigmoid(z)
    cat = jnp.concatenate([o_ref[0], z.astype(BF16)], axis=1)
    m = mod_ref[0]
    x3 = x_ref[0] + m[2:3] * _dot(cat, w_ref[...])
    out_ref[0] = _rms(_mlp_rows(x3, m, g2_ref[...], w1_ref, w2_ref), gf_ref[...])


def _odd_finish(o, glu, w_dw, g_ln, b_ln, x, mod, w_out, g2, w1, w2, g_final, t):
    b, _, d = x.shape
    tm = LATENT_ROW_TILE
    nlat = t // tm
    r = tm // CONV_HALO
    row = lambda w: pl.BlockSpec((1, tm, w), lambda bi, i: (bi, i, 0))
    return pl.pallas_call(
        functools.partial(_odd_finish_kernel, tm=tm, nlat=nlat), name="odd_finish_mlp",
        grid=(b, nlat),
        in_specs=[row(D_HEADS), row(D_SIDE),
                  pl.BlockSpec((1, CONV_HALO, D_SIDE), lambda bi, i: (bi, jnp.maximum(i * r - 1, 0), 0)),
                  pl.BlockSpec((1, CONV_HALO, D_SIDE), lambda bi, i: (bi, (i + 1) * r, 0)),
                  _const_spec((CONV_W, D_SIDE)), _const_spec((1, D_SIDE)), _const_spec((1, D_SIDE)),
                  row(d), _mod_spec(b, nlat), _resident_spec((d, d)),
                  _const_spec((1, d)), _resident_spec(w1.shape), _resident_spec(w2.shape),
                  _const_spec((1, d))],
        out_specs=row(d),
        out_shape=jax.ShapeDtypeStruct((b, t, d), F32),
        scratch_shapes=[pltpu.VMEM((tm + 2 * CONV_HALO, D_SIDE), F32)],
        compiler_params=_cparams("parallel", "parallel"),
    )(o, glu, glu, glu, w_dw, g_ln, b_ln, x, mod, w_out, g2, w1, w2, g_final)


def kernel(x, c, ctx, c_ctx, w_mod, b_mod, g_norm, w_in_even, b_gate, w_qk_conv, g_mlstm_head,
           w_out_even, w_in_odd, lam_p, g_subln, w_dw, g_conv_ln, b_conv_ln, w_out_odd,
           w_ff1, w_ff2, g_final):
    b, t, d = x.shape
    ctx_len = ctx.shape[1]
    assert w_mod.shape[0] == 2 and d == 1024 and ctx_len == ROW_TILE and t % (FFT_N1 * 8) == 0
    nlat = t // ROW_TILE
    mod = _mod_vectors(c, c_ctx, w_mod, b_mod)

    we = w_in_even[0]
    g0 = 4 * D_HEADS
    w_gate = jnp.zeros((d, 256), F32).at[:, 0:12].set(we[:, g0:g0 + 12]).at[:, 128:140].set(we[:, g0 + 12:g0 + 24])
    w_all = jnp.concatenate([we[:, :g0], w_gate, we[:, g0 + 24:]], axis=1).astype(BF16)
    bg = jnp.zeros((1, 256), F32).at[0, 0:12].set(b_gate[0, :12]).at[0, 128:140].set(b_gate[0, 12:])
    q, k, v, o, gates, p = _proj_even(x, ctx, mod[0], g_norm[0, 0][None], w_all, bg, _channel_dft(),
                                      w_qk_conv[0], nlat)
    hf, hb = _mlstm(q, k, v, gates, t)
    y_lat, y_ctx = _fourier(p, t, ctx_len)
    x2 = _even_finish(hf, hb, o, y_lat, y_ctx, x, ctx, mod[0], g_mlstm_head[0][None],
                      w_out_even[0].astype(BF16), g_norm[0, 1][None], w_ff1[0].astype(BF16),
                      w_ff2[0].astype(BF16), nlat)

    lam_init = 0.8 - 0.6 * math.exp(-0.3 * 1)
    cos, sin = _rope_tables(t, ctx_len)
    qa, ka, va, glu = _proj_odd(x2, mod[1], g_norm[1, 0][None], w_in_odd[0].astype(BF16), cos, sin, nlat)
    oa = _diff_attention(qa, ka, va, lam_p[0], g_subln[0][None], t, lam_init)
    return _odd_finish(oa, glu, w_dw[0], g_conv_ln[0][None], b_conv_ln[0][None], x2, mod[1],
                       w_out_odd[0].astype(BF16), g_norm[1, 1][None], w_ff1[1].astype(BF16),
                       w_ff2[1].astype(BF16), g_final[None], t)
```

```python
import functools
import math

import numpy as np
import jax
import jax.numpy as jnp
from jax import lax
from jax.experimental import pallas as pl
from jax.experimental.pallas import tpu as pltpu

F32 = jnp.float32
BF16 = jnp.bfloat16

EPS = 1e-6
SUBLANES = 8
HEADS = 6
HEAD_W = 128
D_HEADS = HEADS * HEAD_W
D_SIDE = 256
FNET_GC = 64
QK_W = 2 * D_HEADS
CONV_W = 31
CONV_HALO = 16
GRID_W = 64
ROPE_BASE = 10000.0
DIFF_DH = 64
FFT_N1 = 128
ROW_TILE = 256
LATENT_ROW_TILE = 512
MLSTM_CHUNK = 256
MLSTM_SUM_ROWS = 16
ATTN_TQ = 1024
ATTN_TK = 768
ATTN_SCORE_BUFS = 2
ATTN_SUM_ROWS = 16
NEG = -1e30
Q_SCALE_LOG2 = (DIFF_DH ** -0.5) * math.log2(math.e)
VMEM_LIMIT = 56 * 1024 * 1024


def _cparams(*sem, flags=None):
    return pltpu.CompilerParams(dimension_semantics=sem, vmem_limit_bytes=VMEM_LIMIT, flags=flags)


def _dot(a, b):
    return jnp.dot(a, b, preferred_element_type=F32)


def _dot_nt(a, b):
    return lax.dot_general(a, b, (((1,), (1,)), ((), ())), preferred_element_type=F32)


def _dot_tn(a, b):
    return lax.dot_general(a, b, (((0,), (0,)), ((), ())), preferred_element_type=F32)


def _sigmoid(x):
    return 1.0 / (1.0 + jnp.exp(-x))


def _rms(x, g):
    return x * lax.rsqrt(jnp.mean(x * x, axis=-1, keepdims=True) + EPS) * g


def _norm_mod(x, g, shift, scale):
    return _rms(x, g) * (1.0 + scale) + shift


def _mod_kernel(s_ref, w_ref, b_ref, o_ref):
    s = s_ref[...]
    s = s * _sigmoid(s)
    o_ref[0] = _dot(s.astype(BF16), w_ref[0].astype(BF16)) + b_ref[0]


def _mod_vectors(c, c_ctx, w_mod, b_mod):
    depth, d, n = w_mod.shape
    b = c.shape[0]
    s = jnp.zeros((8, d), F32).at[:b].set(c).at[b].set(c_ctx)
    tn = 1536
    out = pl.pallas_call(
        _mod_kernel, name="adaln_mod",
        grid=(depth, n // tn),
        in_specs=[pl.BlockSpec((8, d), lambda l, j: (0, 0)),
                  pl.BlockSpec((1, d, tn), lambda l, j: (l, 0, j)),
                  pl.BlockSpec((1, 1, tn), lambda l, j: (l, 0, j))],
        out_specs=pl.BlockSpec((1, 8, tn), lambda l, j: (l, 0, j)),
        out_shape=jax.ShapeDtypeStruct((depth, 8, n), F32),
        compiler_params=_cparams("parallel", "parallel"),
    )(s, w_mod, b_mod.reshape(depth, 1, n))
    return out.reshape(depth, 8, 6, d)


def _mod_spec(b, nlat):
    return pl.BlockSpec((1, 6, 1024), lambda bi, i: (jnp.where(i >= nlat, b, bi), 0, 0))


def _const_spec(shape):
    nd = len(shape)
    return pl.BlockSpec(shape, lambda *_: (0,) * nd)


def _tile_rows(x_ref, ctx_ref, nlat):
    return jnp.where(pl.program_id(1) >= nlat, ctx_ref[0], x_ref[0])


def _lat_ctx_specs(d, tm, nlat):
    return [pl.BlockSpec((1, tm, d), lambda bi, i: (bi, jnp.minimum(i, nlat - 1), 0)),
            pl.BlockSpec((1, tm, d), lambda bi, i: (bi, jnp.maximum(i - nlat, 0), 0))]


def _proj_even_kernel(x_ref, ctx_ref, mod_ref, g_ref, w_ref, bg_ref, dft_ref, wc_ref,
                      q_ref, k_ref, vt_ref, o_ref, gt_ref, p_ref, xq_scr, pr_scr, *, tm, nlat):
    i = pl.program_id(1)

    @pl.when(i == 0)
    def _():
        xq_scr[...] = jnp.zeros_like(xq_scr)
        pr_scr[...] = jnp.zeros_like(pr_scr)

    m = mod_ref[0]
    g = g_ref[...]
    h = _norm_mod(_tile_rows(x_ref, ctx_ref, nlat), g, m[0:1], m[1:2]).astype(BF16)
    xq_new = _dot(h, w_ref[:, :QK_W])
    u = _dot(h, w_ref[:, QK_W:])
    j = i - 1
    xq = xq_scr[...]
    prow = jnp.where(jnp.logical_or(j == 0, j >= nlat), 0.0, pr_scr[SUBLANES - 1:SUBLANES, :])
    nrow = jnp.where(j >= nlat - 1, 0.0, xq_new[0:1])
    pr_scr[...] = xq[tm - SUBLANES:, :]
    xq_scr[...] = xq_new
    rid = lax.broadcasted_iota(jnp.int32, xq.shape, 0)
    xm = jnp.where(rid == 0, prow, pltpu.roll(xq, 1, axis=0))
    xp = jnp.where(rid == tm - 1, nrow, pltpu.roll(xq, tm - 1, axis=0))
    wc = wc_ref[...]
    y = xm * wc[0:1] + xq * wc[1:2] + xp * wc[2:3]
    y = y * _sigmoid(y)
    q_ref[0] = (y[:, :D_HEADS] * (HEAD_W ** -0.5)).astype(BF16)
    k_ref[0] = y[:, D_HEADS:].astype(BF16)
    vt_ref[0] = u[:, :D_HEADS].T.astype(BF16)
    o_ref[0] = u[:, D_HEADS:2 * D_HEADS]
    c0 = 2 * D_HEADS
    gt_ref[0] = u[:, c0:c0 + 256] + bg_ref[...]
    f = u[:, c0 + 256:c0 + 512].astype(BF16)
    p_ref[0] = _dot(f, dft_ref[...].astype(BF16))


def _proj_even(x, ctx, mod, g, w_all, bg, dftc, w_conv, nlat):
    b, t, d = x.shape
    ttot = t + ctx.shape[1]
    tm = ROW_TILE
    nt = ttot // tm
    cur = lambda i: jnp.minimum(i, nt - 1)
    row = lambda w: pl.BlockSpec((1, tm, w), lambda bi, i: (bi, cur(i), 0))
    late = pl.BlockSpec((1, tm, D_HEADS), lambda bi, i: (bi, jnp.maximum(i - 1, 0), 0))
    return pl.pallas_call(
        functools.partial(_proj_even_kernel, tm=tm, nlat=nlat), name="proj_even",
        grid=(b, nt + 1),
        in_specs=[pl.BlockSpec((1, tm, d), lambda bi, i: (bi, jnp.minimum(i, nlat - 1), 0)),
                  pl.BlockSpec((1, tm, d), lambda bi, i: (bi, jnp.clip(i - nlat, 0, nt - nlat - 1), 0)),
                  _mod_spec(b, nlat), _const_spec((1, d)), _const_spec(w_all.shape), _const_spec((1, 256)),
                  _const_spec(dftc.shape), _const_spec((3, QK_W))],
        out_specs=[late, late, pl.BlockSpec((1, D_HEADS, tm), lambda bi, i: (bi, 0, cur(i))),
                   row(D_HEADS), row(256), row(512)],
        out_shape=[jax.ShapeDtypeStruct((b, ttot, D_HEADS), BF16),
                   jax.ShapeDtypeStruct((b, ttot, D_HEADS), BF16),
                   jax.ShapeDtypeStruct((b, D_HEADS, ttot), BF16),
                   jax.ShapeDtypeStruct((b, ttot, D_HEADS), F32),
                   jax.ShapeDtypeStruct((b, ttot, 256), F32),
                   jax.ShapeDtypeStruct((b, ttot, 512), F32)],
        scratch_shapes=[pltpu.VMEM((tm, QK_W), F32), pltpu.VMEM((SUBLANES, QK_W), F32)],
        compiler_params=_cparams("parallel", "arbitrary"),
    )(x, ctx, mod, g, w_all, bg, dftc, w_conv)


def _mlstm_kernel(qf_ref, kf_ref, vtf_ref, gf_ref, qb_ref, kb_ref, vtb_ref, gb_ref, hf_ref, hb_ref,
                  c_scr, m_scr, *, chunk):
    c = pl.program_id(1)

    @pl.when(c == 0)
    def _():
        c_scr[...] = jnp.zeros_like(c_scr)
        m_scr[...] = jnp.zeros_like(m_scr)

    row = lax.broadcasted_iota(jnp.int32, (chunk, chunk), 0)
    col = lax.broadcasted_iota(jnp.int32, (chunk, chunk), 1)
    lower, upper = col <= row, col >= row
    dirs = [_mlstm_gates(gf_ref, lower, chunk) + (qf_ref, kf_ref, vtf_ref, hf_ref, c_scr.at[0], m_scr.at[0], upper),
            _mlstm_gates(gb_ref, upper, chunk) + (qb_ref, kb_ref, vtb_ref, hb_ref, c_scr.at[1], m_scr.at[1], lower)]
    ones = jnp.ones((MLSTM_SUM_ROWS, chunk), BF16)
    chains = [(d, h) for h in range(HEADS) for d in dirs]

    stage1 = []
    for (r_all, rows, q_ref, k_ref, vt_ref, h_ref, cs, ms, mask_t), h in chains:
        sl = slice(h * HEAD_W, (h + 1) * HEAD_W)
        m_prev = ms[h:h + 1, 0:1]
        rm = jnp.where(mask_t, r_all[:, h:h + 1], NEG)
        mx = jnp.maximum(jnp.max(rm, axis=0, keepdims=True), m_prev)
        st = (_dot_nt(k_ref[0, :, sl], q_ref[0, :, sl]) * jnp.exp(rm - mx)).astype(BF16)
        stage1.append((m_prev, mx, st))

    for ((r_all, rows, q_ref, k_ref, vt_ref, h_ref, cs, ms, mask_t), h), (m_prev, mx, st) in zip(chains, stage1):
        sl = slice(h * HEAD_W, (h + 1) * HEAD_W)
        qh = q_ref[0, :, sl]
        kh = k_ref[0, :, sl]
        vext = jnp.concatenate([vt_ref[0, sl, :], ones], axis=0)
        r_row = rows[h:h + 1, :]
        a_row = rows[HEADS + h:HEADS + h + 1, :]
        ce = cs[h]
        nd = _dot(vext, st) + jnp.exp(m_prev - mx) * _dot_nt(ce.astype(BF16), qh)
        den = jnp.maximum(jnp.abs(nd[HEAD_W:HEAD_W + 1]), jnp.exp(-(a_row + mx)))
        h_ref[0, sl, :] = (nd[:HEAD_W] / den).astype(BF16)

    for ((r_all, rows, q_ref, k_ref, vt_ref, h_ref, cs, ms, mask_t), h), (m_prev, mx, st) in zip(chains, stage1):
        sl = slice(h * HEAD_W, (h + 1) * HEAD_W)
        kh = k_ref[0, :, sl]
        vext = jnp.concatenate([vt_ref[0, sl, :], ones], axis=0)
        r_row = rows[h:h + 1, :]
        a_row = rows[HEADS + h:HEADS + h + 1, :]
        ce = cs[h]
        mx_last = jnp.max(mx, axis=1, keepdims=True)
        a_last = jnp.min(a_row, axis=1, keepdims=True)
        wv = (vext.astype(F32) * jnp.exp(r_row - mx_last)).astype(BF16)
        cs[h] = jnp.exp(m_prev - mx_last) * ce + _dot(wv, kh)
        ms[h:h + 1, :] = jnp.broadcast_to(a_last + mx_last, (1, 128))


def _mlstm_gates(g_ref, before, chunk):
    gates = g_ref[0]
    logf = jnp.minimum(gates, 0.0) - jnp.log(1.0 + jnp.exp(-jnp.abs(gates)))
    csum = before.astype(BF16)
    f_hi = logf.astype(BF16)
    f_mid = (logf - f_hi.astype(F32)).astype(BF16)
    f_lo = (logf - f_hi.astype(F32) - f_mid.astype(F32)).astype(BF16)
    a_all = _dot(csum, f_hi) + _dot(csum, f_mid) + _dot(csum, f_lo)
    r_all = gates - pltpu.roll(a_all, 128 - HEADS, axis=1)
    lane = lax.broadcasted_iota(jnp.int32, (chunk, 128), 1)
    rows = jnp.where(lane < HEADS, r_all, a_all).T
    return r_all, rows


def _mlstm(q, k, vt, gates, nlat_rows):
    b, ttot, _ = q.shape
    L = MLSTM_CHUNK
    nc = ttot // L
    ncl = nlat_rows // L
    ncc = nc - ncl

    def fwd(c):
        return jnp.where(c < ncc, ncl + c, c - ncc)

    def bwd(c):
        return jnp.where(c < ncc, nc - 1 - c, ncl - 1 - (c - ncc))

    def specs(blk, d):
        head = pl.BlockSpec((1, L, D_HEADS), lambda bi, c: (bi, blk(c), 0))
        return [head, head, pl.BlockSpec((1, D_HEADS, L), lambda bi, c: (bi, 0, blk(c))),
                pl.BlockSpec((1, L, 128), lambda bi, c: (bi, blk(c), d))]

    out = lambda blk: pl.BlockSpec((1, D_HEADS, L), lambda bi, c: (bi, 0, blk(c)))
    return pl.pallas_call(
        functools.partial(_mlstm_kernel, chunk=L), name="mlstm_scan",
        grid=(b, nc),
        in_specs=specs(fwd, 0) + specs(bwd, 1),
        out_specs=[out(fwd), out(bwd)],
        out_shape=[jax.ShapeDtypeStruct((b, D_HEADS, ttot), BF16)] * 2,
        scratch_shapes=[pltpu.VMEM((2, HEADS, HEAD_W + MLSTM_SUM_ROWS, HEAD_W), F32),
                        pltpu.VMEM((2, 8, 128), F32)],
        compiler_params=_cparams("parallel", "arbitrary"),
    )(q, k, vt, gates, q, k, vt, gates)


def _fft1_kernel(p_ref, g_ref, o_ref):
    xt = jnp.swapaxes(p_ref[0, :, 0, :, :], 0, 1)
    out = []
    for r in range(SUBLANES):
        x = xt[r].astype(BF16)
        out.append(_dot(g_ref[r].astype(BF16), jnp.concatenate([x[:, :256], x[:, 256:]], axis=0)))
    bb = jnp.swapaxes(jnp.stack(out), 0, 1)
    o_ref[0, 0] = bb[:FFT_N1]
    o_ref[0, 1] = bb[FFT_N1:]


def _fft2_kernel(b_ref, t_ref, o_ref, *, kb, scale):
    tc = t_ref[0].astype(BF16)
    ts = t_ref[1].astype(BF16)
    ys = [(_dot(tc, b_ref[0, 0, j].astype(BF16)) + _dot(ts, b_ref[0, 1, j].astype(BF16))) * scale
          for j in range(kb)]
    o_ref[0] = jnp.swapaxes(jnp.stack(ys), 0, 1)


def _dft_ctx_kernel(p_ref, t_ref, o_ref, *, scale):
    p = p_ref[0].astype(BF16)
    st = jnp.concatenate([p[:, :256], p[:, 256:]], axis=0)
    o_ref[0] = _dot(t_ref[...].astype(BF16), st) * scale


def _fft_tables(t):
    n1, n2 = FFT_N1, t // FFT_N1
    k1 = np.arange(n1, dtype=np.int64)[None, :, None]
    nn = (n2 * np.arange(n1, dtype=np.int64)[None, None, :] + np.arange(n2, dtype=np.int64)[:, None, None])
    ang = 2.0 * np.pi * ((k1 * nn) % t).astype(np.float64) / t
    gr, gi = np.cos(ang), -np.sin(ang)
    g = np.concatenate([np.concatenate([gr, -gi], axis=2), np.concatenate([gi, gr], axis=2)], axis=1)
    a2 = 2.0 * np.pi * ((np.arange(n2)[:, None] * np.arange(n2)[None, :]) % n2) / n2
    t2 = np.stack([np.cos(a2), np.sin(a2)])
    return jnp.asarray(g, F32), jnp.asarray(t2, F32)


def _dft_matrix_cs(n):
    a = 2.0 * np.pi * ((np.arange(n)[:, None] * np.arange(n)[None, :]) % n) / n
    return np.cos(a), np.sin(a)


def _channel_dft():
    c, s = _dft_matrix_cs(FNET_GC)
    eye = np.eye(D_SIDE // FNET_GC)
    return jnp.asarray(np.concatenate([np.kron(eye, c), -np.kron(eye, s)], axis=1), F32)


def _fourier(p, t, ctx_len):
    b = p.shape[0]
    n1, n2 = FFT_N1, t // FFT_N1
    g, t2 = _fft_tables(t)
    kb = 16
    ttot = p.shape[1]
    p5 = p.reshape(b, ttot // n2, n2 // SUBLANES, SUBLANES, 512)
    st1 = pl.pallas_call(
        _fft1_kernel, name="fft_stage1",
        grid=(b, n2 // SUBLANES),
        in_specs=[pl.BlockSpec((1, n1, 1, SUBLANES, 512), lambda bi, j: (bi, 0, j, 0, 0)),
                  pl.BlockSpec((SUBLANES, 256, 256), lambda bi, j: (j, 0, 0))],
        out_specs=pl.BlockSpec((1, 2, n1, SUBLANES, 256), lambda bi, j: (bi, 0, 0, j, 0)),
        out_shape=jax.ShapeDtypeStruct((b, 2, n1, n2, 256), F32),
        compiler_params=_cparams("parallel", "parallel"),
    )(p5, g)
    y = pl.pallas_call(
        functools.partial(_fft2_kernel, kb=kb, scale=1.0 / math.sqrt(t * FNET_GC)), name="fft_stage2",
        grid=(b, n1 // kb),
        in_specs=[pl.BlockSpec((1, 2, kb, n2, 256), lambda bi, j: (bi, 0, j, 0, 0)),
                  _const_spec((2, n2, n2))],
        out_specs=pl.BlockSpec((1, n2, kb, 256), lambda bi, j: (bi, 0, j, 0)),
        out_shape=jax.ShapeDtypeStruct((b, n2, n1, 256), F32),
        compiler_params=_cparams("parallel", "parallel"),
    )(st1, t2)
    y_lat = y.reshape(b, t, 256)
    cc, sc = _dft_matrix_cs(ctx_len)
    tc = jnp.asarray(np.concatenate([cc, sc], axis=1), F32)
    y_ctx = pl.pallas_call(
        functools.partial(_dft_ctx_kernel, scale=1.0 / math.sqrt(ctx_len * FNET_GC)), name="dft_ctx",
        grid=(b,),
        in_specs=[pl.BlockSpec((1, ctx_len, 512), lambda bi: (bi, t // ctx_len, 0)),
                  _const_spec((ctx_len, 2 * ctx_len))],
        out_specs=pl.BlockSpec((1, ctx_len, 256), lambda bi: (bi, 0, 0)),
        out_shape=jax.ShapeDtypeStruct((b, ctx_len, 256), F32),
        compiler_params=_cparams("parallel"),
    )(p, tc)
    return y_lat, y_ctx


def _even_finish_kernel(hf_ref, hb_ref, o_ref, yl_ref, yc_ref, x_ref, ctx_ref, mod_ref, gh_ref, w_ref,
                        g2_ref, w1_ref, w2_ref, out_ref, *, nlat):
    i = pl.program_id(1)
    hs = (hf_ref[0].astype(F32) + hb_ref[0].astype(F32)).T
    gate = _sigmoid(o_ref[0])
    gh = gh_ref[...]
    parts = []
    for h in range(HEADS):
        sl = slice(h * HEAD_W, (h + 1) * HEAD_W)
        parts.append((_rms(hs[:, sl], gh[:, sl]) * gate[:, sl]).astype(BF16))
    y = jnp.where(i >= nlat, yc_ref[0], yl_ref[0])
    parts.append(y.astype(BF16))
    out = _dot(jnp.concatenate(parts, axis=1), w_ref[...])
    m = mod_ref[0]
    x1 = _tile_rows(x_ref, ctx_ref, nlat) + m[2:3] * out
    out_ref[0] = _mlp_rows(x1, m, g2_ref[...], w1_ref, w2_ref)


def _mlp_rows(x, m, g, w1_ref, w2_ref):
    h = _norm_mod(x, g, m[3:4], m[4:5]).astype(BF16)
    a = jnp.maximum(_dot(h, w1_ref[...]), 0.0)
    return x + m[5:6] * _dot((a * a).astype(BF16), w2_ref[...])


def _resident_spec(shape):
    nd = len(shape)
    return pl.BlockSpec(shape, lambda *_: (0,) * nd, pipeline_mode=pl.Buffered(1))


def _even_finish(hf, hb, o, y_lat, y_ctx, x, ctx, mod, g_head, w_out, g2, w1, w2, nlat):
    b, t, d = x.shape
    ttot = t + ctx.shape[1]
    tm = ROW_TILE
    nt = ttot // tm
    row = lambda w: pl.BlockSpec((1, tm, w), lambda bi, i: (bi, i, 0))
    return pl.pallas_call(
        functools.partial(_even_finish_kernel, nlat=nlat), name="even_finish_mlp",
        grid=(b, nt),
        in_specs=[pl.BlockSpec((1, D_HEADS, tm), lambda bi, i: (bi, 0, i)),
                  pl.BlockSpec((1, D_HEADS, tm), lambda bi, i: (bi, 0, i)),
                  row(D_HEADS),
                  pl.BlockSpec((1, tm, 256), lambda bi, i: (bi, jnp.minimum(i, nlat - 1), 0)),
                  pl.BlockSpec((1, tm, 256), lambda bi, i: (bi, jnp.maximum(i - nlat, 0), 0))]
        + _lat_ctx_specs(d, tm, nlat)
        + [_mod_spec(b, nlat), _const_spec((1, D_HEADS)), _resident_spec((d, d)),
           _const_spec((1, d)), _resident_spec(w1.shape), _resident_spec(w2.shape)],
        out_specs=row(d),
        out_shape=jax.ShapeDtypeStruct((b, ttot, d), F32),
        compiler_params=_cparams("parallel", "parallel"),
    )(hf, hb, o, y_lat, y_ctx, x, ctx, mod, g_head, w_out, g2, w1, w2)


def _swap_halves(x):
    lane = lax.broadcasted_iota(jnp.int32, x.shape, 1)
    return jnp.where(lane % 64 < 32, pltpu.roll(x, 96, axis=1), pltpu.roll(x, 32, axis=1))


def _proj_odd_kernel(x_ref, mod_ref, g_ref, w_ref, cos_ref, sin_ref, q_ref, k_ref, vt_ref, glu_ref):
    m = mod_ref[0]
    h = _norm_mod(x_ref[0], g_ref[...], m[0:1], m[1:2]).astype(BF16)
    u = _dot(h, w_ref[...])
    cos = cos_ref[...]
    sin = sin_ref[...]
    for hh in range(HEADS):
        sl = slice(hh * HEAD_W, (hh + 1) * HEAD_W)
        qh = u[:, sl]
        kh = u[:, D_HEADS + hh * HEAD_W:D_HEADS + (hh + 1) * HEAD_W]
        q_ref[0, :, sl] = ((qh * cos + _swap_halves(qh) * sin) * Q_SCALE_LOG2).astype(BF16)
        k_ref[0, :, sl] = (kh * cos + _swap_halves(kh) * sin).astype(BF16)
    vt_ref[0] = u[:, 2 * D_HEADS:3 * D_HEADS].T.astype(BF16)
    a = u[:, 3 * D_HEADS:3 * D_HEADS + D_SIDE]
    gte = u[:, 3 * D_HEADS + D_SIDE:]
    glu_ref[0] = a * _sigmoid(gte)


def _rope_tables(t, ctx_len):
    rows = t // GRID_W
    row = np.repeat(np.arange(rows, dtype=np.float64), GRID_W)
    col = np.tile(np.arange(GRID_W, dtype=np.float64), rows)
    n_freq = DIFF_DH // 4
    inv = ROPE_BASE ** (-np.arange(n_freq, dtype=np.float64) / n_freq)
    ang = np.concatenate([row[:, None] * inv, col[:, None] * inv], axis=-1)
    cos, sin = np.cos(ang), np.sin(ang)
    cos = np.concatenate([cos, cos, cos, cos], axis=-1)
    sin = np.concatenate([-sin, sin, -sin, sin], axis=-1)
    cos = np.concatenate([cos, np.ones((ctx_len, 128))], axis=0)
    sin = np.concatenate([sin, np.zeros((ctx_len, 128))], axis=0)
    return jnp.asarray(cos, F32), jnp.asarray(sin, F32)


def _proj_odd(x, mod, g, w_all, cos, sin, nlat):
    b, ttot, d = x.shape
    tm = ROW_TILE
    nt = ttot // tm
    row = lambda w: pl.BlockSpec((1, tm, w), lambda bi, i: (bi, i, 0))
    tab = pl.BlockSpec((tm, 128), lambda bi, i: (i, 0))
    return pl.pallas_call(
        _proj_odd_kernel, name="proj_odd",
        grid=(b, nt),
        in_specs=[row(d), _mod_spec(b, nlat), _const_spec((1, d)), _const_spec(w_all.shape), tab, tab],
        out_specs=[row(D_HEADS), row(D_HEADS),
                   pl.BlockSpec((1, D_HEADS, tm), lambda bi, i: (bi, 0, i)), row(D_SIDE)],
        out_shape=[jax.ShapeDtypeStruct((b, ttot, D_HEADS), BF16)] * 2
        + [jax.ShapeDtypeStruct((b, D_HEADS, ttot), BF16), jax.ShapeDtypeStruct((b, ttot, D_SIDE), F32)],
        compiler_params=_cparams("parallel", "parallel"),
    )(x, mod, g, w_all, cos, sin)


def _attn_kernel(q_ref, k_ref, vt_ref, lam_ref, gs_ref, o_ref, acc0, acc1, *bufs, tk, nk, lam_init):
    q = q_ref[0]
    q0 = q[:, :DIFF_DH]
    q1 = q[:, DIFF_DH:]
    tq = q.shape[0]
    ones = jnp.ones((ATTN_SUM_ROWS, tk), BF16)
    acc0[...] = jnp.zeros_like(acc0)
    acc1[...] = jnp.zeros_like(acc1)

    def online(st, m_old, acc, vext):
        m_new = jnp.maximum(m_old, jnp.max(st, axis=0, keepdims=True))
        p = jnp.exp2(st - m_new).astype(BF16)
        acc[...] = jnp.exp2(m_old - m_new) * acc[...] + _dot(vext, p)
        return m_new

    def scores(j, s_buf):
        kk = k_ref[0, j * tk:(j + 1) * tk, :]
        s_buf[0] = _dot_nt(kk[:, :DIFF_DH], q0)
        s_buf[1] = _dot_nt(kk[:, DIFF_DH:], q1)

    def consume(j, s_buf, m0, m1):
        vext = jnp.concatenate([vt_ref[0, :, j * tk:(j + 1) * tk], ones], axis=0)
        return online(s_buf[0], m0, acc0, vext), online(s_buf[1], m1, acc1, vext)

    init = jnp.full((1, tq), NEG, F32)
    carry = (init, init)
    ahead = len(bufs) - 1
    for j in range(min(ahead, nk)):
        scores(j, bufs[j % len(bufs)])
    for j in range(nk):
        if j + ahead < nk:
            scores(j + ahead, bufs[(j + ahead) % len(bufs)])
        carry = consume(j, bufs[j % len(bufs)], *carry)
    lp = lam_ref[...]
    lam = (jnp.exp(jnp.sum(lp[0:1] * lp[1:2], axis=1, keepdims=True))
           - jnp.exp(jnp.sum(lp[2:3] * lp[3:4], axis=1, keepdims=True)) + lam_init)
    a0 = acc0[...]
    a1 = acc1[...]
    o = a0[:HEAD_W] / a0[HEAD_W:HEAD_W + 1] - lam * (a1[:HEAD_W] / a1[HEAD_W:HEAD_W + 1])
    o = o * lax.rsqrt(jnp.mean(o * o, axis=0, keepdims=True) + EPS)
    o_ref[0] = (o.T * gs_ref[...] * (1.0 - lam_init)).astype(BF16)


def _diff_attention(q, k, v, lam_p, g_sub, t, lam_init):
    b, ttot, _ = q.shape
    tq = ATTN_TQ
    tk = max(n for n in range(256, ATTN_TK + 1, 256) if ttot % n == 0)
    return pl.pallas_call(
        functools.partial(_attn_kernel, tk=tk, nk=ttot // tk, lam_init=lam_init), name="diff_attn",
        grid=(b, HEADS, t // tq),
        in_specs=[pl.BlockSpec((1, tq, HEAD_W), lambda bi, h, i: (bi, i, h)),
                  pl.BlockSpec((1, ttot, HEAD_W), lambda bi, h, i: (bi, 0, h)),
                  pl.BlockSpec((1, HEAD_W, ttot), lambda bi, h, i: (bi, h, 0)),
                  _const_spec(lam_p.shape), _const_spec((1, HEAD_W))],
        out_specs=pl.BlockSpec((1, tq, HEAD_W), lambda bi, h, i: (bi, i, h)),
        out_shape=jax.ShapeDtypeStruct((b, t, D_HEADS), BF16),
        scratch_shapes=[pltpu.VMEM((HEAD_W + ATTN_SUM_ROWS, tq), F32)] * 2
        + [pltpu.VMEM((2, tk, tq), F32)] * ATTN_SCORE_BUFS,
        compiler_params=_cparams("parallel", "parallel", "parallel"),
    )(q, k, v, lam_p, g_sub)


def _odd_finish_kernel(o_ref, glu_ref, prev_ref, next_ref, wdw_ref, gln_ref, bln_ref, x_ref, mod_ref,
                       w_ref, g2_ref, w1_ref, w2_ref, gf_ref, out_ref, xs, *, tm, nlat):
    i = pl.program_id(1)
    hal = CONV_HALO
    xs[0:hal, :] = jnp.where(i == 0, 0.0, prev_ref[0])
    xs[hal:hal + tm, :] = glu_ref[0]
    xs[hal + tm:, :] = jnp.where(i == nlat - 1, 0.0, next_ref[0])
    wdw = wdw_ref[...]
    z = jnp.zeros((tm, D_SIDE), F32)
    off = hal - (CONV_W - 1) // 2
    for r in range(SUBLANES):
        u = None
        for a in range((off + CONV_W - 1) // SUBLANES + 1):
            kk = SUBLANES * a + r - off
            if 0 <= kk < CONV_W:
                term = xs[SUBLANES * a:SUBLANES * a + tm + SUBLANES, :] * wdw[kk:kk + 1]
                u = term if u is None else u + term
        z = z + u[r:r + tm]
    mu = jnp.mean(z, axis=-1, keepdims=True)
    zc = z - mu
    var = jnp.mean(zc * zc, axis=-1, keepdims=True)
    z = zc * lax.rsqrt(var + EPS) * gln_ref[...] + bln_ref[...]
    z = z * _sigmoid(z)
    cat = jnp.concatenate([o_ref[0], z.astype(BF16)], axis=1)
    m = mod_ref[0]
    x3 = x_ref[0] + m[2:3] * _dot(cat, w_ref[...])
    out_ref[0] = _rms(_mlp_rows(x3, m, g2_ref[...], w1_ref, w2_ref), gf_ref[...])


def _odd_finish(o, glu, w_dw, g_ln, b_ln, x, mod, w_out, g2, w1, w2, g_final, t):
    b, _, d = x.shape
    tm = LATENT_ROW_TILE
    nlat = t // tm
    r = tm // CONV_HALO
    row = lambda w: pl.BlockSpec((1, tm, w), lambda bi, i: (bi, i, 0))
    return pl.pallas_call(
        functools.partial(_odd_finish_kernel, tm=tm, nlat=nlat), name="odd_finish_mlp",
        grid=(b, nlat),
        in_specs=[row(D_HEADS), row(D_SIDE),
                  pl.BlockSpec((1, CONV_HALO, D_SIDE), lambda bi, i: (bi, jnp.maximum(i * r - 1, 0), 0)),
                  pl.BlockSpec((1, CONV_HALO, D_SIDE), lambda bi, i: (bi, (i + 1) * r, 0)),
                  _const_spec((CONV_W, D_SIDE)), _const_spec((1, D_SIDE)), _const_spec((1, D_SIDE)),
                  row(d), _mod_spec(b, nlat), _resident_spec((d, d)),
                  _const_spec((1, d)), _resident_spec(w1.shape), _resident_spec(w2.shape),
                  _const_spec((1, d))],
        out_specs=row(d),
        out_shape=jax.ShapeDtypeStruct((b, t, d), F32),
        scratch_shapes=[pltpu.VMEM((tm + 2 * CONV_HALO, D_SIDE), F32)],
        compiler_params=_cparams("parallel", "parallel"),
    )(o, glu, glu, glu, w_dw, g_ln, b_ln, x, mod, w_out, g2, w1, w2, g_final)


def kernel(x, c, ctx, c_ctx, w_mod, b_mod, g_norm, w_in_even, b_gate, w_qk_conv, g_mlstm_head,
           w_out_even, w_in_odd, lam_p, g_subln, w_dw, g_conv_ln, b_conv_ln, w_out_odd,
           w_ff1, w_ff2, g_final):
    b, t, d = x.shape
    ctx_len = ctx.shape[1]
    assert w_mod.shape[0] == 2 and d == 1024 and ctx_len == ROW_TILE and t % (FFT_N1 * 8) == 0
    nlat = t // ROW_TILE
    mod = _mod_vectors(c, c_ctx, w_mod, b_mod)

    we = w_in_even[0]
    g0 = 4 * D_HEADS
    w_gate = jnp.zeros((d, 256), F32).at[:, 0:12].set(we[:, g0:g0 + 12]).at[:, 128:140].set(we[:, g0 + 12:g0 + 24])
    w_all = jnp.concatenate([we[:, :g0], w_gate, we[:, g0 + 24:]], axis=1).astype(BF16)
    bg = jnp.zeros((1, 256), F32).at[0, 0:12].set(b_gate[0, :12]).at[0, 128:140].set(b_gate[0, 12:])
    q, k, v, o, gates, p = _proj_even(x, ctx, mod[0], g_norm[0, 0][None], w_all, bg, _channel_dft(),
                                      w_qk_conv[0], nlat)
    hf, hb = _mlstm(q, k, v, gates, t)
    y_lat, y_ctx = _fourier(p, t, ctx_len)
    x2 = _even_finish(hf, hb, o, y_lat, y_ctx, x, ctx, mod[0], g_mlstm_head[0][None],
                      w_out_even[0].astype(BF16), g_norm[0, 1][None], w_ff1[0].astype(BF16),
                      w_ff2[0].astype(BF16), nlat)

    lam_init = 0.8 - 0.6 * math.exp(-0.3 * 1)
    cos, sin = _rope_tables(t, ctx_len)
    qa, ka, va, glu = _proj_odd(x2, mod[1], g_norm[1, 0][None], w_in_odd[0].astype(BF16), cos, sin, nlat)
    oa = _diff_attention(qa, ka, va, lam_p[0], g_subln[0][None], t, lam_init)
    return _odd_finish(oa, glu, w_dw[0], g_conv_ln[0][None], b_conv_ln[0][None], x2, mod[1],
                       w_out_odd[0].astype(BF16), g_norm[1, 1][None], w_ff1[1].astype(BF16),
                       w_ff2[1].astype(BF16), g_final[None], t)
```
